```python
import jax, jax.numpy as jnp
from jax import lax
import numpy as np

D_MODEL = 1024
BATCH = 8
SEQ = 4096
DEPTH = 1

CHUNK = 64
Q_BLOCK = 128
MLA_HEADS = 8
MLA_NOPE = 64
MLA_ROPE = 32
MLA_QK = MLA_NOPE + MLA_ROPE
MLA_V = 64
Q_LORA = 256
KV_LORA = 128
ROPE_BASE = 10000.0
CA_HEADS = 8
CA_HEAD_DIM = 64
LEFT_CHUNKS = 8
BAND = (LEFT_CHUNKS + 1) * CHUNK
MAX_REL = 256
MLA_WIDTH = MLA_HEADS * MLA_V
CA_WIDTH = CA_HEADS * CA_HEAD_DIM
IN_WIDTHS = (Q_LORA, KV_LORA, MLA_ROPE, CA_WIDTH, CA_WIDTH, CA_WIDTH, D_MODEL, D_MODEL)
IN_WIDTH = Q_LORA + KV_LORA + MLA_ROPE + 3 * CA_WIDTH + 2 * D_MODEL
N_GROUPS = 4
EXPERTS_PER_GROUP = 8
N_EXPERTS = N_GROUPS * EXPERTS_PER_GROUP
TOP_K = 2
EXPERT_FF = 256
NORM_EPS = 1e-6
NEG_INF = -1e30
N_MOD = 6

kernel_name = 'hybrid_mla_chunkattn_hmoe_adaln'


def rms_norm(x, g):
    xf = x.astype(jnp.float32)
    y = xf * lax.rsqrt(jnp.mean(xf * xf, axis=-1, keepdims=True) + NORM_EPS)
    return (y * g.astype(jnp.float32)).astype(x.dtype)


def modulate(x, shift, scale):
    return x * (1 + scale[:, None, :]) + shift[:, None, :]


def rope(x, positions):
    half = x.shape[-1] // 2
    inv = ROPE_BASE ** (-jnp.arange(half, dtype=jnp.float32) / half)
    ang = positions.astype(jnp.float32)[..., None] * inv
    ang = ang.reshape(ang.shape[:2] + (1,) * (x.ndim - 3) + (half,))
    cos = jnp.cos(ang).astype(x.dtype)
    sin = jnp.sin(ang).astype(x.dtype)
    x1, x2 = x[..., :half], x[..., half:]
    return jnp.concatenate([x1 * cos - x2 * sin, x2 * cos + x1 * sin], axis=-1)


def mla_attend(q, k, v):
    b, s, h, _ = q.shape
    n_blocks = s // Q_BLOCK
    scale = MLA_QK ** -0.5
    key_chunk = jnp.arange(s) // CHUNK

    def one_block(i):
        qb = lax.dynamic_slice_in_dim(q, i * Q_BLOCK, Q_BLOCK, axis=1)
        sc = jnp.einsum('bqhd,bkhd->bhqk', qb, k).astype(jnp.float32) * scale
        q_chunk = (i * Q_BLOCK + jnp.arange(Q_BLOCK)) // CHUNK
        mask = key_chunk[None, :] <= q_chunk[:, None]
        sc = jnp.where(mask[None, None], sc, NEG_INF)
        p = jax.nn.softmax(sc, axis=-1).astype(v.dtype)
        return jnp.einsum('bhqk,bkhd->bqhd', p, v)

    out = lax.map(one_block, jnp.arange(n_blocks))
    return out.transpose(1, 0, 2, 3, 4).reshape(b, s, h * v.shape[-1])


def chunk_band_attend(q, k, v, rel_table):
    b, s, h, d = q.shape
    n_chunks = s // CHUNK
    scale = d ** -0.5
    pad = ((0, 0), (LEFT_CHUNKS * CHUNK, 0), (0, 0), (0, 0))
    kp = jnp.pad(k, pad)
    vp = jnp.pad(v, pad)
    rel = (jnp.arange(CHUNK)[:, None] + LEFT_CHUNKS * CHUNK) - jnp.arange(BAND)[None, :]
    idx = jnp.clip(rel, -MAX_REL, MAX_REL) + MAX_REL
    bias = rel_table.astype(jnp.float32)[:, idx]

    def one_chunk(n):
        qc = lax.dynamic_slice_in_dim(q, n * CHUNK, CHUNK, axis=1)
        kb = lax.dynamic_slice_in_dim(kp, n * CHUNK, BAND, axis=1)
        vb = lax.dynamic_slice_in_dim(vp, n * CHUNK, BAND, axis=1)
        sc = jnp.einsum('bqhd,bkhd->bhqk', qc, kb).astype(jnp.float32) * scale + bias[None]
        valid = (n * CHUNK - LEFT_CHUNKS * CHUNK + jnp.arange(BAND)) >= 0
        sc = jnp.where(valid[None, None, None, :], sc, NEG_INF)
        p = jax.nn.softmax(sc, axis=-1).astype(vb.dtype)
        return jnp.einsum('bhqk,bkhd->bqhd', p, vb)

    out = lax.map(one_chunk, jnp.arange(n_chunks))
    return out.transpose(1, 0, 2, 3, 4).reshape(b, s, h * d)


def hybrid_mixer(h, positions, w_in, g_q, w_uq, g_kv, w_uk, w_uv, rel_table, w_oa, w_ob, w_out):
    b, s, _ = h.shape
    proj = h @ w_in
    cuts = [int(v) for v in np.cumsum(IN_WIDTHS)[:-1]]
    q_lat, kv_lat, k_rope, q_b, k_b, v_b, gate_a, gate_b = jnp.split(proj, cuts, axis=-1)
    q_a = (rms_norm(q_lat, g_q) @ w_uq).reshape(b, s, MLA_HEADS, MLA_QK)
    q_a = jnp.concatenate([q_a[..., :MLA_NOPE], rope(q_a[..., MLA_NOPE:], positions)], axis=-1)
    kv_n = rms_norm(kv_lat, g_kv)
    k_nope = (kv_n @ w_uk).reshape(b, s, MLA_HEADS, MLA_NOPE)
    v_a = (kv_n @ w_uv).reshape(b, s, MLA_HEADS, MLA_V)
    k_r = rope(k_rope, positions)
    k_a = jnp.concatenate(
        [k_nope, jnp.broadcast_to(k_r[:, :, None, :], (b, s, MLA_HEADS, MLA_ROPE))], axis=-1)
    o_a = mla_attend(q_a, k_a, v_a)
    o_b = chunk_band_attend(
        q_b.reshape(b, s, CA_HEADS, CA_HEAD_DIM),
        k_b.reshape(b, s, CA_HEADS, CA_HEAD_DIM),
        v_b.reshape(b, s, CA_HEADS, CA_HEAD_DIM), rel_table)
    merged = jax.nn.sigmoid(gate_a) * (o_a @ w_oa) + jax.nn.sigmoid(gate_b) * (o_b @ w_ob)
    return merged @ w_out


def hierarchical_moe(h, w_rg, b_rg, w_re, b_re, w_gate, w_up, w_down):
    b, s, d = h.shape
    t = h.reshape(b * s, d)
    n_tok = t.shape[0]
    g_prob = jax.nn.softmax((t @ w_rg + b_rg).astype(jnp.float32), axis=-1)
    g_w, g_idx = lax.top_k(g_prob, 1)
    g_w, g_idx = g_w[:, 0], g_idx[:, 0]
    e_logits = (t @ w_re + b_re).astype(jnp.float32).reshape(n_tok, N_GROUPS, EXPERTS_PER_GROUP)
    e_sel = jnp.take_along_axis(e_logits, g_idx[:, None, None], axis=1)[:, 0]
    e_prob = jax.nn.softmax(e_sel, axis=-1)
    top_w, top_i = lax.top_k(e_prob, TOP_K)
    top_w = top_w / jnp.sum(top_w, axis=-1, keepdims=True)
    in_group = jnp.sum(jax.nn.one_hot(top_i, EXPERTS_PER_GROUP, dtype=jnp.float32) * top_w[..., None], axis=1)
    combine = (jax.nn.one_hot(g_idx, N_GROUPS, dtype=jnp.float32)[:, :, None]
               * in_group[:, None, :] * g_w[:, None, None]).astype(h.dtype)
    y = jnp.zeros_like(t)
    for g in range(N_GROUPS):
        sl = slice(g * EXPERTS_PER_GROUP, (g + 1) * EXPERTS_PER_GROUP)
        a = jnp.einsum('td,edf->tef', t, w_gate[sl])
        u = jnp.einsum('td,edf->tef', t, w_up[sl])
        hid = jax.nn.silu(a) * u * combine[:, g, :, None]
        y = y + jnp.einsum('tef,efd->td', hid, w_down[sl])
    return y.reshape(b, s, d)


def setup_inputs(seed: int = 0) -> dict:
    key = jax.random.key(seed)
    ks = iter(jax.random.split(key, 32))
    L, D = DEPTH, D_MODEL

    def w(shape, fan_in):
        return jax.random.normal(next(ks), shape, jnp.float32) * (fan_in ** -0.5)

    def gain(shape):
        return 1.0 + 0.02 * jax.random.normal(next(ks), shape, jnp.float32)

    def small(shape, s=0.01):
        return s * jax.random.normal(next(ks), shape, jnp.float32)

    x = jax.random.normal(next(ks), (BATCH, SEQ, D), jnp.float32)
    c = jax.random.normal(next(ks), (BATCH, D), jnp.float32)
    offsets = jax.random.randint(next(ks), (BATCH, 1), 0, 10000, dtype=jnp.int32)
    positions = offsets + jnp.arange(SEQ, dtype=jnp.int32)[None, :]
    return {
        'x': x,
        'c': c,
        'positions': positions,
        'w_ada': w((L, D, N_MOD * D), D),
        'b_ada': small((L, N_MOD * D), 0.02),
        'g_mix': gain((L, D)),
        'w_in': w((L, D, IN_WIDTH), D),
        'g_q': gain((L, Q_LORA)),
        'w_uq': w((L, Q_LORA, MLA_HEADS * MLA_QK), Q_LORA),
        'g_kv': gain((L, KV_LORA)),
        'w_uk': w((L, KV_LORA, MLA_HEADS * MLA_NOPE), KV_LORA),
        'w_uv': w((L, KV_LORA, MLA_WIDTH), KV_LORA),
        'rel_bias': small((L, CA_HEADS, 2 * MAX_REL + 1), 0.5),
        'w_oa': w((L, MLA_WIDTH, D), MLA_WIDTH),
        'w_ob': w((L, CA_WIDTH, D), CA_WIDTH),
        'w_out': w((L, D, D), D),
        'g_ffn': gain((L, D)),
        'w_rg': w((L, D, N_GROUPS), D),
        'b_rg': small((L, N_GROUPS)),
        'w_re': w((L, D, N_EXPERTS), D),
        'b_re': small((L, N_EXPERTS)),
        'w_gate': w((L, N_EXPERTS, D, EXPERT_FF), D),
        'w_up': w((L, N_EXPERTS, D, EXPERT_FF), D),
        'w_down': w((L, N_EXPERTS, EXPERT_FF, D), EXPERT_FF),
        'g_final': gain((D,)),
    }


def reference(x, c, positions, w_ada, b_ada, g_mix, w_in, g_q, w_uq, g_kv, w_uk, w_uv,
              rel_bias, w_oa, w_ob, w_out, g_ffn, w_rg, b_rg, w_re, b_re,
              w_gate, w_up, w_down, g_final):
    c_act = jax.nn.silu(c)
    for l in range(DEPTH):
        mod = c_act @ w_ada[l] + b_ada[l]
        sh1, sc1, gt1, sh2, sc2, gt2 = jnp.split(mod, N_MOD, axis=-1)
        h = modulate(rms_norm(x, g_mix[l]), sh1, sc1)
        mix = hybrid_mixer(h, positions, w_in[l], g_q[l], w_uq[l], g_kv[l], w_uk[l], w_uv[l],
                           rel_bias[l], w_oa[l], w_ob[l], w_out[l])
        x = x + gt1[:, None, :] * mix
        h = modulate(rms_norm(x, g_ffn[l]), sh2, sc2)
        ffn = hierarchical_moe(h, w_rg[l], b_rg[l], w_re[l], b_re[l], w_gate[l], w_up[l], w_down[l])
        x = x + gt2[:, None, :] * ffn
    return rms_norm(x, g_final)
```

```python
import functools

import numpy as np
import jax
import jax.numpy as jnp
from jax import lax
from jax.experimental import pallas as pl
from jax.experimental.pallas import tpu as pltpu

F32 = jnp.float32
BF16 = jnp.bfloat16

NORM_EPS = 1e-6
NEG_INF = -1e30
ROPE_BASE = 10000.0
CHUNK = 64
LEFT_CHUNKS = 8
MAX_REL = 256
N_GROUPS = 4
EXPERTS_PER_GROUP = 8
LANES = 128
HEAD_DIM = 64
ROPE_DIM = 32

VMEM_LIMIT = 56 * 1024 * 1024


def _cparams(sem):
    return pltpu.CompilerParams(dimension_semantics=sem, vmem_limit_bytes=VMEM_LIMIT)


def _dot(a, b):
    return jnp.dot(a, b, preferred_element_type=F32)


def _dot_nt(a, b):
    return lax.dot_general(a, b, (((1,), (1,)), ((), ())), preferred_element_type=F32)


def _rms(x, g):
    return x * lax.rsqrt(jnp.mean(x * x, axis=-1, keepdims=True) + NORM_EPS) * g


def _ada_kernel(c_ref, w_ref, b_ref, o_ref):
    c = c_ref[...]
    o_ref[...] = _dot(c * jax.nn.sigmoid(c), w_ref[...]) + b_ref[...]


def _ada(c, w_ada, b_ada):
    bsz, d = c.shape
    n = w_ada.shape[1]
    return pl.pallas_call(
        _ada_kernel,
        grid=(n // d,),
        in_specs=[pl.BlockSpec((bsz, d), lambda j: (0, 0)),
                  pl.BlockSpec((d, d), lambda j: (0, j)),
                  pl.BlockSpec((1, d), lambda j: (0, j))],
        out_specs=pl.BlockSpec((bsz, d), lambda j: (0, j)),
        out_shape=jax.ShapeDtypeStruct((bsz, n), F32),
        compiler_params=_cparams(("arbitrary",)),
        name="ada",
    )(c, w_ada, b_ada.reshape(1, n))


def _proj_kernel(x_ref, pos_ref, mod_ref, gmix_ref, wbig_ref, gq_ref, wq_ref, gkv_ref, wkv_ref,
                 inv_ref, sgn_ref,
                 qa_ref, ka_ref, va_ref, qb_ref, kb_ref, vb_ref, ga_ref, gb_ref, *, q_lora, kv_lora):
    x = x_ref[...]
    h = _rms(x, gmix_ref[...]) * (1.0 + mod_ref[0, 1:2, :]) + mod_ref[0, 0:1, :]
    hb = h.astype(BF16)

    c0 = q_lora + kv_lora + LANES
    head = _dot(hb, wbig_ref[:, 0:c0])
    q_lat = head[:, 0:q_lora]
    kv_lat = head[:, q_lora:q_lora + kv_lora]
    kr_blk = head[:, q_lora + kv_lora:c0]

    ang = pos_ref[...].astype(F32) * inv_ref[...]
    live = lax.broadcasted_iota(jnp.int32, (1, LANES), 1) < 2 * ROPE_DIM
    cos_t = jnp.where(live, jnp.cos(ang), 0.0)
    sin_t = jnp.where(live, jnp.sin(ang) * sgn_ref[...], 0.0)

    hw = qb_ref.shape[1]
    qn = _rms(q_lat, gq_ref[...]).astype(BF16)
    q_all = _dot(qn, wq_ref[...])
    kvn = _rms(kv_lat, gkv_ref[...]).astype(BF16)
    kv = _dot(kvn, wkv_ref[...])
    k_rot = (kr_blk * cos_t + pltpu.roll(kr_blk, LANES // 2, 1) * sin_t).astype(BF16)
    for p in range(hw // LANES):
        lo, hi = p * LANES, (p + 1) * LANES
        qa_ref[:, 2 * lo:2 * lo + LANES] = q_all[:, lo:hi].astype(BF16)
        qa_ref[:, 2 * lo + LANES:2 * hi] = (
            q_all[:, hw + lo:hw + hi] * cos_t + q_all[:, 2 * hw + lo:2 * hw + hi] * sin_t).astype(BF16)
        ka_ref[:, 2 * lo:2 * lo + LANES] = kv[:, lo:hi].astype(BF16)
        ka_ref[:, 2 * lo + LANES:2 * hi] = k_rot
    va_ref[...] = kv[:, hw:2 * hw].astype(BF16)

    qb_ref[...] = _dot(hb, wbig_ref[:, c0:c0 + hw]).astype(BF16)
    kb_ref[...] = _dot(hb, wbig_ref[:, c0 + hw:c0 + 2 * hw]).astype(BF16)
    vb_ref[...] = _dot(hb, wbig_ref[:, c0 + 2 * hw:c0 + 3 * hw]).astype(BF16)
    d = x.shape[1]
    g0 = c0 + 3 * hw
    ga_ref[...] = jax.nn.sigmoid(_dot(hb, wbig_ref[:, g0:g0 + d])).astype(BF16)
    gb_ref[...] = jax.nn.sigmoid(_dot(hb, wbig_ref[:, g0 + d:g0 + 2 * d])).astype(BF16)


def _proj(x2, pos2, mod3, g_mix, wbig, g_q, wq, g_kv, wkv, seq, tm):
    t, d = x2.shape
    q_lora, kv_lora = g_q.shape[1], g_kv.shape[1]
    hw = wkv.shape[1] // 2
    per_b = seq // tm
    row = lambda i: (i, 0)
    full = lambda i: (0, 0)
    inv = ROPE_BASE ** (-(np.arange(LANES) % (ROPE_DIM // 2)).astype(np.float32) / (ROPE_DIM // 2))
    sgn = np.where((np.arange(LANES) % ROPE_DIM) < ROPE_DIM // 2, -1.0, 1.0).astype(np.float32)
    outs = [jax.ShapeDtypeStruct((t, 2 * hw), BF16), jax.ShapeDtypeStruct((t, 2 * hw), BF16)]
    outs += [jax.ShapeDtypeStruct((t, hw), BF16)] * 4
    outs += [jax.ShapeDtypeStruct((t, d), BF16)] * 2
    return pl.pallas_call(
        functools.partial(_proj_kernel, q_lora=q_lora, kv_lora=kv_lora),
        grid=(t // tm,),
        in_specs=[pl.BlockSpec((tm, d), row),
                  pl.BlockSpec((tm, 1), row),
                  pl.BlockSpec((1,) + mod3.shape[1:], lambda i: (i // per_b, 0, 0)),
                  pl.BlockSpec(g_mix.shape, full),
                  pl.BlockSpec(wbig.shape, full),
                  pl.BlockSpec(g_q.shape, full),
                  pl.BlockSpec(wq.shape, full),
                  pl.BlockSpec(g_kv.shape, full),
                  pl.BlockSpec(wkv.shape, full),
                  pl.BlockSpec((1, LANES), full),
                  pl.BlockSpec((1, LANES), full)],
        out_specs=[pl.BlockSpec((tm, o.shape[1]), row) for o in outs],
        out_shape=outs,
        compiler_params=_cparams(("parallel",)),
        name="proj",
    )(x2, pos2, mod3, g_mix, wbig, g_q, wq, g_kv, wkv, jnp.asarray(inv).reshape(1, LANES),
      jnp.asarray(sgn).reshape(1, LANES))


def _mla_kernel(q_ref, k_ref, v_ref, o_ref, *, tq):
    i = pl.program_id(2)
    q = q_ref[...]
    lane = lax.broadcasted_iota(jnp.int32, (1, 2 * LANES), 1)
    first = (lane < HEAD_DIM) | ((lane >= LANES) & (lane < LANES + ROPE_DIM))
    second = ((lane >= HEAD_DIM) & (lane < LANES)) | ((lane >= LANES + ROPE_DIM) & (lane < LANES + 2 * ROPE_DIM))
    zero = jnp.zeros_like(q)
    qs = (jnp.where(first, q, zero), jnp.where(second, q, zero))

    def step(k, v, state, mask):
        new = []
        for qh, (m, l, acc) in zip(qs, state):
            s = _dot_nt(qh, k)
            if mask is not None:
                s = jnp.where(mask, s, NEG_INF)
            m_new = jnp.maximum(m, jnp.max(s, axis=-1, keepdims=True))
            alpha = jnp.exp(m - m_new)
            p = jnp.exp(s - m_new)
            l_new = alpha * l + jnp.sum(p, axis=-1, keepdims=True)
            acc_new = alpha * acc + _dot(p.astype(BF16), v)
            new.append((m_new, l_new, acc_new))
        return tuple(new)

    def body(j, state):
        start = pl.multiple_of(j * tq, tq)
        return step(k_ref[pl.ds(start, tq), :], v_ref[pl.ds(start, tq), :], state, None)

    init = tuple((jnp.full((tq, 1), NEG_INF, F32), jnp.zeros((tq, 1), F32), jnp.zeros((tq, LANES), F32))
                 for _ in range(2))
    state = lax.fori_loop(0, i, body, init)
    r = lax.broadcasted_iota(jnp.int32, (tq, tq), 0) // CHUNK
    c = lax.broadcasted_iota(jnp.int32, (tq, tq), 1) // CHUNK
    start = pl.multiple_of(i * tq, tq)
    state = step(k_ref[pl.ds(start, tq), :], v_ref[pl.ds(start, tq), :], state, c <= r)
    (_, l0, a0), (_, l1, a1) = state
    out_lane = lax.broadcasted_iota(jnp.int32, (1, LANES), 1)
    o_ref[...] = jnp.where(out_lane < HEAD_DIM, a0 * (1.0 / l0), a1 * (1.0 / l1)).astype(o_ref.dtype)


def _mla(qa, ka, va, bsz, seq, tq):
    t, hw = va.shape
    pairs = hw // LANES
    nq = seq // tq
    return pl.pallas_call(
        functools.partial(_mla_kernel, tq=tq),
        grid=(bsz, pairs, nq),
        in_specs=[pl.BlockSpec((tq, 2 * LANES), lambda b, p, i: (b * nq + i, p)),
                  pl.BlockSpec((seq, 2 * LANES), lambda b, p, i: (b, p)),
                  pl.BlockSpec((seq, LANES), lambda b, p, i: (b, p))],
        out_specs=pl.BlockSpec((tq, LANES), lambda b, p, i: (b * nq + i, p)),
        out_shape=jax.ShapeDtypeStruct((t, hw), BF16),
        compiler_params=_cparams(("parallel", "parallel", "arbitrary")),
        name="mla",
    )(qa, ka, va)


def _band_kernel(q_ref, k_ref, v_ref, bias_ref, o_ref, *, tq, nblk):
    i = pl.program_id(2)
    q = q_ref[...]
    lane = lax.broadcasted_iota(jnp.int32, (1, LANES), 1)
    zero = jnp.zeros_like(q)
    qs = (jnp.where(lane < HEAD_DIM, q, zero), jnp.where(lane >= HEAD_DIM, q, zero))
    blocks = []
    for j in range(nblk):
        blk = i - (nblk - 1) + j
        start = pl.multiple_of(jnp.maximum(blk, 0) * tq, tq)
        blocks.append((blk >= 0, k_ref[pl.ds(start, tq), :], v_ref[pl.ds(start, tq), :]))
    outs = []
    for h, qh in enumerate(qs):
        ss = []
        for j, (valid, k, _) in enumerate(blocks):
            s = _dot_nt(qh, k) + bias_ref[0, h, :, j * tq:(j + 1) * tq]
            ss.append(jnp.where(valid, s, NEG_INF))
        m = functools.reduce(jnp.maximum, [jnp.max(s, axis=-1, keepdims=True) for s in ss])
        ps = [jnp.exp(s - m) for s in ss]
        l = functools.reduce(jnp.add, [jnp.sum(p, axis=-1, keepdims=True) for p in ps])
        acc = functools.reduce(jnp.add, [_dot(p.astype(BF16), blk[2]) for p, blk in zip(ps, blocks)])
        outs.append(acc * (1.0 / l))
    o_ref[...] = jnp.where(lane < HEAD_DIM, outs[0], outs[1]).astype(o_ref.dtype)


def _band_bias(rel_table, tq, nblk):
    r = np.arange(tq)[:, None]
    k = np.arange(nblk * tq)[None, :]
    rel = r - k + (nblk - 1) * tq
    idx = np.clip(rel, -MAX_REL, MAX_REL) + MAX_REL
    qc, kc = r // CHUNK, k // CHUNK
    shift = (nblk - 1) * tq // CHUNK - LEFT_CHUNKS
    band = (kc - shift >= qc) & (kc - shift <= qc + LEFT_CHUNKS)
    bias = jnp.where(band[None], rel_table.astype(F32)[:, idx], NEG_INF)
    return bias.reshape(rel_table.shape[0] // 2, 2, tq, nblk * tq)


def _band(qb, kb, vb, bias, bsz, seq, tq, nblk):
    t, hw = qb.shape
    pairs = hw // LANES
    nq = seq // tq
    return pl.pallas_call(
        functools.partial(_band_kernel, tq=tq, nblk=nblk),
        grid=(bsz, pairs, nq),
        in_specs=[pl.BlockSpec((tq, LANES), lambda b, p, i: (b * nq + i, p)),
                  pl.BlockSpec((seq, LANES), lambda b, p, i: (b, p)),
                  pl.BlockSpec((seq, LANES), lambda b, p, i: (b, p)),
                  pl.BlockSpec((1,) + bias.shape[1:], lambda b, p, i: (p, 0, 0, 0))],
        out_specs=pl.BlockSpec((tq, LANES), lambda b, p, i: (b * nq + i, p)),
        out_shape=jax.ShapeDtypeStruct((t, hw), BF16),
        compiler_params=_cparams(("parallel", "parallel", "arbitrary")),
        name="band",
    )(qb, kb, vb, bias)


def _merge_kernel(oa_ref, ob_ref, ga_ref, gb_ref, x_ref, mod_ref, woa_ref, wob_ref, wout_ref, gffn_ref,
                  wrh_ref, wrl_ref, br_ref, x1_ref, h2_ref, comb_ref):
    merged = (ga_ref[...].astype(F32) * _dot(oa_ref[...], woa_ref[...])
              + gb_ref[...].astype(F32) * _dot(ob_ref[...], wob_ref[...]))
    mix = _dot(merged.astype(BF16), wout_ref[...])
    x1 = x_ref[...] + mod_ref[0, 2:3, :] * mix
    x1_ref[...] = x1
    h2 = _rms(x1, gffn_ref[...]) * (1.0 + mod_ref[0, 4:5, :]) + mod_ref[0, 3:4, :]
    h2_ref[...] = h2.astype(BF16)

    h_hi = h2.astype(BF16)
    h_lo = (h2 - h_hi.astype(F32)).astype(BF16)
    logits = (_dot(h_hi, wrh_ref[...]) + _dot(h_lo, wrh_ref[...]) + _dot(h_hi, wrl_ref[...])) + br_ref[...]

    lane = lax.broadcasted_iota(jnp.int32, logits.shape, 1)
    far = jnp.int32(LANES)
    is_g = lane < N_GROUPS
    gl = jnp.where(is_g, logits, NEG_INF)
    g_max = jnp.max(gl, axis=-1, keepdims=True)
    g_w = 1.0 / jnp.sum(jnp.where(is_g, jnp.exp(gl - g_max), 0.0), axis=-1, keepdims=True)
    g_idx = jnp.min(jnp.where(gl == g_max, lane, far), axis=-1, keepdims=True)
    lo = N_GROUPS + EXPERTS_PER_GROUP * g_idx
    el = jnp.where((lane >= lo) & (lane < lo + EXPERTS_PER_GROUP), logits, NEG_INF)
    e1 = jnp.max(el, axis=-1, keepdims=True)
    i1 = jnp.min(jnp.where(el == e1, lane, far), axis=-1, keepdims=True)
    el2 = jnp.where(lane == i1, NEG_INF, el)
    e2 = jnp.max(el2, axis=-1, keepdims=True)
    i2 = jnp.min(jnp.where(el2 == e2, lane, far), axis=-1, keepdims=True)
    ratio = jnp.exp(e2 - e1)
    w1 = g_w / (1.0 + ratio)
    w2 = g_w * ratio / (1.0 + ratio)
    comb_ref[...] = jnp.where(lane == i1, w1, 0.0) + jnp.where(lane == i2, w2, 0.0)


def _merge(oa, ob, ga, gb, x2, mod3, woa, wob, wout, g_ffn, wrh, wrl, br, seq, tm):
    t, d = x2.shape
    hw = oa.shape[1]
    per_b = seq // tm
    row = lambda i: (i, 0)
    full = lambda i: (0, 0)
    return pl.pallas_call(
        _merge_kernel,
        grid=(t // tm,),
        in_specs=[pl.BlockSpec((tm, hw), row), pl.BlockSpec((tm, hw), row),
                  pl.BlockSpec((tm, d), row), pl.BlockSpec((tm, d), row),
                  pl.BlockSpec((tm, d), row),
                  pl.BlockSpec((1,) + mod3.shape[1:], lambda i: (i // per_b, 0, 0)),
                  pl.BlockSpec(woa.shape, full), pl.BlockSpec(wob.shape, full),
                  pl.BlockSpec(wout.shape, full), pl.BlockSpec(g_ffn.shape, full),
                  pl.BlockSpec(wrh.shape, full), pl.BlockSpec(wrl.shape, full),
                  pl.BlockSpec(br.shape, full)],
        out_specs=[pl.BlockSpec((tm, d), row), pl.BlockSpec((tm, d), row), pl.BlockSpec((tm, LANES), row)],
        out_shape=[jax.ShapeDtypeStruct((t, d), F32), jax.ShapeDtypeStruct((t, d), BF16),
                   jax.ShapeDtypeStruct((t, LANES), F32)],
        compiler_params=_cparams(("parallel",)),
        name="merge",
    )(oa, ob, ga, gb, x2, mod3, woa, wob, wout, g_ffn, wrh, wrl, br)


def _moe_kernel(h_ref, comb_ref, x1_ref, mod_ref, wg_ref, wu_ref, wd_ref, gfin_ref, o_ref, acc_ref):
    e = pl.program_id(1)

    @pl.when(e == 0)
    def _():
        acc_ref[...] = jnp.zeros_like(acc_ref)

    h = h_ref[...]
    a = _dot(h, wg_ref[0])
    u = _dot(h, wu_ref[0])
    comb = comb_ref[...]
    lane = lax.broadcasted_iota(jnp.int32, comb.shape, 1)
    cw = jnp.sum(jnp.where(lane == e + N_GROUPS, comb, 0.0), axis=-1, keepdims=True)
    hid = (a * jax.nn.sigmoid(a)) * u * cw
    acc_ref[...] += _dot(hid.astype(BF16), wd_ref[0])

    @pl.when(e == pl.num_programs(1) - 1)
    def _():
        x2 = x1_ref[...] + mod_ref[0, 5:6, :] * acc_ref[...]
        o_ref[...] = _rms(x2, gfin_ref[...])


def _moe(h2, comb, x1, mod3, wg, wu, wd, g_final, seq, tm):
    t, d = x1.shape
    n_exp, _, ff = wg.shape
    per_b = seq // tm
    return pl.pallas_call(
        _moe_kernel,
        grid=(t // tm, n_exp),
        in_specs=[pl.BlockSpec((tm, d), lambda i, e: (i, 0)),
                  pl.BlockSpec((tm, LANES), lambda i, e: (i, 0)),
                  pl.BlockSpec((tm, d), lambda i, e: (i, 0)),
                  pl.BlockSpec((1,) + mod3.shape[1:], lambda i, e: (i // per_b, 0, 0)),
                  pl.BlockSpec((1, d, ff), lambda i, e: (e, 0, 0)),
                  pl.BlockSpec((1, d, ff), lambda i, e: (e, 0, 0)),
                  pl.BlockSpec((1, ff, d), lambda i, e: (e, 0, 0)),
                  pl.BlockSpec((1, d), lambda i, e: (0, 0))],
        out_specs=pl.BlockSpec((tm, d), lambda i, e: (i, 0)),
        out_shape=jax.ShapeDtypeStruct((t, d), F32),
        scratch_shapes=[pltpu.VMEM((tm, d), F32)],
        compiler_params=_cparams(("parallel", "arbitrary")),
        name="moe",
    )(h2, comb, x1, mod3, wg, wu, wd, g_final)


def _layout_weights(w_in, w_uq, w_uk, w_uv, heads, q_lora, kv_lora, d):
    hw = heads * HEAD_DIM
    cuts = np.cumsum([q_lora, kv_lora, ROPE_DIM, hw, hw, hw, d, d])
    w_qlat, w_kvlat, w_kr, w_qb, w_kb, w_vb, w_ga, w_gb = jnp.split(w_in, [int(v) for v in cuts[:-1]], axis=1)
    swap = np.concatenate([np.arange(ROPE_DIM // 2, ROPE_DIM), np.arange(ROPE_DIM // 2)])
    w_kr_sw = w_kr[:, swap]
    kr_blk = jnp.concatenate([w_kr, w_kr, w_kr_sw, w_kr_sw], axis=1)
    band_scale = HEAD_DIM ** -0.5
    wbig = jnp.concatenate([w_qlat, w_kvlat, kr_blk, w_qb * band_scale, w_kb, w_vb, w_ga, w_gb], axis=1)

    qk = HEAD_DIM + ROPE_DIM
    wq3 = w_uq.reshape(q_lora, heads, qk) * (qk ** -0.5)
    nope = wq3[:, :, :HEAD_DIM].reshape(q_lora, hw)
    rope = wq3[:, :, HEAD_DIM:]
    pad = jnp.zeros((q_lora, heads // 2, LANES - 2 * ROPE_DIM), w_uq.dtype)

    def lay(r):
        return jnp.concatenate([r.reshape(q_lora, heads // 2, 2 * ROPE_DIM), pad], axis=2).reshape(q_lora, hw)

    wq = jnp.concatenate([nope, lay(rope), lay(rope[:, :, swap])], axis=1)
    wkv = jnp.concatenate([w_uk, w_uv], axis=1)
    return wbig.astype(BF16), wq.astype(BF16), wkv.astype(BF16)


def kernel(x, c, positions, w_ada, b_ada, g_mix, w_in, g_q, w_uq, g_kv, w_uk, w_uv, rel_bias, w_oa, w_ob,
           w_out, g_ffn, w_rg, b_rg, w_re, b_re, w_gate, w_up, w_down, g_final):
    bsz, seq, d = x.shape
    depth = w_ada.shape[0]
    t = bsz * seq
    heads = rel_bias.shape[1]
    q_lora, kv_lora = g_q.shape[1], g_kv.shape[1]
    tq = 256
    nblk = -(-LEFT_CHUNKS * CHUNK // tq) + 1
    assert seq % tq == 0 and tq % CHUNK == 0 and heads % 2 == 0

    x2 = x.reshape(t, d)
    pos2 = positions.reshape(t, 1)
    out = x2
    for l in range(depth):
        mod3 = _ada(c, w_ada[l], b_ada[l]).reshape(bsz, -1, d)
        wbig, wq, wkv = _layout_weights(w_in[l], w_uq[l], w_uk[l], w_uv[l], heads, q_lora, kv_lora, d)
        qa, ka, va, qb, kb, vb, ga, gb = _proj(
            x2, pos2, mod3, g_mix[l].reshape(1, d), wbig, g_q[l].reshape(1, -1), wq, g_kv[l].reshape(1, -1), wkv,
            seq, tm=256)
        oa = _mla(qa, ka, va, bsz, seq, tq)
        ob = _band(qb, kb, vb, _band_bias(rel_bias[l], tq, nblk), bsz, seq, tq, nblk)

        n_route = N_GROUPS + N_GROUPS * EXPERTS_PER_GROUP
        w_r = jnp.concatenate([w_rg[l], w_re[l], jnp.zeros((d, LANES - n_route), F32)], axis=1)
        b_r = jnp.concatenate([b_rg[l], b_re[l], jnp.zeros((LANES - n_route,), F32)]).reshape(1, LANES)
        w_r_hi = w_r.astype(BF16)
        w_r_lo = (w_r - w_r_hi.astype(F32)).astype(BF16)
        x1, h2, comb = _merge(oa, ob, ga, gb, x2, mod3, w_oa[l].astype(BF16), w_ob[l].astype(BF16),
                              w_out[l].astype(BF16), g_ffn[l].reshape(1, d), w_r_hi, w_r_lo, b_r, seq, tm=256)
        assert l == depth - 1, "multi-layer stacks need an un-normalised residual output"
        out = _moe(h2, comb, x1, mod3, w_gate[l].astype(BF16), w_up[l].astype(BF16), w_down[l].astype(BF16),
                   g_final.reshape(1, d), seq, tm=1024)
    return out.reshape(bsz, seq, d)
```

```python
import functools

import numpy as np
import jax
import jax.numpy as jnp
from jax import lax
from jax.experimental import pallas as pl
from jax.experimental.pallas import tpu as pltpu

F32 = jnp.float32
BF16 = jnp.bfloat16

NORM_EPS = 1e-6
NEG_INF = -1e30
ROPE_BASE = 10000.0
CHUNK = 64
LEFT_CHUNKS = 8
MAX_REL = 256
N_GROUPS = 4
EXPERTS_PER_GROUP = 8
LANES = 128
HEAD_DIM = 64
ROPE_DIM = 32
VT_ROWS = HEAD_DIM + 16
LOG2_E = 1.4426950408889634

VMEM_LIMIT = 56 * 1024 * 1024


def _cparams(sem):
    return pltpu.CompilerParams(dimension_semantics=sem, vmem_limit_bytes=VMEM_LIMIT)


def _dot(a, b):
    return jnp.dot(a, b, preferred_element_type=F32)


def _dot_nt(a, b):
    return lax.dot_general(a, b, (((1,), (1,)), ((), ())), preferred_element_type=F32)


def _rms(x, g):
    return x * lax.rsqrt(jnp.mean(x * x, axis=-1, keepdims=True) + NORM_EPS) * g


def _ada_kernel(c_ref, w_ref, b_ref, o_ref):
    c = c_ref[...]
    o_ref[...] = _dot(c * jax.nn.sigmoid(c), w_ref[...]) + b_ref[...]


def _ada(c, w_ada, b_ada):
    bsz, d = c.shape
    n = w_ada.shape[1]
    return pl.pallas_call(
        _ada_kernel,
        grid=(n // d,),
        in_specs=[pl.BlockSpec((bsz, d), lambda j: (0, 0)),
                  pl.BlockSpec((d, d), lambda j: (0, j)),
                  pl.BlockSpec((1, d), lambda j: (0, j))],
        out_specs=pl.BlockSpec((bsz, d), lambda j: (0, j)),
        out_shape=jax.ShapeDtypeStruct((bsz, n), F32),
        compiler_params=_cparams(("arbitrary",)),
        name="ada",
    )(c, w_ada, b_ada.reshape(1, n))


def _proj_kernel(x_ref, pos_ref, mod_ref, gmix_ref, wbig_ref, gq_ref, wq_ref, gkv_ref, wkv_ref,
                 wvt_ref, one_ref, inv_ref, sgn_ref,
                 qa_ref, ka_ref, vt_ref, qb_ref, kb_ref, vb_ref, ga_ref, gb_ref, *, q_lora, kv_lora):
    x = x_ref[...]
    h = _rms(x, gmix_ref[...]) * (1.0 + mod_ref[0, 1:2, :]) + mod_ref[0, 0:1, :]
    hb = h.astype(BF16)

    c0 = q_lora + kv_lora + LANES
    head = _dot(hb, wbig_ref[:, 0:c0])
    q_lat = head[:, 0:q_lora]
    kv_lat = head[:, q_lora:q_lora + kv_lora]
    kr_blk = head[:, q_lora + kv_lora:c0]

    ang = pos_ref[...].astype(F32) * inv_ref[...]
    live = lax.broadcasted_iota(jnp.int32, (1, LANES), 1) < 2 * ROPE_DIM
    cos_t = jnp.where(live, jnp.cos(ang), 0.0)
    sin_t = jnp.where(live, jnp.sin(ang) * sgn_ref[...], 0.0)

    hw = qb_ref.shape[1]
    qn = _rms(q_lat, gq_ref[...]).astype(BF16)
    q_all = _dot(qn, wq_ref[...])
    kvn = _rms(kv_lat, gkv_ref[...]).astype(BF16)
    k_nope = _dot(kvn, wkv_ref[...])
    k_rot = (kr_blk * cos_t + pltpu.roll(kr_blk, LANES // 2, 1) * sin_t).astype(BF16)
    for p in range(hw // LANES):
        lo, hi = p * LANES, (p + 1) * LANES
        qa_ref[:, 2 * lo:2 * lo + LANES] = q_all[:, lo:hi].astype(BF16)
        qa_ref[:, 2 * lo + LANES:2 * hi] = (
            q_all[:, hw + lo:hw + hi] * cos_t + q_all[:, 2 * hw + lo:2 * hw + hi] * sin_t).astype(BF16)
        ka_ref[:, 2 * lo:2 * lo + LANES] = k_nope[:, lo:hi].astype(BF16)
        ka_ref[:, 2 * lo + LANES:2 * hi] = k_rot
    vt_ref[0] = (_dot_nt(wvt_ref[...], kvn) + one_ref[...]).astype(BF16)

    qb_ref[...] = _dot(hb, wbig_ref[:, c0:c0 + hw]).astype(BF16)
    kb_ref[...] = _dot(hb, wbig_ref[:, c0 + hw:c0 + 2 * hw]).astype(BF16)
    vb_ref[...] = _dot(hb, wbig_ref[:, c0 + 2 * hw:c0 + 3 * hw]).astype(BF16)
    d = x.shape[1]
    g0 = c0 + 3 * hw
    ga_ref[...] = jax.nn.sigmoid(_dot(hb, wbig_ref[:, g0:g0 + d])).astype(BF16)
    gb_ref[...] = jax.nn.sigmoid(_dot(hb, wbig_ref[:, g0 + d:g0 + 2 * d])).astype(BF16)


def _proj(x2, pos2, mod3, g_mix, wbig, g_q, wq, g_kv, wkv, wvt, seq, tm):
    t, d = x2.shape
    q_lora, kv_lora = g_q.shape[1], g_kv.shape[1]
    hw = wkv.shape[1]
    per_b = seq // tm
    row = lambda i: (i, 0)
    full = lambda i: (0, 0)
    inv = ROPE_BASE ** (-(np.arange(LANES) % (ROPE_DIM // 2)).astype(np.float32) / (ROPE_DIM // 2))
    sgn = np.where((np.arange(LANES) % ROPE_DIM) < ROPE_DIM // 2, -1.0, 1.0).astype(np.float32)
    ones_col = (np.arange(wvt.shape[0]) % VT_ROWS == HEAD_DIM).astype(np.float32).reshape(-1, 1)
    outs = [jax.ShapeDtypeStruct((t, 2 * hw), BF16), jax.ShapeDtypeStruct((t, 2 * hw), BF16),
            jax.ShapeDtypeStruct((t // seq, wvt.shape[0], seq), BF16)]
    outs += [jax.ShapeDtypeStruct((t, hw), BF16)] * 3
    outs += [jax.ShapeDtypeStruct((t, d), BF16)] * 2
    out_specs = [pl.BlockSpec((tm, o.shape[1]), row) for o in outs]
    out_specs[2] = pl.BlockSpec((1, wvt.shape[0], tm), lambda i: (i // per_b, 0, i % per_b))
    return pl.pallas_call(
        functools.partial(_proj_kernel, q_lora=q_lora, kv_lora=kv_lora),
        grid=(t // tm,),
        in_specs=[pl.BlockSpec((tm, d), row),
                  pl.BlockSpec((tm, 1), row),
                  pl.BlockSpec((1,) + mod3.shape[1:], lambda i: (i // per_b, 0, 0)),
                  pl.BlockSpec(g_mix.shape, full),
                  pl.BlockSpec(wbig.shape, full),
                  pl.BlockSpec(g_q.shape, full),
                  pl.BlockSpec(wq.shape, full),
                  pl.BlockSpec(g_kv.shape, full),
                  pl.BlockSpec(wkv.shape, full),
                  pl.BlockSpec(wvt.shape, full),
                  pl.BlockSpec(ones_col.shape, full),
                  pl.BlockSpec((1, LANES), full),
                  pl.BlockSpec((1, LANES), full)],
        out_specs=out_specs,
        out_shape=outs,
        compiler_params=_cparams(("parallel",)),
        name="proj",
    )(x2, pos2, mod3, g_mix, wbig, g_q, wq, g_kv, wkv, wvt, jnp.asarray(ones_col),
      jnp.asarray(inv).reshape(1, LANES), jnp.asarray(sgn).reshape(1, LANES))


def _mla_kernel(q_ref, k_ref, vt_ref, o_ref, *, tq):
    i = pl.program_id(2)
    q = q_ref[...]
    lane = lax.broadcasted_iota(jnp.int32, (1, 2 * LANES), 1)
    first = (lane < HEAD_DIM) | ((lane >= LANES) & (lane < LANES + ROPE_DIM))
    second = ((lane >= HEAD_DIM) & (lane < LANES)) | ((lane >= LANES + ROPE_DIM) & (lane < LANES + 2 * ROPE_DIM))
    zero = jnp.zeros_like(q)
    qs = (jnp.where(first, q, zero), jnp.where(second, q, zero))

    def step(start, state, mask):
        k = k_ref[pl.ds(start, tq), :]
        sts = [_dot_nt(k, qh) for qh in qs]
        ps, ms, alphas = [], [], []
        for st, (m, _) in zip(sts, state):
            if mask is not None:
                st = jnp.where(mask, st, NEG_INF)
            m_new = jnp.maximum(m, jnp.max(st, axis=0, keepdims=True))
            ms.append(m_new)
            alphas.append(jnp.exp2(m - m_new))
            ps.append(jnp.exp2(st - m_new).astype(BF16))
        new = []
        for h, (p, m_new, alpha, (_, acc)) in enumerate(zip(ps, ms, alphas, state)):
            vt = vt_ref[0, h * VT_ROWS:(h + 1) * VT_ROWS, pl.ds(start, tq)]
            new.append((m_new, alpha * acc + _dot(vt, p)))
        return tuple(new)

    init = tuple((jnp.full((1, tq), NEG_INF, F32), jnp.zeros((VT_ROWS, tq), F32)) for _ in range(2))
    state = lax.fori_loop(0, i, lambda j, st: step(pl.multiple_of(j * tq, tq), st, None), init)
    kc = lax.broadcasted_iota(jnp.int32, (tq, tq), 0) // CHUNK
    qc = lax.broadcasted_iota(jnp.int32, (tq, tq), 1) // CHUNK
    (_, a0), (_, a1) = step(pl.multiple_of(i * tq, tq), state, kc <= qc)
    out_t = jnp.concatenate([a[0:HEAD_DIM] * (1.0 / a[HEAD_DIM:HEAD_DIM + 1]) for a in (a0, a1)], axis=0)
    o_ref[...] = out_t.T.astype(o_ref.dtype)


def _mla(qa, ka, vt, bsz, seq, tq):
    t = qa.shape[0]
    pairs = qa.shape[1] // (2 * LANES)
    nq = seq // tq
    return pl.pallas_call(
        functools.partial(_mla_kernel, tq=tq),
        grid=(bsz, pairs, nq),
        in_specs=[pl.BlockSpec((tq, 2 * LANES), lambda b, p, i: (b * nq + i, p)),
                  pl.BlockSpec((seq, 2 * LANES), lambda b, p, i: (b, p)),
                  pl.BlockSpec((1, 2 * VT_ROWS, seq), lambda b, p, i: (b, p, 0))],
        out_specs=pl.BlockSpec((tq, LANES), lambda b, p, i: (b * nq + i, p)),
        out_shape=jax.ShapeDtypeStruct((t, pairs * LANES), BF16),
        compiler_params=_cparams(("parallel", "parallel", "arbitrary")),
        name="mla",
    )(qa, ka, vt)


def _band_kernel(q_ref, k_ref, v_ref, bias_ref, o_ref, *, tq, nblk):
    i = pl.program_id(2)
    q = q_ref[...]
    lane = lax.broadcasted_iota(jnp.int32, (1, LANES), 1)
    zero = jnp.zeros_like(q)
    qs = (jnp.where(lane < HEAD_DIM, q, zero), jnp.where(lane >= HEAD_DIM, q, zero))
    blocks = []
    for j in range(nblk):
        blk = i - (nblk - 1) + j
        start = pl.multiple_of(jnp.maximum(blk, 0) * tq, tq)
        blocks.append((blk >= 0, k_ref[pl.ds(start, tq), :], v_ref[pl.ds(start, tq), :]))
    outs = []
    for h, qh in enumerate(qs):
        ss = []
        for j, (valid, k, _) in enumerate(blocks):
            s = _dot_nt(qh, k) + bias_ref[0, h, :, j * tq:(j + 1) * tq]
            ss.append(jnp.where(valid, s, NEG_INF))
        m = functools.reduce(jnp.maximum, [jnp.max(s, axis=-1, keepdims=True) for s in ss])
        ps = [jnp.exp(s - m) for s in ss]
        l = functools.reduce(jnp.add, [jnp.sum(p, axis=-1, keepdims=True) for p in ps])
        acc = functools.reduce(jnp.add, [_dot(p.astype(BF16), blk[2]) for p, blk in zip(ps, blocks)])
        outs.append(acc * (1.0 / l))
    o_ref[...] = jnp.where(lane < HEAD_DIM, outs[0], outs[1]).astype(o_ref.dtype)


def _band_bias(rel_table, tq, nblk):
    nk = nblk * tq
    ring = nk + tq
    d = np.arange(ring)
    key_minus_query = np.where(d < nk, d, d - ring)
    idx = np.clip((nblk - 1) * tq - key_minus_query, -MAX_REL, MAX_REL) + MAX_REL
    per_offset = rel_table.astype(F32)[:, idx]
    heads = rel_table.shape[0]
    toeplitz = jnp.tile(per_offset, (1, tq))[:, :tq * (ring - 1)].reshape(heads, tq, ring - 1)[:, :, :nk]
    qc = np.arange(tq)[:, None] // CHUNK
    kc = np.arange(nk)[None, :] // CHUNK - ((nblk - 1) * tq // CHUNK - LEFT_CHUNKS)
    band = (kc >= qc) & (kc <= qc + LEFT_CHUNKS)
    bias = jnp.where(band[None], toeplitz, NEG_INF)
    return bias.reshape(heads // 2, 2, tq, nk)


def _band(qb, kb, vb, bias, bsz, seq, tq, nblk):
    t, hw = qb.shape
    pairs = hw // LANES
    nq = seq // tq
    return pl.pallas_call(
        functools.partial(_band_kernel, tq=tq, nblk=nblk),
        grid=(bsz, pairs, nq),
        in_specs=[pl.BlockSpec((tq, LANES), lambda b, p, i: (b * nq + i, p)),
                  pl.BlockSpec((seq, LANES), lambda b, p, i: (b, p)),
                  pl.BlockSpec((seq, LANES), lambda b, p, i: (b, p)),
                  pl.BlockSpec((1,) + bias.shape[1:], lambda b, p, i: (p, 0, 0, 0))],
        out_specs=pl.BlockSpec((tq, LANES), lambda b, p, i: (b * nq + i, p)),
        out_shape=jax.ShapeDtypeStruct((t, hw), BF16),
        compiler_params=_cparams(("parallel", "parallel", "arbitrary")),
        name="band",
    )(qb, kb, vb, bias)


def _merge_kernel(oa_ref, ob_ref, ga_ref, gb_ref, x_ref, mod_ref, woa_ref, wob_ref, wout_ref, gffn_ref,
                  wrh_ref, wrl_ref, br_ref, x1_ref, h2_ref, comb_ref):
    merged = (ga_ref[...].astype(F32) * _dot(oa_ref[...], woa_ref[...])
              + gb_ref[...].astype(F32) * _dot(ob_ref[...], wob_ref[...]))
    mix = _dot(merged.astype(BF16), wout_ref[...])
    x1 = x_ref[...] + mod_ref[0, 2:3, :] * mix
    x1_ref[...] = x1
    h2 = _rms(x1, gffn_ref[...]) * (1.0 + mod_ref[0, 4:5, :]) + mod_ref[0, 3:4, :]
    h2_ref[...] = h2.astype(BF16)

    h_hi = h2.astype(BF16)
    h_lo = (h2 - h_hi.astype(F32)).astype(BF16)
    logits = (_dot(h_hi, wrh_ref[...]) + _dot(h_lo, wrh_ref[...]) + _dot(h_hi, wrl_ref[...])) + br_ref[...]

    lane = lax.broadcasted_iota(jnp.int32, logits.shape, 1)
    far = jnp.int32(LANES)
    is_g = lane < N_GROUPS
    gl = jnp.where(is_g, logits, NEG_INF)
    g_max = jnp.max(gl, axis=-1, keepdims=True)
    g_w = 1.0 / jnp.sum(jnp.where(is_g, jnp.exp(gl - g_max), 0.0), axis=-1, keepdims=True)
    g_idx = jnp.min(jnp.where(gl == g_max, lane, far), axis=-1, keepdims=True)
    lo = N_GROUPS + EXPERTS_PER_GROUP * g_idx
    el = jnp.where((lane >= lo) & (lane < lo + EXPERTS_PER_GROUP), logits, NEG_INF)
    e1 = jnp.max(el, axis=-1, keepdims=True)
    i1 = jnp.min(jnp.where(el == e1, lane, far), axis=-1, keepdims=True)
    el2 = jnp.where(lane == i1, NEG_INF, el)
    e2 = jnp.max(el2, axis=-1, keepdims=True)
    i2 = jnp.min(jnp.where(el2 == e2, lane, far), axis=-1, keepdims=True)
    ratio = jnp.exp(e2 - e1)
    w1 = g_w / (1.0 + ratio)
    w2 = g_w * ratio / (1.0 + ratio)
    comb_ref[...] = jnp.where(lane == i1, w1, 0.0) + jnp.where(lane == i2, w2, 0.0)


def _merge(oa, ob, ga, gb, x2, mod3, woa, wob, wout, g_ffn, wrh, wrl, br, seq, tm):
    t, d = x2.shape
    hw = oa.shape[1]
    per_b = seq // tm
    row = lambda i: (i, 0)
    full = lambda i: (0, 0)
    return pl.pallas_call(
        _merge_kernel,
        grid=(t // tm,),
        in_specs=[pl.BlockSpec((tm, hw), row), pl.BlockSpec((tm, hw), row),
                  pl.BlockSpec((tm, d), row), pl.BlockSpec((tm, d), row),
                  pl.BlockSpec((tm, d), row),
                  pl.BlockSpec((1,) + mod3.shape[1:], lambda i: (i // per_b, 0, 0)),
                  pl.BlockSpec(woa.shape, full), pl.BlockSpec(wob.shape, full),
                  pl.BlockSpec(wout.shape, full), pl.BlockSpec(g_ffn.shape, full),
                  pl.BlockSpec(wrh.shape, full), pl.BlockSpec(wrl.shape, full),
                  pl.BlockSpec(br.shape, full)],
        out_specs=[pl.BlockSpec((tm, d), row), pl.BlockSpec((tm, d), row), pl.BlockSpec((tm, LANES), row)],
        out_shape=[jax.ShapeDtypeStruct((t, d), F32), jax.ShapeDtypeStruct((t, d), BF16),
                   jax.ShapeDtypeStruct((t, LANES), F32)],
        compiler_params=_cparams(("parallel",)),
        name="merge",
    )(oa, ob, ga, gb, x2, mod3, woa, wob, wout, g_ffn, wrh, wrl, br)


def _moe_kernel(h_ref, comb_ref, x1_ref, mod_ref, wg_ref, wu_ref, wd_ref, gfin_ref, o_ref, acc_ref):
    e = pl.program_id(1)

    @pl.when(e == 0)
    def _():
        acc_ref[...] = jnp.zeros_like(acc_ref)

    h = h_ref[...]
    a = _dot(h, wg_ref[0])
    u = _dot(h, wu_ref[0])
    comb = comb_ref[...]
    lane = lax.broadcasted_iota(jnp.int32, comb.shape, 1)
    cw = jnp.sum(jnp.where(lane == e + N_GROUPS, comb, 0.0), axis=-1, keepdims=True)
    hid = (a * jax.nn.sigmoid(a)) * u * cw
    acc_ref[...] += _dot(hid.astype(BF16), wd_ref[0])

    @pl.when(e == pl.num_programs(1) - 1)
    def _():
        x2 = x1_ref[...] + mod_ref[0, 5:6, :] * acc_ref[...]
        o_ref[...] = _rms(x2, gfin_ref[...])


def _moe(h2, comb, x1, mod3, wg, wu, wd, g_final, seq, tm):
    t, d = x1.shape
    n_exp, _, ff = wg.shape
    per_b = seq // tm
    return pl.pallas_call(
        _moe_kernel,
        grid=(t // tm, n_exp),
        in_specs=[pl.BlockSpec((tm, d), lambda i, e: (i, 0)),
                  pl.BlockSpec((tm, LANES), lambda i, e: (i, 0)),
                  pl.BlockSpec((tm, d), lambda i, e: (i, 0)),
                  pl.BlockSpec((1,) + mod3.shape[1:], lambda i, e: (i // per_b, 0, 0)),
                  pl.BlockSpec((1, d, ff), lambda i, e: (e, 0, 0)),
                  pl.BlockSpec((1, d, ff), lambda i, e: (e, 0, 0)),
                  pl.BlockSpec((1, ff, d), lambda i, e: (e, 0, 0)),
                  pl.BlockSpec((1, d), lambda i, e: (0, 0))],
        out_specs=pl.BlockSpec((tm, d), lambda i, e: (i, 0)),
        out_shape=jax.ShapeDtypeStruct((t, d), F32),
        scratch_shapes=[pltpu.VMEM((tm, d), F32)],
        compiler_params=_cparams(("parallel", "arbitrary")),
        name="moe",
    )(h2, comb, x1, mod3, wg, wu, wd, g_final)


def _layout_weights(w_in, w_uq, w_uk, w_uv, heads, q_lora, kv_lora, d):
    hw = heads * HEAD_DIM
    cuts = np.cumsum([q_lora, kv_lora, ROPE_DIM, hw, hw, hw, d, d])
    w_qlat, w_kvlat, w_kr, w_qb, w_kb, w_vb, w_ga, w_gb = jnp.split(w_in, [int(v) for v in cuts[:-1]], axis=1)
    swap = np.concatenate([np.arange(ROPE_DIM // 2, ROPE_DIM), np.arange(ROPE_DIM // 2)])
    w_kr_sw = w_kr[:, swap]
    kr_blk = jnp.concatenate([w_kr, w_kr, w_kr_sw, w_kr_sw], axis=1)
    band_scale = HEAD_DIM ** -0.5
    wbig = jnp.concatenate([w_qlat, w_kvlat, kr_blk, w_qb * band_scale, w_kb, w_vb, w_ga, w_gb], axis=1)

    qk = HEAD_DIM + ROPE_DIM
    wq3 = w_uq.reshape(q_lora, heads, qk) * (qk ** -0.5 * LOG2_E)
    nope = wq3[:, :, :HEAD_DIM].reshape(q_lora, hw)
    rope = wq3[:, :, HEAD_DIM:]
    pad = jnp.zeros((q_lora, heads // 2, LANES - 2 * ROPE_DIM), w_uq.dtype)

    def lay(r):
        return jnp.concatenate([r.reshape(q_lora, heads // 2, 2 * ROPE_DIM), pad], axis=2).reshape(q_lora, hw)

    wq = jnp.concatenate([nope, lay(rope), lay(rope[:, :, swap])], axis=1)
    wvt = jnp.pad(w_uv.reshape(kv_lora, heads, HEAD_DIM).transpose(1, 2, 0),
                  ((0, 0), (0, VT_ROWS - HEAD_DIM), (0, 0))).reshape(heads * VT_ROWS, kv_lora)
    return wbig.astype(BF16), wq.astype(BF16), w_uk.astype(BF16), wvt.astype(BF16)


def kernel(x, c, positions, w_ada, b_ada, g_mix, w_in, g_q, w_uq, g_kv, w_uk, w_uv, rel_bias, w_oa, w_ob,
           w_out, g_ffn, w_rg, b_rg, w_re, b_re, w_gate, w_up, w_down, g_final):
    bsz, seq, d = x.shape
    depth = w_ada.shape[0]
    t = bsz * seq
    heads = rel_bias.shape[1]
    q_lora, kv_lora = g_q.shape[1], g_kv.shape[1]
    tq = 256
    tq_mla = 512
    nblk = -(-LEFT_CHUNKS * CHUNK // tq) + 1
    assert seq % tq == 0 and seq % tq_mla == 0 and tq % CHUNK == 0 and heads % 2 == 0

    x2 = x.reshape(t, d)
    pos2 = positions.reshape(t, 1)
    out = x2
    for l in range(depth):
        mod3 = _ada(c, w_ada[l], b_ada[l]).reshape(bsz, -1, d)
        wbig, wq, wkv, wvt = _layout_weights(w_in[l], w_uq[l], w_uk[l], w_uv[l], heads, q_lora, kv_lora, d)
        qa, ka, vt, qb, kb, vb, ga, gb = _proj(
            x2, pos2, mod3, g_mix[l].reshape(1, d), wbig, g_q[l].reshape(1, -1), wq, g_kv[l].reshape(1, -1), wkv,
            wvt, seq, tm=256)
        oa = _mla(qa, ka, vt, bsz, seq, tq_mla)
        ob = _band(qb, kb, vb, _band_bias(rel_bias[l], tq, nblk), bsz, seq, tq, nblk)

        n_route = N_GROUPS + N_GROUPS * EXPERTS_PER_GROUP
        w_r = jnp.concatenate([w_rg[l], w_re[l], jnp.zeros((d, LANES - n_route), F32)], axis=1)
        b_r = jnp.concatenate([b_rg[l], b_re[l], jnp.zeros((LANES - n_route,), F32)]).reshape(1, LANES)
        w_r_hi = w_r.astype(BF16)
        w_r_lo = (w_r - w_r_hi.astype(F32)).astype(BF16)
        x1, h2, comb = _merge(oa, ob, ga, gb, x2, mod3, w_oa[l].astype(BF16), w_ob[l].astype(BF16),
                              w_out[l].astype(BF16), g_ffn[l].reshape(1, d), w_r_hi, w_r_lo, b_r, seq, tm=256)
        assert l == depth - 1, "multi-layer stacks need an un-normalised residual output"
        out = _moe(h2, comb, x1, mod3, w_gate[l].astype(BF16), w_up[l].astype(BF16), w_down[l].astype(BF16),
                   g_final.reshape(1, d), seq, tm=1024)
    return out.reshape(bsz, seq, d)
```

```python
import functools

import numpy as np
import jax
import jax.numpy as jnp
from jax import lax
from jax.experimental import pallas as pl
from jax.experimental.pallas import tpu as pltpu

F32 = jnp.float32
BF16 = jnp.bfloat16

NORM_EPS = 1e-6
NEG_INF = -1e30
ROPE_BASE = 10000.0
CHUNK = 64
LEFT_CHUNKS = 8
MAX_REL = 256
N_GROUPS = 4
EXPERTS_PER_GROUP = 8
PAIRS_PER_GROUP = EXPERTS_PER_GROUP * (EXPERTS_PER_GROUP - 1) // 2
N_CLASSES = N_GROUPS * PAIRS_PER_GROUP
MOE_ROWS = 128
ROUTE_CLS, ROUTE_W_LO, ROUTE_W_HI = 0, 1, 2
ROUTE_ROWS = 8
LANES = 128
HEAD_DIM = 64
ROPE_DIM = 32
VT_ROWS = HEAD_DIM + 16
LOG2_E = 1.4426950408889634

VMEM_LIMIT = 56 * 1024 * 1024


def _cparams(sem):
    return pltpu.CompilerParams(dimension_semantics=sem, vmem_limit_bytes=VMEM_LIMIT)


def _dot(a, b):
    return jnp.dot(a, b, preferred_element_type=F32)


def _dot_nt(a, b):
    return lax.dot_general(a, b, (((1,), (1,)), ((), ())), preferred_element_type=F32)


def _rms(x, g):
    return x * lax.rsqrt(jnp.mean(x * x, axis=-1, keepdims=True) + NORM_EPS) * g


def _ada_kernel(c_ref, w_ref, b_ref, o_ref):
    c = c_ref[...]
    o_ref[...] = _dot(c * jax.nn.sigmoid(c), w_ref[...]) + b_ref[...]


def _ada(c, w_ada, b_ada):
    bsz, d = c.shape
    n = w_ada.shape[1]
    return pl.pallas_call(
        _ada_kernel,
        grid=(n // d,),
        in_specs=[pl.BlockSpec((bsz, d), lambda j: (0, 0)),
                  pl.BlockSpec((d, d), lambda j: (0, j)),
                  pl.BlockSpec((1, d), lambda j: (0, j))],
        out_specs=pl.BlockSpec((bsz, d), lambda j: (0, j)),
        out_shape=jax.ShapeDtypeStruct((bsz, n), F32),
        compiler_params=_cparams(("arbitrary",)),
        name="ada",
    )(c, w_ada, b_ada.reshape(1, n))


def _proj_kernel(x_ref, pos_ref, mod_ref, gmix_ref, wbig_ref, gq_ref, wq_ref, gkv_ref, wkv_ref,
                 wvt_ref, one_ref, inv_ref, sgn_ref,
                 qa_ref, ka_ref, vt_ref, qb_ref, kb_ref, vb_ref, ga_ref, gb_ref, *, q_lora, kv_lora):
    x = x_ref[...]
    h = _rms(x, gmix_ref[...]) * (1.0 + mod_ref[0, 1:2, :]) + mod_ref[0, 0:1, :]
    hb = h.astype(BF16)

    c0 = q_lora + kv_lora + LANES
    head = _dot(hb, wbig_ref[:, 0:c0])
    q_lat = head[:, 0:q_lora]
    kv_lat = head[:, q_lora:q_lora + kv_lora]
    kr_blk = head[:, q_lora + kv_lora:c0]

    ang = pos_ref[...].astype(F32) * inv_ref[...]
    live = lax.broadcasted_iota(jnp.int32, (1, LANES), 1) < 2 * ROPE_DIM
    cos_t = jnp.where(live, jnp.cos(ang), 0.0)
    sin_t = jnp.where(live, jnp.sin(ang) * sgn_ref[...], 0.0)

    hw = qb_ref.shape[1]
    qn = _rms(q_lat, gq_ref[...]).astype(BF16)
    q_all = _dot(qn, wq_ref[...])
    kvn = _rms(kv_lat, gkv_ref[...]).astype(BF16)
    k_nope = _dot(kvn, wkv_ref[...])
    k_rot = (kr_blk * cos_t + pltpu.roll(kr_blk, LANES // 2, 1) * sin_t).astype(BF16)
    for p in range(hw // LANES):
        lo, hi = p * LANES, (p + 1) * LANES
        qa_ref[:, 2 * lo:2 * lo + LANES] = q_all[:, lo:hi].astype(BF16)
        qa_ref[:, 2 * lo + LANES:2 * hi] = (
            q_all[:, hw + lo:hw + hi] * cos_t + q_all[:, 2 * hw + lo:2 * hw + hi] * sin_t).astype(BF16)
        ka_ref[:, 2 * lo:2 * lo + LANES] = k_nope[:, lo:hi].astype(BF16)
        ka_ref[:, 2 * lo + LANES:2 * hi] = k_rot
    vt_ref[0] = (_dot_nt(wvt_ref[...], kvn) + one_ref[...]).astype(BF16)

    qb_ref[...] = _dot(hb, wbig_ref[:, c0:c0 + hw]).astype(BF16)
    kb_ref[...] = _dot(hb, wbig_ref[:, c0 + hw:c0 + 2 * hw]).astype(BF16)
    vb_ref[...] = _dot(hb, wbig_ref[:, c0 + 2 * hw:c0 + 3 * hw]).astype(BF16)
    d = x.shape[1]
    g0 = c0 + 3 * hw
    ga_ref[...] = jax.nn.sigmoid(_dot(hb, wbig_ref[:, g0:g0 + d])).astype(BF16)
    gb_ref[...] = jax.nn.sigmoid(_dot(hb, wbig_ref[:, g0 + d:g0 + 2 * d])).astype(BF16)


def _proj(x2, pos2, mod3, g_mix, wbig, g_q, wq, g_kv, wkv, wvt, seq, tm):
    t, d = x2.shape
    q_lora, kv_lora = g_q.shape[1], g_kv.shape[1]
    hw = wkv.shape[1]
    per_b = seq // tm
    row = lambda i: (i, 0)
    full = lambda i: (0, 0)
    inv = ROPE_BASE ** (-(np.arange(LANES) % (ROPE_DIM // 2)).astype(np.float32) / (ROPE_DIM // 2))
    sgn = np.where((np.arange(LANES) % ROPE_DIM) < ROPE_DIM // 2, -1.0, 1.0).astype(np.float32)
    ones_col = (np.arange(wvt.shape[0]) % VT_ROWS == HEAD_DIM).astype(np.float32).reshape(-1, 1)
    outs = [jax.ShapeDtypeStruct((t, 2 * hw), BF16), jax.ShapeDtypeStruct((t, 2 * hw), BF16),
            jax.ShapeDtypeStruct((t // seq, wvt.shape[0], seq), BF16)]
    outs += [jax.ShapeDtypeStruct((t, hw), BF16)] * 3
    outs += [jax.ShapeDtypeStruct((t, d), BF16)] * 2
    out_specs = [pl.BlockSpec((tm, o.shape[1]), row) for o in outs]
    out_specs[2] = pl.BlockSpec((1, wvt.shape[0], tm), lambda i: (i // per_b, 0, i % per_b))
    return pl.pallas_call(
        functools.partial(_proj_kernel, q_lora=q_lora, kv_lora=kv_lora),
        grid=(t // tm,),
        in_specs=[pl.BlockSpec((tm, d), row),
                  pl.BlockSpec((tm, 1), row),
                  pl.BlockSpec((1,) + mod3.shape[1:], lambda i: (i // per_b, 0, 0)),
                  pl.BlockSpec(g_mix.shape, full),
                  pl.BlockSpec(wbig.shape, full),
                  pl.BlockSpec(g_q.shape, full),
                  pl.BlockSpec(wq.shape, full),
                  pl.BlockSpec(g_kv.shape, full),
                  pl.BlockSpec(wkv.shape, full),
                  pl.BlockSpec(wvt.shape, full),
                  pl.BlockSpec(ones_col.shape, full),
                  pl.BlockSpec((1, LANES), full),
                  pl.BlockSpec((1, LANES), full)],
        out_specs=out_specs,
        out_shape=outs,
        compiler_params=_cparams(("parallel",)),
        name="proj",
    )(x2, pos2, mod3, g_mix, wbig, g_q, wq, g_kv, wkv, wvt, jnp.asarray(ones_col),
      jnp.asarray(inv).reshape(1, LANES), jnp.asarray(sgn).reshape(1, LANES))


def _mla_kernel(q_ref, k_ref, vt_ref, o_ref, *, tq):
    i = pl.program_id(2)
    q = q_ref[...]
    lane = lax.broadcasted_iota(jnp.int32, (1, 2 * LANES), 1)
    first = (lane < HEAD_DIM) | ((lane >= LANES) & (lane < LANES + ROPE_DIM))
    second = ((lane >= HEAD_DIM) & (lane < LANES)) | ((lane >= LANES + ROPE_DIM) & (lane < LANES + 2 * ROPE_DIM))
    zero = jnp.zeros_like(q)
    qs = (jnp.where(first, q, zero), jnp.where(second, q, zero))

    def step(start, state, mask):
        k = k_ref[pl.ds(start, tq), :]
        sts = [_dot_nt(k, qh) for qh in qs]
        ps, ms, alphas = [], [], []
        for st, (m, _) in zip(sts, state):
            if mask is not None:
                st = jnp.where(mask, st, NEG_INF)
            m_new = jnp.maximum(m, jnp.max(st, axis=0, keepdims=True))
            ms.append(m_new)
            alphas.append(jnp.exp2(m - m_new))
            ps.append(jnp.exp2(st - m_new).astype(BF16))
        new = []
        for h, (p, m_new, alpha, (_, acc)) in enumerate(zip(ps, ms, alphas, state)):
            vt = vt_ref[0, h * VT_ROWS:(h + 1) * VT_ROWS, pl.ds(start, tq)]
            new.append((m_new, alpha * acc + _dot(vt, p)))
        return tuple(new)

    init = tuple((jnp.full((1, tq), NEG_INF, F32), jnp.zeros((VT_ROWS, tq), F32)) for _ in range(2))
    state = lax.fori_loop(0, i, lambda j, st: step(pl.multiple_of(j * tq, tq), st, None), init)
    kc = lax.broadcasted_iota(jnp.int32, (tq, tq), 0) // CHUNK
    qc = lax.broadcasted_iota(jnp.int32, (tq, tq), 1) // CHUNK
    (_, a0), (_, a1) = step(pl.multiple_of(i * tq, tq), state, kc <= qc)
    out_t = jnp.concatenate([a[0:HEAD_DIM] * (1.0 / a[HEAD_DIM:HEAD_DIM + 1]) for a in (a0, a1)], axis=0)
    o_ref[...] = out_t.T.astype(o_ref.dtype)


def _mla(qa, ka, vt, bsz, seq, tq):
    t = qa.shape[0]
    pairs = qa.shape[1] // (2 * LANES)
    nq = seq // tq
    return pl.pallas_call(
        functools.partial(_mla_kernel, tq=tq),
        grid=(bsz, pairs, nq),
        in_specs=[pl.BlockSpec((tq, 2 * LANES), lambda b, p, i: (b * nq + i, p)),
                  pl.BlockSpec((seq, 2 * LANES), lambda b, p, i: (b, p)),
                  pl.BlockSpec((1, 2 * VT_ROWS, seq), lambda b, p, i: (b, p, 0))],
        out_specs=pl.BlockSpec((tq, LANES), lambda b, p, i: (b * nq + i, p)),
        out_shape=jax.ShapeDtypeStruct((t, pairs * LANES), BF16),
        compiler_params=_cparams(("parallel", "parallel", "arbitrary")),
        name="mla",
    )(qa, ka, vt)


def _band_kernel(q_ref, k_ref, v_ref, bias_ref, o_ref, *, tq, nblk):
    i = pl.program_id(2)
    q = q_ref[...]
    lane = lax.broadcasted_iota(jnp.int32, (1, LANES), 1)
    zero = jnp.zeros_like(q)
    qs = (jnp.where(lane < HEAD_DIM, q, zero), jnp.where(lane >= HEAD_DIM, q, zero))
    blocks = []
    for j in range(nblk):
        blk = i - (nblk - 1) + j
        start = pl.multiple_of(jnp.maximum(blk, 0) * tq, tq)
        blocks.append((blk >= 0, k_ref[pl.ds(start, tq), :], v_ref[pl.ds(start, tq), :]))
    outs = []
    for h, qh in enumerate(qs):
        ss = []
        for j, (valid, k, _) in enumerate(blocks):
            s = _dot_nt(qh, k) + bias_ref[0, h, :, j * tq:(j + 1) * tq]
            ss.append(jnp.where(valid, s, NEG_INF))
        m = functools.reduce(jnp.maximum, [jnp.max(s, axis=-1, keepdims=True) for s in ss])
        ps = [jnp.exp(s - m) for s in ss]
        l = functools.reduce(jnp.add, [jnp.sum(p, axis=-1, keepdims=True) for p in ps])
        acc = functools.reduce(jnp.add, [_dot(p.astype(BF16), blk[2]) for p, blk in zip(ps, blocks)])
        outs.append(acc * (1.0 / l))
    o_ref[...] = jnp.where(lane < HEAD_DIM, outs[0], outs[1]).astype(o_ref.dtype)


def _band_bias(rel_table, tq, nblk):
    nk = nblk * tq
    ring = nk + tq
    d = np.arange(ring)
    key_minus_query = np.where(d < nk, d, d - ring)
    idx = np.clip((nblk - 1) * tq - key_minus_query, -MAX_REL, MAX_REL) + MAX_REL
    per_offset = rel_table.astype(F32)[:, idx]
    heads = rel_table.shape[0]
    toeplitz = jnp.tile(per_offset, (1, tq))[:, :tq * (ring - 1)].reshape(heads, tq, ring - 1)[:, :, :nk]
    qc = np.arange(tq)[:, None] // CHUNK
    kc = np.arange(nk)[None, :] // CHUNK - ((nblk - 1) * tq // CHUNK - LEFT_CHUNKS)
    band = (kc >= qc) & (kc <= qc + LEFT_CHUNKS)
    bias = jnp.where(band[None], toeplitz, NEG_INF)
    return bias.reshape(heads // 2, 2, tq, nk)


def _band(qb, kb, vb, bias, bsz, seq, tq, nblk):
    t, hw = qb.shape
    pairs = hw // LANES
    nq = seq // tq
    return pl.pallas_call(
        functools.partial(_band_kernel, tq=tq, nblk=nblk),
        grid=(bsz, pairs, nq),
        in_specs=[pl.BlockSpec((tq, LANES), lambda b, p, i: (b * nq + i, p)),
                  pl.BlockSpec((seq, LANES), lambda b, p, i: (b, p)),
                  pl.BlockSpec((seq, LANES), lambda b, p, i: (b, p)),
                  pl.BlockSpec((1,) + bias.shape[1:], lambda b, p, i: (p, 0, 0, 0))],
        out_specs=pl.BlockSpec((tq, LANES), lambda b, p, i: (b * nq + i, p)),
        out_shape=jax.ShapeDtypeStruct((t, hw), BF16),
        compiler_params=_cparams(("parallel", "parallel", "arbitrary")),
        name="band",
    )(qb, kb, vb, bias)


def _merge_kernel(oa_ref, ob_ref, ga_ref, gb_ref, x_ref, mod_ref, woa_ref, wob_ref, wout_ref, gffn_ref,
                  wrh_ref, wrl_ref, br_ref, x1_ref, h2x_ref, rt_ref):
    merged = (ga_ref[...].astype(F32) * _dot(oa_ref[...], woa_ref[...])
              + gb_ref[...].astype(F32) * _dot(ob_ref[...], wob_ref[...]))
    mix = _dot(merged.astype(BF16), wout_ref[...])
    x1 = x_ref[...] + mod_ref[0, 2:3, :] * mix
    x1_ref[...] = x1
    h2 = _rms(x1, gffn_ref[...]) * (1.0 + mod_ref[0, 4:5, :]) + mod_ref[0, 3:4, :]

    h_hi = h2.astype(BF16)
    h_lo = (h2 - h_hi.astype(F32)).astype(BF16)
    logits = (_dot(h_hi, wrh_ref[...]) + _dot(h_lo, wrh_ref[...]) + _dot(h_hi, wrl_ref[...])) + br_ref[...]

    lane = lax.broadcasted_iota(jnp.int32, logits.shape, 1)
    far = jnp.int32(LANES)
    is_g = lane < N_GROUPS
    gl = jnp.where(is_g, logits, NEG_INF)
    g_max = jnp.max(gl, axis=-1, keepdims=True)
    g_w = 1.0 / jnp.sum(jnp.where(is_g, jnp.exp(gl - g_max), 0.0), axis=-1, keepdims=True)
    g_idx = jnp.min(jnp.where(gl == g_max, lane, far), axis=-1, keepdims=True)
    lo = N_GROUPS + EXPERTS_PER_GROUP * g_idx
    el = jnp.where((lane >= lo) & (lane < lo + EXPERTS_PER_GROUP), logits, NEG_INF)
    e1 = jnp.max(el, axis=-1, keepdims=True)
    i1 = jnp.min(jnp.where(el == e1, lane, far), axis=-1, keepdims=True)
    el2 = jnp.where(lane == i1, NEG_INF, el)
    e2 = jnp.max(el2, axis=-1, keepdims=True)
    i2 = jnp.min(jnp.where(el2 == e2, lane, far), axis=-1, keepdims=True)
    ratio = jnp.exp(e2 - e1)
    w1 = g_w / (1.0 + ratio)
    w2 = g_w * ratio / (1.0 + ratio)
    e_lo, e_hi = jnp.minimum(i1, i2) - lo, jnp.maximum(i1, i2) - lo
    cls = (g_idx * PAIRS_PER_GROUP + jnp.right_shift(e_lo * (2 * EXPERTS_PER_GROUP - 1 - e_lo), 1)
           + (e_hi - e_lo - 1))
    first_is_lo = i1 < i2
    w_lo, w_hi = jnp.where(first_is_lo, w1, w2), jnp.where(first_is_lo, w2, w1)
    route = (jnp.where(lane == ROUTE_CLS, cls.astype(F32), 0.0) + jnp.where(lane == ROUTE_W_LO, w_lo, 0.0)
             + jnp.where(lane == ROUTE_W_HI, w_hi, 0.0))
    d = h2.shape[1]
    h2x_ref[:, 0:d] = h2
    h2x_ref[:, d:d + LANES] = route
    rt_ref[...] = route.T[0:ROUTE_ROWS, :]


def _merge(oa, ob, ga, gb, x2, mod3, woa, wob, wout, g_ffn, wrh, wrl, br, seq, tm):
    t, d = x2.shape
    hw = oa.shape[1]
    per_b = seq // tm
    row = lambda i: (i, 0)
    full = lambda i: (0, 0)
    return pl.pallas_call(
        _merge_kernel,
        grid=(t // tm,),
        in_specs=[pl.BlockSpec((tm, hw), row), pl.BlockSpec((tm, hw), row),
                  pl.BlockSpec((tm, d), row), pl.BlockSpec((tm, d), row),
                  pl.BlockSpec((tm, d), row),
                  pl.BlockSpec((1,) + mod3.shape[1:], lambda i: (i // per_b, 0, 0)),
                  pl.BlockSpec(woa.shape, full), pl.BlockSpec(wob.shape, full),
                  pl.BlockSpec(wout.shape, full), pl.BlockSpec(g_ffn.shape, full),
                  pl.BlockSpec(wrh.shape, full), pl.BlockSpec(wrl.shape, full),
                  pl.BlockSpec(br.shape, full)],
        out_specs=[pl.BlockSpec((tm, d), row), pl.BlockSpec((tm, d + LANES), row),
                   pl.BlockSpec((ROUTE_ROWS, tm), lambda i: (0, i))],
        out_shape=[jax.ShapeDtypeStruct((t, d), F32), jax.ShapeDtypeStruct((t, d + LANES), F32),
                   jax.ShapeDtypeStruct((ROUTE_ROWS, t), F32)],
        compiler_params=_cparams(("parallel",)),
        name="merge",
    )(oa, ob, ga, gb, x2, mod3, woa, wob, wout, g_ffn, wrh, wrl, br)


def _plan_kernel(rt_ref, pos_ref, cnt_ref, carry_ref, off_ref, *, tc):
    phase, i = pl.program_id(0), pl.program_id(1)
    cls = rt_ref[ROUTE_CLS:ROUTE_CLS + 1, :]
    onehot = lax.broadcasted_iota(jnp.int32, (LANES, tc), 0).astype(F32) == cls
    per_class = jnp.sum(onehot.astype(F32), axis=1, keepdims=True)

    @pl.when((phase == 0) & (i == 0))
    def _():
        cnt_ref[...] = jnp.zeros_like(cnt_ref)

    @pl.when(phase == 0)
    def _():
        cnt_ref[...] += per_class

    @pl.when((phase == 1) & (i == 0))
    def _():
        padded = jnp.floor((cnt_ref[...] + (MOE_ROWS - 1)) * (1.0 / MOE_ROWS)) * MOE_ROWS
        hi = jnp.floor(padded * (1.0 / 256.0))
        digits = [jnp.broadcast_to(v, (LANES, LANES)).astype(BF16) for v in (hi, padded - 256.0 * hi)]
        below = (lax.broadcasted_iota(jnp.int32, (LANES, LANES), 1)
                 < lax.broadcasted_iota(jnp.int32, (LANES, LANES), 0)).astype(BF16)
        off = 256.0 * _dot(below, digits[0]) + _dot(below, digits[1])
        off_ref[...] = off[:, 0:1]
        carry_ref[...] = jnp.zeros_like(carry_ref)

    @pl.when(phase == 1)
    def _():
        earlier = (lax.broadcasted_iota(jnp.int32, (tc, tc), 0)
                   < lax.broadcasted_iota(jnp.int32, (tc, tc), 1)).astype(BF16)
        rank = _dot(onehot.astype(BF16), earlier)
        row = rank + (carry_ref[...] + off_ref[...])
        pos_ref[...] = jnp.sum(jnp.where(onehot, row, 0.0), axis=0, keepdims=True).astype(jnp.int32)
        carry_ref[...] += per_class


def _plan(route_t, tc):
    t = route_t.shape[1]
    return pl.pallas_call(
        functools.partial(_plan_kernel, tc=tc),
        grid=(2, t // tc),
        in_specs=[pl.BlockSpec((ROUTE_ROWS, tc), lambda ph, i: (0, i))],
        out_specs=[pl.BlockSpec((1, tc), lambda ph, i: (0, i * ph)),
                   pl.BlockSpec((LANES, 1), lambda ph, i: (0, 0))],
        out_shape=[jax.ShapeDtypeStruct((1, t), jnp.int32), jax.ShapeDtypeStruct((LANES, 1), F32)],
        scratch_shapes=[pltpu.VMEM((LANES, 1), F32), pltpu.VMEM((LANES, 1), F32)],
        compiler_params=_cparams(("arbitrary", "arbitrary")),
        name="plan",
    )(route_t)


def _tile_tables(counts, n_tiles):
    pairs = [(lo, hi) for lo in range(EXPERTS_PER_GROUP) for hi in range(lo + 1, EXPERTS_PER_GROUP)]
    exp_lo = np.array([g * EXPERTS_PER_GROUP + lo for g in range(N_GROUPS) for lo, _ in pairs], np.int32)
    exp_hi = np.array([g * EXPERTS_PER_GROUP + hi for g in range(N_GROUPS) for _, hi in pairs], np.int32)
    cnt = counts.reshape(-1)[:N_CLASSES].astype(jnp.int32)
    tiles_per_class = (cnt + (MOE_ROWS - 1)) // MOE_ROWS
    tile_end = jnp.cumsum(tiles_per_class)
    n_used = tile_end[-1]
    j = jnp.arange(n_tiles, dtype=jnp.int32)
    tile_cls = jnp.sum(tile_end[None, :] <= jnp.minimum(j, n_used - 1)[:, None], axis=1, dtype=jnp.int32)
    last_of_class = jnp.any((j[:, None] == tile_end[None, :] - 1) & (tiles_per_class[None, :] > 0), axis=1)
    zero_fill = (last_of_class | (j >= n_used)).astype(jnp.int32)
    return (jnp.asarray(exp_lo)[tile_cls], jnp.asarray(exp_hi)[tile_cls], n_used.reshape(1).astype(jnp.int32),
            zero_fill)


def _row_copy(src, src_row, dst, dst_row, n, sem):
    return pltpu.make_async_copy(src.at[pl.ds(src_row, n)], dst.at[pl.ds(dst_row, n)], sem)


def _dispatch_kernel(pos_ref, zf_ref, h_ref, hs_ref, zero_ref, sem, *, td, n_tiles):
    i = pl.program_id(0)

    @pl.when(i == 0)
    def _():
        zero_ref[...] = jnp.zeros_like(zero_ref)

        def fill(j, n):
            @pl.when(zf_ref[j] > 0)
            def _():
                _row_copy(zero_ref, 0, hs_ref, pl.multiple_of(j * MOE_ROWS, MOE_ROWS), MOE_ROWS, sem).start()
            return n + zf_ref[j]

        n_fill = lax.fori_loop(0, n_tiles, fill, jnp.int32(0))

        def drain(_, carry):
            _row_copy(zero_ref, 0, hs_ref, 0, MOE_ROWS, sem).wait()
            return carry

        lax.fori_loop(0, n_fill, drain, 0)

    base = i * td

    def send(r, carry):
        _row_copy(h_ref, r, hs_ref, pos_ref[base + r], 1, sem).start()
        return carry

    lax.fori_loop(0, td, send, 0, unroll=8)
    _row_copy(h_ref, 0, hs_ref, 0, td, sem).wait()


def _dispatch(h2x, pos, zero_fill, n_tiles, td):
    t, w = h2x.shape
    return pl.pallas_call(
        functools.partial(_dispatch_kernel, td=td, n_tiles=n_tiles),
        grid_spec=pltpu.PrefetchScalarGridSpec(
            num_scalar_prefetch=2,
            grid=(t // td,),
            in_specs=[pl.BlockSpec((td, w), lambda i, pos, zf: (i, 0))],
            out_specs=pl.BlockSpec(memory_space=pl.ANY),
            scratch_shapes=[pltpu.VMEM((MOE_ROWS, w), F32), pltpu.SemaphoreType.DMA(())]),
        out_shape=jax.ShapeDtypeStruct((n_tiles * MOE_ROWS, w), F32),
        compiler_params=_cparams(("arbitrary",)),
        name="dispatch",
    )(pos, zero_fill, h2x)


def _moe_kernel(elo_ref, ehi_ref, nu_ref, hs_ref, wg_lo, wu_lo, wd_lo, wg_hi, wu_hi, wd_hi, y_ref):
    j = pl.program_id(0)
    d = y_ref.shape[1]

    @pl.when(j < nu_ref[0])
    def _():
        h = hs_ref[:, 0:d].astype(BF16)
        y = None
        for lane, wg, wu, wd in ((ROUTE_W_LO, wg_lo, wu_lo, wd_lo), (ROUTE_W_HI, wg_hi, wu_hi, wd_hi)):
            a = _dot(h, wg[0])
            hid = (a * jax.nn.sigmoid(a)) * _dot(h, wu[0]) * hs_ref[:, d + lane:d + lane + 1]
            part = _dot(hid.astype(BF16), wd[0])
            y = part if y is None else y + part
        y_ref[...] = y

    @pl.when(j >= nu_ref[0])
    def _():
        y_ref[...] = jnp.zeros_like(y_ref)


def _moe(hs, exp_lo, exp_hi, n_used, wg, wu, wd):
    n_tiles = hs.shape[0] // MOE_ROWS
    _, d, ff = wg.shape
    up = lambda sel: pl.BlockSpec((1, d, ff), lambda j, elo, ehi, nu: ((elo, ehi)[sel][j], 0, 0))
    down = lambda sel: pl.BlockSpec((1, ff, d), lambda j, elo, ehi, nu: ((elo, ehi)[sel][j], 0, 0))
    return pl.pallas_call(
        _moe_kernel,
        grid_spec=pltpu.PrefetchScalarGridSpec(
            num_scalar_prefetch=3,
            grid=(n_tiles,),
            in_specs=[pl.BlockSpec((MOE_ROWS, hs.shape[1]),
                                   lambda j, elo, ehi, nu: (jnp.minimum(j, nu[0] - 1), 0)),
                      up(0), up(0), down(0), up(1), up(1), down(1)],
            out_specs=pl.BlockSpec((MOE_ROWS, d), lambda j, elo, ehi, nu: (j, 0))),
        out_shape=jax.ShapeDtypeStruct((n_tiles * MOE_ROWS, d), F32),
        compiler_params=_cparams(("arbitrary",)),
        name="moe",
    )(exp_lo, exp_hi, n_used, hs, wg, wu, wd, wg, wu, wd)


def _final_kernel(pos_ref, x1_ref, mod_ref, gfin_ref, y_ref, o_ref, rows_ref, sem, *, tf):
    base = pl.program_id(0) * tf

    def fetch(r, carry):
        _row_copy(y_ref, pos_ref[base + r], rows_ref, r, 1, sem).start()
        return carry

    lax.fori_loop(0, tf, fetch, 0, unroll=8)
    _row_copy(y_ref, 0, rows_ref, 0, tf, sem).wait()
    x2 = x1_ref[...] + mod_ref[0, 5:6, :] * rows_ref[...]
    o_ref[...] = _rms(x2, gfin_ref[...])


def _final(y, pos, x1, mod3, g_final, seq, tf):
    t, d = x1.shape
    per_b = seq // tf
    return pl.pallas_call(
        functools.partial(_final_kernel, tf=tf),
        grid_spec=pltpu.PrefetchScalarGridSpec(
            num_scalar_prefetch=1,
            grid=(t // tf,),
            in_specs=[pl.BlockSpec((tf, d), lambda i, pos: (i, 0)),
                      pl.BlockSpec((1,) + mod3.shape[1:], lambda i, pos: (i // per_b, 0, 0)),
                      pl.BlockSpec((1, d), lambda i, pos: (0, 0)),
                      pl.BlockSpec(memory_space=pl.ANY)],
            out_specs=pl.BlockSpec((tf, d), lambda i, pos: (i, 0)),
            scratch_shapes=[pltpu.VMEM((tf, d), F32), pltpu.SemaphoreType.DMA(())]),
        out_shape=jax.ShapeDtypeStruct((t, d), F32),
        compiler_params=_cparams(("arbitrary",)),
        name="final",
    )(pos, x1, mod3, g_final, y)


def _layout_weights(w_in, w_uq, w_uk, w_uv, heads, q_lora, kv_lora, d):
    hw = heads * HEAD_DIM
    cuts = np.cumsum([q_lora, kv_lora, ROPE_DIM, hw, hw, hw, d, d])
    w_qlat, w_kvlat, w_kr, w_qb, w_kb, w_vb, w_ga, w_gb = jnp.split(w_in, [int(v) for v in cuts[:-1]], axis=1)
    swap = np.concatenate([np.arange(ROPE_DIM // 2, ROPE_DIM), np.arange(ROPE_DIM // 2)])
    w_kr_sw = w_kr[:, swap]
    kr_blk = jnp.concatenate([w_kr, w_kr, w_kr_sw, w_kr_sw], axis=1)
    band_scale = HEAD_DIM ** -0.5
    wbig = jnp.concatenate([w_qlat, w_kvlat, kr_blk, w_qb * band_scale, w_kb, w_vb, w_ga, w_gb], axis=1)

    qk = HEAD_DIM + ROPE_DIM
    wq3 = w_uq.reshape(q_lora, heads, qk) * (qk ** -0.5 * LOG2_E)
    nope = wq3[:, :, :HEAD_DIM].reshape(q_lora, hw)
    rope = wq3[:, :, HEAD_DIM:]
    pad = jnp.zeros((q_lora, heads // 2, LANES - 2 * ROPE_DIM), w_uq.dtype)

    def lay(r):
        return jnp.concatenate([r.reshape(q_lora, heads // 2, 2 * ROPE_DIM), pad], axis=2).reshape(q_lora, hw)

    wq = jnp.concatenate([nope, lay(rope), lay(rope[:, :, swap])], axis=1)
    wvt = jnp.pad(w_uv.reshape(kv_lora, heads, HEAD_DIM).transpose(1, 2, 0),
                  ((0, 0), (0, VT_ROWS - HEAD_DIM), (0, 0))).reshape(heads * VT_ROWS, kv_lora)
    return wbig.astype(BF16), wq.astype(BF16), w_uk.astype(BF16), wvt.astype(BF16)


def kernel(x, c, positions, w_ada, b_ada, g_mix, w_in, g_q, w_uq, g_kv, w_uk, w_uv, rel_bias, w_oa, w_ob,
           w_out, g_ffn, w_rg, b_rg, w_re, b_re, w_gate, w_up, w_down, g_final):
    bsz, seq, d = x.shape
    depth = w_ada.shape[0]
    t = bsz * seq
    heads = rel_bias.shape[1]
    q_lora, kv_lora = g_q.shape[1], g_kv.shape[1]
    tq = 256
    tq_mla = 512
    nblk = -(-LEFT_CHUNKS * CHUNK // tq) + 1
    assert seq % tq == 0 and seq % tq_mla == 0 and tq % CHUNK == 0 and heads % 2 == 0

    x2 = x.reshape(t, d)
    pos2 = positions.reshape(t, 1)
    out = x2
    for l in range(depth):
        mod3 = _ada(c, w_ada[l], b_ada[l]).reshape(bsz, -1, d)
        wbig, wq, wkv, wvt = _layout_weights(w_in[l], w_uq[l], w_uk[l], w_uv[l], heads, q_lora, kv_lora, d)
        qa, ka, vt, qb, kb, vb, ga, gb = _proj(
            x2, pos2, mod3, g_mix[l].reshape(1, d), wbig, g_q[l].reshape(1, -1), wq, g_kv[l].reshape(1, -1), wkv,
            wvt, seq, tm=256)
        oa = _mla(qa, ka, vt, bsz, seq, tq_mla)
        ob = _band(qb, kb, vb, _band_bias(rel_bias[l], tq, nblk), bsz, seq, tq, nblk)

        n_route = N_GROUPS + N_GROUPS * EXPERTS_PER_GROUP
        w_r = jnp.concatenate([w_rg[l], w_re[l], jnp.zeros((d, LANES - n_route), F32)], axis=1)
        b_r = jnp.concatenate([b_rg[l], b_re[l], jnp.zeros((LANES - n_route,), F32)]).reshape(1, LANES)
        w_r_hi = w_r.astype(BF16)
        w_r_lo = (w_r - w_r_hi.astype(F32)).astype(BF16)
        x1, h2x, route_t = _merge(oa, ob, ga, gb, x2, mod3, w_oa[l].astype(BF16), w_ob[l].astype(BF16),
                                  w_out[l].astype(BF16), g_ffn[l].reshape(1, d), w_r_hi, w_r_lo, b_r, seq, tm=256)
        n_tiles = t // MOE_ROWS + N_CLASSES
        pos, counts = _plan(route_t, tc=512)
        pos = pos.reshape(t)
        exp_lo, exp_hi, n_used, zero_fill = _tile_tables(counts, n_tiles)
        hs = _dispatch(h2x, pos, zero_fill, n_tiles, td=512)
        y = _moe(hs, exp_lo, exp_hi, n_used, w_gate[l].astype(BF16), w_up[l].astype(BF16), w_down[l].astype(BF16))
        assert l == depth - 1, "multi-layer stacks need an un-normalised residual output"
        out = _final(y, pos, x1, mod3, g_final.reshape(1, d), seq, tf=256)
    return out.reshape(bsz, seq, d)
```

```python
import functools

import numpy as np
import jax
import jax.numpy as jnp
from jax import lax
from jax.experimental import pallas as pl
from jax.experimental.pallas import tpu as pltpu

F32 = jnp.float32
BF16 = jnp.bfloat16

NORM_EPS = 1e-6
NEG_INF = -1e30
ROPE_BASE = 10000.0
CHUNK = 64
LEFT_CHUNKS = 8
MAX_REL = 256
N_GROUPS = 4
EXPERTS_PER_GROUP = 8
PAIRS_PER_GROUP = EXPERTS_PER_GROUP * (EXPERTS_PER_GROUP - 1) // 2
N_CLASSES = N_GROUPS * PAIRS_PER_GROUP
MOE_ROWS = 128
ROUTE_CLS, ROUTE_W_LO, ROUTE_W_HI = 0, 1, 2
ROUTE_ROWS = 8
LANES = 128
HEAD_DIM = 64
ROPE_DIM = 32
VT_ROWS = HEAD_DIM + 16
LOG2_E = 1.4426950408889634

VMEM_LIMIT = 56 * 1024 * 1024


def _cparams(sem):
    return pltpu.CompilerParams(dimension_semantics=sem, vmem_limit_bytes=VMEM_LIMIT)


def _dot(a, b):
    return jnp.dot(a, b, preferred_element_type=F32)


def _dot_nt(a, b):
    return lax.dot_general(a, b, (((1,), (1,)), ((), ())), preferred_element_type=F32)


def _rms(x, g):
    return x * lax.rsqrt(jnp.mean(x * x, axis=-1, keepdims=True) + NORM_EPS) * g


def _ada_kernel(c_ref, w_ref, b_ref, o_ref):
    c = c_ref[...]
    o_ref[...] = _dot(c * jax.nn.sigmoid(c), w_ref[...]) + b_ref[...]


def _ada(c, w_ada, b_ada):
    bsz, d = c.shape
    n = w_ada.shape[1]
    return pl.pallas_call(
        _ada_kernel,
        grid=(n // d,),
        in_specs=[pl.BlockSpec((bsz, d), lambda j: (0, 0)),
                  pl.BlockSpec((d, d), lambda j: (0, j)),
                  pl.BlockSpec((1, d), lambda j: (0, j))],
        out_specs=pl.BlockSpec((bsz, d), lambda j: (0, j)),
        out_shape=jax.ShapeDtypeStruct((bsz, n), F32),
        compiler_params=_cparams(("arbitrary",)),
        name="ada",
    )(c, w_ada, b_ada.reshape(1, n))


def _proj_kernel(x_ref, pos_ref, mod_ref, gmix_ref, wbig_ref, gq_ref, wq_ref, gkv_ref, wkv_ref,
                 wvt_ref, wvbt_ref, one_ref, inv_ref, sgn_ref,
                 qa_ref, ka_ref, vt_ref, qb_ref, kb_ref, vtb_ref, ga_ref, gb_ref, *, q_lora, kv_lora):
    x = x_ref[...]
    h = _rms(x, gmix_ref[...]) * (1.0 + mod_ref[0, 1:2, :]) + mod_ref[0, 0:1, :]
    hb = h.astype(BF16)

    c0 = q_lora + kv_lora + LANES
    head = _dot(hb, wbig_ref[:, 0:c0])
    q_lat = head[:, 0:q_lora]
    kv_lat = head[:, q_lora:q_lora + kv_lora]
    kr_blk = head[:, q_lora + kv_lora:c0]

    ang = pos_ref[...].astype(F32) * inv_ref[...]
    live = lax.broadcasted_iota(jnp.int32, (1, LANES), 1) < 2 * ROPE_DIM
    cos_t = jnp.where(live, jnp.cos(ang), 0.0)
    sin_t = jnp.where(live, jnp.sin(ang) * sgn_ref[...], 0.0)

    hw = qb_ref.shape[1]
    qn = _rms(q_lat, gq_ref[...]).astype(BF16)
    q_all = _dot(qn, wq_ref[...])
    kvn = _rms(kv_lat, gkv_ref[...]).astype(BF16)
    k_nope = _dot(kvn, wkv_ref[...])
    k_rot = (kr_blk * cos_t + pltpu.roll(kr_blk, LANES // 2, 1) * sin_t).astype(BF16)
    for p in range(hw // LANES):
        lo, hi = p * LANES, (p + 1) * LANES
        qa_ref[:, 2 * lo:2 * lo + LANES] = q_all[:, lo:hi].astype(BF16)
        qa_ref[:, 2 * lo + LANES:2 * hi] = (
            q_all[:, hw + lo:hw + hi] * cos_t + q_all[:, 2 * hw + lo:2 * hw + hi] * sin_t).astype(BF16)
        ka_ref[:, 2 * lo:2 * lo + LANES] = k_nope[:, lo:hi].astype(BF16)
        ka_ref[:, 2 * lo + LANES:2 * hi] = k_rot
    vt_ref[0] = (_dot_nt(wvt_ref[...], kvn) + one_ref[...]).astype(BF16)

    qb_ref[...] = _dot(hb, wbig_ref[:, c0:c0 + hw]).astype(BF16)
    kb_ref[...] = _dot(hb, wbig_ref[:, c0 + hw:c0 + 2 * hw]).astype(BF16)
    vtb_ref[0] = (_dot_nt(wvbt_ref[...], hb) + one_ref[...]).astype(BF16)
    d = x.shape[1]
    g0 = c0 + 2 * hw
    ga_ref[...] = jax.nn.sigmoid(_dot(hb, wbig_ref[:, g0:g0 + d])).astype(BF16)
    gb_ref[...] = jax.nn.sigmoid(_dot(hb, wbig_ref[:, g0 + d:g0 + 2 * d])).astype(BF16)


def _proj(x2, pos2, mod3, g_mix, wbig, g_q, wq, g_kv, wkv, wvt, wvbt, seq, tm):
    t, d = x2.shape
    q_lora, kv_lora = g_q.shape[1], g_kv.shape[1]
    hw = wkv.shape[1]
    per_b = seq // tm
    row = lambda i: (i, 0)
    full = lambda i: (0, 0)
    inv = ROPE_BASE ** (-(np.arange(LANES) % (ROPE_DIM // 2)).astype(np.float32) / (ROPE_DIM // 2))
    sgn = np.where((np.arange(LANES) % ROPE_DIM) < ROPE_DIM // 2, -1.0, 1.0).astype(np.float32)
    ones_col = (np.arange(wvt.shape[0]) % VT_ROWS == HEAD_DIM).astype(np.float32).reshape(-1, 1)
    v_t = jax.ShapeDtypeStruct((t // seq, wvt.shape[0], seq), BF16)
    outs = [jax.ShapeDtypeStruct((t, 2 * hw), BF16), jax.ShapeDtypeStruct((t, 2 * hw), BF16), v_t,
            jax.ShapeDtypeStruct((t, hw), BF16), jax.ShapeDtypeStruct((t, hw), BF16), v_t,
            jax.ShapeDtypeStruct((t, d), BF16), jax.ShapeDtypeStruct((t, d), BF16)]
    v_t_spec = pl.BlockSpec((1, wvt.shape[0], tm), lambda i: (i // per_b, 0, i % per_b))
    out_specs = [v_t_spec if o is v_t else pl.BlockSpec((tm, o.shape[1]), row) for o in outs]
    return pl.pallas_call(
        functools.partial(_proj_kernel, q_lora=q_lora, kv_lora=kv_lora),
        grid=(t // tm,),
        in_specs=[pl.BlockSpec((tm, d), row),
                  pl.BlockSpec((tm, 1), row),
                  pl.BlockSpec((1,) + mod3.shape[1:], lambda i: (i // per_b, 0, 0)),
                  pl.BlockSpec(g_mix.shape, full),
                  pl.BlockSpec(wbig.shape, full),
                  pl.BlockSpec(g_q.shape, full),
                  pl.BlockSpec(wq.shape, full),
                  pl.BlockSpec(g_kv.shape, full),
                  pl.BlockSpec(wkv.shape, full),
                  pl.BlockSpec(wvt.shape, full),
                  pl.BlockSpec(wvbt.shape, full),
                  pl.BlockSpec(ones_col.shape, full),
                  pl.BlockSpec((1, LANES), full),
                  pl.BlockSpec((1, LANES), full)],
        out_specs=out_specs,
        out_shape=outs,
        compiler_params=_cparams(("parallel",)),
        name="proj",
    )(x2, pos2, mod3, g_mix, wbig, g_q, wq, g_kv, wkv, wvt, wvbt, jnp.asarray(ones_col),
      jnp.asarray(inv).reshape(1, LANES), jnp.asarray(sgn).reshape(1, LANES))


def _mla_kernel(q_ref, k_ref, vt_ref, o_ref, sa, sb, *, tq):
    i = pl.program_id(2)
    q = q_ref[...]
    lane = lax.broadcasted_iota(jnp.int32, (1, 2 * LANES), 1)
    first = (lane < HEAD_DIM) | ((lane >= LANES) & (lane < LANES + ROPE_DIM))
    second = ((lane >= HEAD_DIM) & (lane < LANES)) | ((lane >= LANES + ROPE_DIM) & (lane < LANES + 2 * ROPE_DIM))
    zero = jnp.zeros_like(q)
    qs = (jnp.where(first, q, zero), jnp.where(second, q, zero))

    def scores(blk, buf):
        k = k_ref[pl.ds(pl.multiple_of(blk * tq, tq), tq), :]
        for h, qh in enumerate(qs):
            buf[h] = _dot_nt(k, qh)

    def consume(blk, buf, state, mask):
        start = pl.multiple_of(blk * tq, tq)
        ps, ms, alphas = [], [], []
        for h, (m, _) in enumerate(state):
            st = buf[h]
            if mask is not None:
                st = jnp.where(mask, st, NEG_INF)
            m_new = jnp.maximum(m, jnp.max(st, axis=0, keepdims=True))
            ms.append(m_new)
            alphas.append(jnp.exp2(m - m_new))
            ps.append(jnp.exp2(st - m_new).astype(BF16))
        new = []
        for h, (p, m_new, alpha, (_, acc)) in enumerate(zip(ps, ms, alphas, state)):
            vt = vt_ref[0, h * VT_ROWS:(h + 1) * VT_ROWS, pl.ds(start, tq)]
            new.append((m_new, alpha * acc + _dot(vt, p)))
        return tuple(new)

    def two_blocks(jj, state):
        scores(2 * jj + 1, sb)
        state = consume(2 * jj, sa, state, None)
        scores(2 * jj + 2, sa)
        return consume(2 * jj + 1, sb, state, None)

    init = tuple((jnp.full((1, tq), NEG_INF, F32), jnp.zeros((VT_ROWS, tq), F32)) for _ in range(2))
    scores(0, sa)
    state = lax.fori_loop(0, i // 2, two_blocks, init)
    kc = lax.broadcasted_iota(jnp.int32, (tq, tq), 0) // CHUNK
    qc = lax.broadcasted_iota(jnp.int32, (tq, tq), 1) // CHUNK

    def odd_tail(state):
        scores(i, sb)
        return consume(i, sb, consume(i - 1, sa, state, None), kc <= qc)

    (_, a0), (_, a1) = lax.cond(i % 2 == 1, odd_tail, lambda state: consume(i, sa, state, kc <= qc), state)
    out_t = jnp.concatenate([a[0:HEAD_DIM] * (1.0 / a[HEAD_DIM:HEAD_DIM + 1]) for a in (a0, a1)], axis=0)
    o_ref[...] = out_t.T.astype(o_ref.dtype)


def _mla(qa, ka, vt, bsz, seq, tq):
    t = qa.shape[0]
    pairs = qa.shape[1] // (2 * LANES)
    nq = seq // tq
    return pl.pallas_call(
        functools.partial(_mla_kernel, tq=tq),
        grid=(bsz, pairs, nq),
        in_specs=[pl.BlockSpec((tq, 2 * LANES), lambda b, p, i: (b * nq + i, p)),
                  pl.BlockSpec((seq, 2 * LANES), lambda b, p, i: (b, p)),
                  pl.BlockSpec((1, 2 * VT_ROWS, seq), lambda b, p, i: (b, p, 0))],
        out_specs=pl.BlockSpec((tq, LANES), lambda b, p, i: (b * nq + i, p)),
        out_shape=jax.ShapeDtypeStruct((t, pairs * LANES), BF16),
        scratch_shapes=[pltpu.VMEM((2, tq, tq), F32)] * 2,
        compiler_params=_cparams(("parallel", "parallel", "arbitrary")),
        name="mla",
    )(qa, ka, vt)


def _band_kernel(q_ref, k_ref, vt_ref, bias_ref, o_ref, *, tq, nblk):
    lane = lax.broadcasted_iota(jnp.int32, (1, LANES), 1)

    def scores(n):
        tile = first + n
        q = q_ref[n * tq:(n + 1) * tq, :]
        zero = jnp.zeros_like(q)
        qs = (jnp.where(lane < HEAD_DIM, q, zero), jnp.where(lane >= HEAD_DIM, q, zero))
        starts, valid = [], []
        for j in range(nblk):
            blk = tile - (nblk - 1) + j
            starts.append(pl.multiple_of(jnp.maximum(blk, 0) * tq, tq))
            valid.append(blk >= 0)
        ks = [k_ref[pl.ds(s, tq), :] for s in starts]
        return starts, [[jnp.where(valid[j], _dot_nt(ks[j], qh) + bias_ref[0, h, j * tq:(j + 1) * tq, :], NEG_INF)
                         for j in range(nblk)] for h, qh in enumerate(qs)]

    def finish(starts, sts):
        outs = []
        for h in range(2):
            m = functools.reduce(jnp.maximum, [jnp.max(st, axis=0, keepdims=True) for st in sts[h]])
            acc = functools.reduce(jnp.add, [
                _dot(vt_ref[0, h * VT_ROWS:(h + 1) * VT_ROWS, pl.ds(starts[j], tq)], jnp.exp2(st - m).astype(BF16))
                for j, st in enumerate(sts[h])])
            outs.append(acc[0:HEAD_DIM] * (1.0 / acc[HEAD_DIM:HEAD_DIM + 1]))
        return jnp.concatenate(outs, axis=0).T

    tiles_per_step = q_ref.shape[0] // tq
    first = pl.program_id(2) * tiles_per_step
    pending = scores(0)
    for n in range(tiles_per_step):
        upcoming = scores(n + 1) if n + 1 < tiles_per_step else None
        o_ref[n * tq:(n + 1) * tq, :] = finish(*pending).astype(o_ref.dtype)
        pending = upcoming


def _band_bias(rel_table, tq, nblk):
    nk = nblk * tq
    ring = nk + tq
    d = np.arange(ring)
    key_minus_query = np.where(d < nk, d, d - ring)
    idx = np.clip((nblk - 1) * tq - key_minus_query, -MAX_REL, MAX_REL) + MAX_REL
    per_offset = rel_table.astype(F32)[:, idx]
    heads = rel_table.shape[0]
    toeplitz = jnp.tile(per_offset, (1, tq))[:, :tq * (ring - 1)].reshape(heads, tq, ring - 1)[:, :, :nk]
    qc = np.arange(tq)[:, None] // CHUNK
    kc = np.arange(nk)[None, :] // CHUNK - ((nblk - 1) * tq // CHUNK - LEFT_CHUNKS)
    band = (kc >= qc) & (kc <= qc + LEFT_CHUNKS)
    bias = jnp.where(band[None], toeplitz * LOG2_E, NEG_INF)
    return bias.reshape(heads // 2, 2, tq, nk).swapaxes(2, 3)


def _band(qb, kb, vtb, bias, bsz, seq, tq, nblk, tiles_per_step):
    t, hw = qb.shape
    pairs = hw // LANES
    rows = tq * tiles_per_step
    nq = seq // rows
    return pl.pallas_call(
        functools.partial(_band_kernel, tq=tq, nblk=nblk),
        grid=(bsz, pairs, nq),
        in_specs=[pl.BlockSpec((rows, LANES), lambda b, p, i: (b * nq + i, p)),
                  pl.BlockSpec((seq, LANES), lambda b, p, i: (b, p)),
                  pl.BlockSpec((1, 2 * VT_ROWS, seq), lambda b, p, i: (b, p, 0)),
                  pl.BlockSpec((1,) + bias.shape[1:], lambda b, p, i: (p, 0, 0, 0))],
        out_specs=pl.BlockSpec((rows, LANES), lambda b, p, i: (b * nq + i, p)),
        out_shape=jax.ShapeDtypeStruct((t, hw), BF16),
        compiler_params=_cparams(("parallel", "parallel", "arbitrary")),
        name="band",
    )(qb, kb, vtb, bias)


def _merge_kernel(oa_ref, ob_ref, ga_ref, gb_ref, x_ref, mod_ref, woa_ref, wob_ref, wout_ref, gffn_ref,
                  wr_ref, br_ref, x1_ref, h2x_ref, rt_ref):
    merged = (ga_ref[...].astype(F32) * _dot(oa_ref[...], woa_ref[...])
              + gb_ref[...].astype(F32) * _dot(ob_ref[...], wob_ref[...]))
    mix = _dot(merged.astype(BF16), wout_ref[...])
    x1 = x_ref[...] + mod_ref[0, 2:3, :] * mix
    x1_ref[...] = x1
    h2 = _rms(x1, gffn_ref[...]) * (1.0 + mod_ref[0, 4:5, :]) + mod_ref[0, 3:4, :]

    h_hi = h2.astype(BF16)
    h_lo = (h2 - h_hi.astype(F32)).astype(BF16)
    r_hi = _dot(h_hi, wr_ref[...])
    logits = (r_hi[:, 0:LANES] + _dot(h_lo, wr_ref[...])[:, 0:LANES] + r_hi[:, LANES:2 * LANES]) + br_ref[...]

    lane = lax.broadcasted_iota(jnp.int32, logits.shape, 1)
    far = jnp.int32(LANES)
    is_g = lane < N_GROUPS
    gl = jnp.where(is_g, logits, NEG_INF)
    g_max = jnp.max(gl, axis=-1, keepdims=True)
    g_w = 1.0 / jnp.sum(jnp.where(is_g, jnp.exp(gl - g_max), 0.0), axis=-1, keepdims=True)
    g_idx = jnp.min(jnp.where(gl == g_max, lane, far), axis=-1, keepdims=True)
    lo = N_GROUPS + EXPERTS_PER_GROUP * g_idx
    el = jnp.where((lane >= lo) & (lane < lo + EXPERTS_PER_GROUP), logits, NEG_INF)
    e1 = jnp.max(el, axis=-1, keepdims=True)
    i1 = jnp.min(jnp.where(el == e1, lane, far), axis=-1, keepdims=True)
    el2 = jnp.where(lane == i1, NEG_INF, el)
    e2 = jnp.max(el2, axis=-1, keepdims=True)
    i2 = jnp.min(jnp.where(el2 == e2, lane, far), axis=-1, keepdims=True)
    ratio = jnp.exp(e2 - e1)
    w1 = g_w / (1.0 + ratio)
    w2 = g_w * ratio / (1.0 + ratio)
    e_lo, e_hi = jnp.minimum(i1, i2) - lo, jnp.maximum(i1, i2) - lo
    cls = (g_idx * PAIRS_PER_GROUP + jnp.right_shift(e_lo * (2 * EXPERTS_PER_GROUP - 1 - e_lo), 1)
           + (e_hi - e_lo - 1))
    first_is_lo = i1 < i2
    w_lo, w_hi = jnp.where(first_is_lo, w1, w2), jnp.where(first_is_lo, w2, w1)
    route = (jnp.where(lane == ROUTE_CLS, cls.astype(F32), 0.0) + jnp.where(lane == ROUTE_W_LO, w_lo, 0.0)
             + jnp.where(lane == ROUTE_W_HI, w_hi, 0.0))
    d = h2.shape[1]
    h2x_ref[:, 0:d] = h2
    h2x_ref[:, d:d + LANES] = route
    rt_ref[...] = route.T[0:ROUTE_ROWS, :]


def _merge(oa, ob, ga, gb, x2, mod3, woa, wob, wout, g_ffn, wr, br, seq, tm):
    t, d = x2.shape
    hw = oa.shape[1]
    per_b = seq // tm
    row = lambda i: (i, 0)
    full = lambda i: (0, 0)
    return pl.pallas_call(
        _merge_kernel,
        grid=(t // tm,),
        in_specs=[pl.BlockSpec((tm, hw), row), pl.BlockSpec((tm, hw), row),
                  pl.BlockSpec((tm, d), row), pl.BlockSpec((tm, d), row),
                  pl.BlockSpec((tm, d), row),
                  pl.BlockSpec((1,) + mod3.shape[1:], lambda i: (i // per_b, 0, 0)),
                  pl.BlockSpec(woa.shape, full), pl.BlockSpec(wob.shape, full),
                  pl.BlockSpec(wout.shape, full), pl.BlockSpec(g_ffn.shape, full),
                  pl.BlockSpec(wr.shape, full), pl.BlockSpec(br.shape, full)],
        out_specs=[pl.BlockSpec((tm, d), row), pl.BlockSpec((tm, d + LANES), row),
                   pl.BlockSpec((ROUTE_ROWS, tm), lambda i: (0, i))],
        out_shape=[jax.ShapeDtypeStruct((t, d), F32), jax.ShapeDtypeStruct((t, d + LANES), F32),
                   jax.ShapeDtypeStruct((ROUTE_ROWS, t), F32)],
        compiler_params=_cparams(("parallel",)),
        name="merge",
    )(oa, ob, ga, gb, x2, mod3, woa, wob, wout, g_ffn, wr, br)


def _plan_kernel(rt_ref, pos_ref, cnt_ref, carry_ref, off_ref, *, tc):
    phase, i = pl.program_id(0), pl.program_id(1)
    cls = rt_ref[ROUTE_CLS:ROUTE_CLS + 1, :]
    onehot = lax.broadcasted_iota(jnp.int32, (LANES, tc), 0).astype(F32) == cls
    per_class = jnp.sum(onehot.astype(F32), axis=1, keepdims=True)

    @pl.when((phase == 0) & (i == 0))
    def _():
        cnt_ref[...] = jnp.zeros_like(cnt_ref)

    @pl.when(phase == 0)
    def _():
        cnt_ref[...] += per_class

    @pl.when((phase == 1) & (i == 0))
    def _():
        padded = jnp.floor((cnt_ref[...] + (MOE_ROWS - 1)) * (1.0 / MOE_ROWS)) * MOE_ROWS
        hi = jnp.floor(padded * (1.0 / 256.0))
        digits = [jnp.broadcast_to(v, (LANES, LANES)).astype(BF16) for v in (hi, padded - 256.0 * hi)]
        below = (lax.broadcasted_iota(jnp.int32, (LANES, LANES), 1)
                 < lax.broadcasted_iota(jnp.int32, (LANES, LANES), 0)).astype(BF16)
        off = 256.0 * _dot(below, digits[0]) + _dot(below, digits[1])
        off_ref[...] = off[:, 0:1]
        carry_ref[...] = jnp.zeros_like(carry_ref)

    @pl.when(phase == 1)
    def _():
        earlier = (lax.broadcasted_iota(jnp.int32, (tc, tc), 0)
                   < lax.broadcasted_iota(jnp.int32, (tc, tc), 1)).astype(BF16)
        rank = _dot(onehot.astype(BF16), earlier)
        row = rank + (carry_ref[...] + off_ref[...])
        pos_ref[...] = jnp.sum(jnp.where(onehot, row, 0.0), axis=0, keepdims=True).astype(jnp.int32)
        carry_ref[...] += per_class


def _plan(route_t, tc):
    t = route_t.shape[1]
    return pl.pallas_call(
        functools.partial(_plan_kernel, tc=tc),
        grid=(2, t // tc),
        in_specs=[pl.BlockSpec((ROUTE_ROWS, tc), lambda ph, i: (0, i))],
        out_specs=[pl.BlockSpec((1, tc), lambda ph, i: (0, i * ph)),
                   pl.BlockSpec((LANES, 1), lambda ph, i: (0, 0))],
        out_shape=[jax.ShapeDtypeStruct((1, t), jnp.int32), jax.ShapeDtypeStruct((LANES, 1), F32)],
        scratch_shapes=[pltpu.VMEM((LANES, 1), F32), pltpu.VMEM((LANES, 1), F32)],
        compiler_params=_cparams(("arbitrary", "arbitrary")),
        name="plan",
    )(route_t)


def _tile_tables(counts, n_tiles):
    pairs = [(lo, hi) for lo in range(EXPERTS_PER_GROUP) for hi in range(lo + 1, EXPERTS_PER_GROUP)]
    exp_lo = np.array([g * EXPERTS_PER_GROUP + lo for g in range(N_GROUPS) for lo, _ in pairs], np.int32)
    exp_hi = np.array([g * EXPERTS_PER_GROUP + hi for g in range(N_GROUPS) for _, hi in pairs], np.int32)
    cnt = counts.reshape(-1)[:N_CLASSES].astype(jnp.int32)
    tiles_per_class = (cnt + (MOE_ROWS - 1)) // MOE_ROWS
    tile_end = jnp.cumsum(tiles_per_class)
    n_used = tile_end[-1]
    j = jnp.arange(n_tiles, dtype=jnp.int32)
    tile_cls = jnp.sum(tile_end[None, :] <= jnp.minimum(j, n_used - 1)[:, None], axis=1, dtype=jnp.int32)
    last_of_class = jnp.any((j[:, None] == tile_end[None, :] - 1) & (tiles_per_class[None, :] > 0), axis=1)
    zero_fill = (last_of_class | (j >= n_used)).astype(jnp.int32)
    return (jnp.asarray(exp_lo)[tile_cls], jnp.asarray(exp_hi)[tile_cls], n_used.reshape(1).astype(jnp.int32),
            zero_fill)


def _row_copy(src, src_row, dst, dst_row, n, sem):
    return pltpu.make_async_copy(src.at[pl.ds(src_row, n)], dst.at[pl.ds(dst_row, n)], sem)


def _dispatch_kernel(pos_ref, zf_ref, h_ref, hs_ref, zero_ref, sem, *, td, n_tiles):
    i = pl.program_id(0)

    @pl.when(i == 0)
    def _():
        zero_ref[...] = jnp.zeros_like(zero_ref)

        def fill(j, n):
            @pl.when(zf_ref[j] > 0)
            def _():
                _row_copy(zero_ref, 0, hs_ref, pl.multiple_of(j * MOE_ROWS, MOE_ROWS), MOE_ROWS, sem).start()
            return n + zf_ref[j]

        n_fill = lax.fori_loop(0, n_tiles, fill, jnp.int32(0))

        def drain(_, carry):
            _row_copy(zero_ref, 0, hs_ref, 0, MOE_ROWS, sem).wait()
            return carry

        lax.fori_loop(0, n_fill, drain, 0)

    base = i * td

    def send(r, carry):
        _row_copy(h_ref, r, hs_ref, pos_ref[base + r], 1, sem).start()
        return carry

    lax.fori_loop(0, td, send, 0, unroll=8)
    _row_copy(h_ref, 0, hs_ref, 0, td, sem).wait()


def _dispatch(h2x, pos, zero_fill, n_tiles, td):
    t, w = h2x.shape
    return pl.pallas_call(
        functools.partial(_dispatch_kernel, td=td, n_tiles=n_tiles),
        grid_spec=pltpu.PrefetchScalarGridSpec(
            num_scalar_prefetch=2,
            grid=(t // td,),
            in_specs=[pl.BlockSpec((td, w), lambda i, pos, zf: (i, 0))],
            out_specs=pl.BlockSpec(memory_space=pl.ANY),
            scratch_shapes=[pltpu.VMEM((MOE_ROWS, w), F32), pltpu.SemaphoreType.DMA(())]),
        out_shape=jax.ShapeDtypeStruct((n_tiles * MOE_ROWS, w), F32),
        compiler_params=_cparams(("arbitrary",)),
        name="dispatch",
    )(pos, zero_fill, h2x)


def _moe_kernel(elo_ref, ehi_ref, nu_ref, hs_ref, wg_lo, wu_lo, wd_lo, wg_hi, wu_hi, wd_hi, y_ref):
    j = pl.program_id(0)
    d = y_ref.shape[1]

    @pl.when(j < nu_ref[0])
    def _():
        h = hs_ref[:, 0:d].astype(BF16)
        ups = [(_dot(h, wg[0]), _dot(h, wu[0])) for wg, wu in ((wg_lo, wu_lo), (wg_hi, wu_hi))]
        hids = [((a * jax.nn.sigmoid(a)) * u * hs_ref[:, d + lane:d + lane + 1]).astype(BF16)
                for (a, u), lane in zip(ups, (ROUTE_W_LO, ROUTE_W_HI))]
        y_ref[...] = _dot(hids[0], wd_lo[0]) + _dot(hids[1], wd_hi[0])

    @pl.when(j >= nu_ref[0])
    def _():
        y_ref[...] = jnp.zeros_like(y_ref)


def _moe(hs, exp_lo, exp_hi, n_used, wg, wu, wd):
    n_tiles = hs.shape[0] // MOE_ROWS
    _, d, ff = wg.shape
    up = lambda sel: pl.BlockSpec((1, d, ff), lambda j, elo, ehi, nu: ((elo, ehi)[sel][j], 0, 0))
    down = lambda sel: pl.BlockSpec((1, ff, d), lambda j, elo, ehi, nu: ((elo, ehi)[sel][j], 0, 0))
    return pl.pallas_call(
        _moe_kernel,
        grid_spec=pltpu.PrefetchScalarGridSpec(
            num_scalar_prefetch=3,
            grid=(n_tiles,),
            in_specs=[pl.BlockSpec((MOE_ROWS, hs.shape[1]),
                                   lambda j, elo, ehi, nu: (jnp.minimum(j, nu[0] - 1), 0)),
                      up(0), up(0), down(0), up(1), up(1), down(1)],
            out_specs=pl.BlockSpec((MOE_ROWS, d), lambda j, elo, ehi, nu: (j, 0))),
        out_shape=jax.ShapeDtypeStruct((n_tiles * MOE_ROWS, d), F32),
        compiler_params=_cparams(("arbitrary",)),
        name="moe",
    )(exp_lo, exp_hi, n_used, hs, wg, wu, wd, wg, wu, wd)


def _final_kernel(pos_ref, x1_ref, mod_ref, gfin_ref, y_ref, o_ref, rows_ref, sem, *, tf):
    i, n = pl.program_id(0), pl.num_programs(0)

    def fetch(step, slot):
        def one(r, carry):
            _row_copy(y_ref, pos_ref[step * tf + r], rows_ref.at[slot], r, 1, sem.at[slot]).start()
            return carry

        lax.fori_loop(0, tf, one, 0, unroll=8)

    @pl.when(i == 0)
    def _():
        fetch(0, 0)

    @pl.when(i + 1 < n)
    def _():
        fetch(i + 1, (i + 1) % 2)

    slot = i % 2
    _row_copy(y_ref, 0, rows_ref.at[slot], 0, tf, sem.at[slot]).wait()
    x2 = x1_ref[...] + mod_ref[0, 5:6, :] * rows_ref[slot]
    o_ref[...] = _rms(x2, gfin_ref[...])


def _final(y, pos, x1, mod3, g_final, seq, tf):
    t, d = x1.shape
    per_b = seq // tf
    return pl.pallas_call(
        functools.partial(_final_kernel, tf=tf),
        grid_spec=pltpu.PrefetchScalarGridSpec(
            num_scalar_prefetch=1,
            grid=(t // tf,),
            in_specs=[pl.BlockSpec((tf, d), lambda i, pos: (i, 0)),
                      pl.BlockSpec((1,) + mod3.shape[1:], lambda i, pos: (i // per_b, 0, 0)),
                      pl.BlockSpec((1, d), lambda i, pos: (0, 0)),
                      pl.BlockSpec(memory_space=pl.ANY)],
            out_specs=pl.BlockSpec((tf, d), lambda i, pos: (i, 0)),
            scratch_shapes=[pltpu.VMEM((2, tf, d), F32), pltpu.SemaphoreType.DMA((2,))]),
        out_shape=jax.ShapeDtypeStruct((t, d), F32),
        compiler_params=_cparams(("arbitrary",)),
        name="final",
    )(pos, x1, mod3, g_final, y)


def _layout_weights(w_in, w_uq, w_uk, w_uv, heads, q_lora, kv_lora, d):
    hw = heads * HEAD_DIM
    cuts = np.cumsum([q_lora, kv_lora, ROPE_DIM, hw, hw, hw, d, d])
    w_qlat, w_kvlat, w_kr, w_qb, w_kb, w_vb, w_ga, w_gb = jnp.split(w_in, [int(v) for v in cuts[:-1]], axis=1)
    swap = np.concatenate([np.arange(ROPE_DIM // 2, ROPE_DIM), np.arange(ROPE_DIM // 2)])
    w_kr_sw = w_kr[:, swap]
    kr_blk = jnp.concatenate([w_kr, w_kr, w_kr_sw, w_kr_sw], axis=1)
    band_scale = HEAD_DIM ** -0.5 * LOG2_E
    wbig = jnp.concatenate([w_qlat, w_kvlat, kr_blk, w_qb * band_scale, w_kb, w_ga, w_gb], axis=1)

    def transposed_values(w, k):
        return jnp.pad(w.reshape(k, heads, HEAD_DIM).transpose(1, 2, 0),
                       ((0, 0), (0, VT_ROWS - HEAD_DIM), (0, 0))).reshape(heads * VT_ROWS, k)

    qk = HEAD_DIM + ROPE_DIM
    wq3 = w_uq.reshape(q_lora, heads, qk) * (qk ** -0.5 * LOG2_E)
    nope = wq3[:, :, :HEAD_DIM].reshape(q_lora, hw)
    rope = wq3[:, :, HEAD_DIM:]
    pad = jnp.zeros((q_lora, heads // 2, LANES - 2 * ROPE_DIM), w_uq.dtype)

    def lay(r):
        return jnp.concatenate([r.reshape(q_lora, heads // 2, 2 * ROPE_DIM), pad], axis=2).reshape(q_lora, hw)

    wq = jnp.concatenate([nope, lay(rope), lay(rope[:, :, swap])], axis=1)
    return (wbig.astype(BF16), wq.astype(BF16), w_uk.astype(BF16), transposed_values(w_uv, kv_lora).astype(BF16),
            transposed_values(w_vb, d).astype(BF16))


def kernel(x, c, positions, w_ada, b_ada, g_mix, w_in, g_q, w_uq, g_kv, w_uk, w_uv, rel_bias, w_oa, w_ob,
           w_out, g_ffn, w_rg, b_rg, w_re, b_re, w_gate, w_up, w_down, g_final):
    bsz, seq, d = x.shape
    depth = w_ada.shape[0]
    t = bsz * seq
    heads = rel_bias.shape[1]
    q_lora, kv_lora = g_q.shape[1], g_kv.shape[1]
    tq = 256
    tq_mla = 512
    nblk = -(-LEFT_CHUNKS * CHUNK // tq) + 1
    assert seq % tq == 0 and seq % tq_mla == 0 and tq % CHUNK == 0 and heads % 2 == 0

    x2 = x.reshape(t, d)
    pos2 = positions.reshape(t, 1)
    out = x2
    for l in range(depth):
        mod3 = _ada(c, w_ada[l], b_ada[l]).reshape(bsz, -1, d)
        wbig, wq, wkv, wvt, wvbt = _layout_weights(w_in[l], w_uq[l], w_uk[l], w_uv[l], heads, q_lora, kv_lora, d)
        qa, ka, vt, qb, kb, vtb, ga, gb = _proj(
            x2, pos2, mod3, g_mix[l].reshape(1, d), wbig, g_q[l].reshape(1, -1), wq, g_kv[l].reshape(1, -1), wkv,
            wvt, wvbt, seq, tm=256)
        oa = _mla(qa, ka, vt, bsz, seq, tq_mla)
        ob = _band(qb, kb, vtb, _band_bias(rel_bias[l], tq, nblk), bsz, seq, tq, nblk, tiles_per_step=4)

        n_route = N_GROUPS + N_GROUPS * EXPERTS_PER_GROUP
        w_r = jnp.concatenate([w_rg[l], w_re[l], jnp.zeros((d, LANES - n_route), F32)], axis=1)
        b_r = jnp.concatenate([b_rg[l], b_re[l], jnp.zeros((LANES - n_route,), F32)]).reshape(1, LANES)
        w_r_hi = w_r.astype(BF16)
        w_r_lo = (w_r - w_r_hi.astype(F32)).astype(BF16)
        x1, h2x, route_t = _merge(oa, ob, ga, gb, x2, mod3, w_oa[l].astype(BF16), w_ob[l].astype(BF16),
                                  w_out[l].astype(BF16), g_ffn[l].reshape(1, d),
                                  jnp.concatenate([w_r_hi, w_r_lo], axis=1), b_r, seq, tm=256)
        n_tiles = t // MOE_ROWS + N_CLASSES
        pos, counts = _plan(route_t, tc=512)
        pos = pos.reshape(t)
        exp_lo, exp_hi, n_used, zero_fill = _tile_tables(counts, n_tiles)
        hs = _dispatch(h2x, pos, zero_fill, n_tiles, td=2048)
        y = _moe(hs, exp_lo, exp_hi, n_used, w_gate[l].astype(BF16), w_up[l].astype(BF16), w_down[l].astype(BF16))
        assert l == depth - 1, "multi-layer stacks need an un-normalised residual output"
        out = _final(y, pos, x1, mod3, g_final.reshape(1, d), seq, tf=512)
    return out.reshape(bsz, seq, d)
```

```python
import functools

import numpy as np
import jax
import jax.numpy as jnp
from jax import lax
from jax.experimental import pallas as pl
from jax.experimental.pallas import tpu as pltpu

F32 = jnp.float32
BF16 = jnp.bfloat16

NORM_EPS = 1e-6
NEG_INF = -1e30
ROPE_BASE = 10000.0
CHUNK = 64
LEFT_CHUNKS = 8
MAX_REL = 256
N_GROUPS = 4
EXPERTS_PER_GROUP = 8
PAIRS_PER_GROUP = EXPERTS_PER_GROUP * (EXPERTS_PER_GROUP - 1) // 2
N_CLASSES = N_GROUPS * PAIRS_PER_GROUP
MOE_ROWS = 256
ROUTE_CLS, ROUTE_W_LO, ROUTE_W_HI = 0, 1, 2
ROUTE_ROWS = 8
LANES = 128
HEAD_DIM = 64
ROPE_DIM = 32
VT_ROWS = HEAD_DIM + 16
LOG2_E = 1.4426950408889634

VMEM_LIMIT = 56 * 1024 * 1024


def _cparams(sem):
    return pltpu.CompilerParams(dimension_semantics=sem, vmem_limit_bytes=VMEM_LIMIT)


def _dot(a, b):
    return jnp.dot(a, b, preferred_element_type=F32)


def _dot_nt(a, b):
    return lax.dot_general(a, b, (((1,), (1,)), ((), ())), preferred_element_type=F32)


def _rms(x, g):
    return x * lax.rsqrt(jnp.mean(x * x, axis=-1, keepdims=True) + NORM_EPS) * g


def _ada_kernel(c_ref, w_ref, b_ref, o_ref):
    c = c_ref[...]
    o_ref[...] = _dot(c * jax.nn.sigmoid(c), w_ref[...]) + b_ref[...]


def _ada(c, w_ada, b_ada):
    bsz, d = c.shape
    n = w_ada.shape[1]
    return pl.pallas_call(
        _ada_kernel,
        grid=(n // d,),
        in_specs=[pl.BlockSpec((bsz, d), lambda j: (0, 0)),
                  pl.BlockSpec((d, d), lambda j: (0, j)),
                  pl.BlockSpec((1, d), lambda j: (0, j))],
        out_specs=pl.BlockSpec((bsz, d), lambda j: (0, j)),
        out_shape=jax.ShapeDtypeStruct((bsz, n), F32),
        compiler_params=_cparams(("arbitrary",)),
        name="ada",
    )(c, w_ada, b_ada.reshape(1, n))


def _proj_kernel(x_ref, pos_ref, mod_ref, gmix_ref, wbig_ref, gq_ref, wq_ref, gkv_ref, wkv_ref,
                 wvt_ref, wvbt_ref, one_ref, inv_ref, sgn_ref,
                 qa_ref, ka_ref, vt_ref, qb_ref, kb_ref, vtb_ref, ga_ref, gb_ref, *, q_lora, kv_lora):
    x = x_ref[...]
    h = _rms(x, gmix_ref[...]) * (1.0 + mod_ref[0, 1:2, :]) + mod_ref[0, 0:1, :]
    hb = h.astype(BF16)

    c0 = q_lora + kv_lora + LANES
    head = _dot(hb, wbig_ref[:, 0:c0])
    q_lat = head[:, 0:q_lora]
    kv_lat = head[:, q_lora:q_lora + kv_lora]
    kr_blk = head[:, q_lora + kv_lora:c0]

    ang = pos_ref[...].astype(F32) * inv_ref[...]
    live = lax.broadcasted_iota(jnp.int32, (1, LANES), 1) < 2 * ROPE_DIM
    cos_t = jnp.where(live, jnp.cos(ang), 0.0)
    sin_t = jnp.where(live, jnp.sin(ang) * sgn_ref[...], 0.0)

    hw = qb_ref.shape[1]
    qn = _rms(q_lat, gq_ref[...]).astype(BF16)
    q_all = _dot(qn, wq_ref[...])
    kvn = _rms(kv_lat, gkv_ref[...]).astype(BF16)
    k_nope = _dot(kvn, wkv_ref[...])
    k_rot = (kr_blk * cos_t + pltpu.roll(kr_blk, LANES // 2, 1) * sin_t).astype(BF16)
    for p in range(hw // LANES):
        lo, hi = p * LANES, (p + 1) * LANES
        qa_ref[:, 2 * lo:2 * lo + LANES] = q_all[:, lo:hi].astype(BF16)
        qa_ref[:, 2 * lo + LANES:2 * hi] = (
            q_all[:, hw + lo:hw + hi] * cos_t + q_all[:, 2 * hw + lo:2 * hw + hi] * sin_t).astype(BF16)
        ka_ref[:, 2 * lo:2 * lo + LANES] = k_nope[:, lo:hi].astype(BF16)
        ka_ref[:, 2 * lo + LANES:2 * hi] = k_rot
    vt_ref[0] = (_dot_nt(wvt_ref[...], kvn) + one_ref[...]).astype(BF16)

    qb_ref[...] = _dot(hb, wbig_ref[:, c0:c0 + hw]).astype(BF16)
    kb_ref[...] = _dot(hb, wbig_ref[:, c0 + hw:c0 + 2 * hw]).astype(BF16)
    vtb_ref[0] = (_dot_nt(wvbt_ref[...], hb) + one_ref[...]).astype(BF16)
    d = x.shape[1]
    g0 = c0 + 2 * hw
    ga_ref[...] = jax.nn.sigmoid(_dot(hb, wbig_ref[:, g0:g0 + d])).astype(BF16)
    gb_ref[...] = jax.nn.sigmoid(_dot(hb, wbig_ref[:, g0 + d:g0 + 2 * d])).astype(BF16)


def _proj(x2, pos2, mod3, g_mix, wbig, g_q, wq, g_kv, wkv, wvt, wvbt, seq, tm):
    t, d = x2.shape
    q_lora, kv_lora = g_q.shape[1], g_kv.shape[1]
    hw = wkv.shape[1]
    per_b = seq // tm
    row = lambda i: (i, 0)
    full = lambda i: (0, 0)
    inv = ROPE_BASE ** (-(np.arange(LANES) % (ROPE_DIM // 2)).astype(np.float32) / (ROPE_DIM // 2))
    sgn = np.where((np.arange(LANES) % ROPE_DIM) < ROPE_DIM // 2, -1.0, 1.0).astype(np.float32)
    ones_col = (np.arange(wvt.shape[0]) % VT_ROWS == HEAD_DIM).astype(np.float32).reshape(-1, 1)
    v_t = jax.ShapeDtypeStruct((t // seq, wvt.shape[0], seq), BF16)
    outs = [jax.ShapeDtypeStruct((t, 2 * hw), BF16), jax.ShapeDtypeStruct((t, 2 * hw), BF16), v_t,
            jax.ShapeDtypeStruct((t, hw), BF16), jax.ShapeDtypeStruct((t, hw), BF16), v_t,
            jax.ShapeDtypeStruct((t, d), BF16), jax.ShapeDtypeStruct((t, d), BF16)]
    v_t_spec = pl.BlockSpec((1, wvt.shape[0], tm), lambda i: (i // per_b, 0, i % per_b))
    out_specs = [v_t_spec if o is v_t else pl.BlockSpec((tm, o.shape[1]), row) for o in outs]
    return pl.pallas_call(
        functools.partial(_proj_kernel, q_lora=q_lora, kv_lora=kv_lora),
        grid=(t // tm,),
        in_specs=[pl.BlockSpec((tm, d), row),
                  pl.BlockSpec((tm, 1), row),
                  pl.BlockSpec((1,) + mod3.shape[1:], lambda i: (i // per_b, 0, 0)),
                  pl.BlockSpec(g_mix.shape, full),
                  pl.BlockSpec(wbig.shape, full),
                  pl.BlockSpec(g_q.shape, full),
                  pl.BlockSpec(wq.shape, full),
                  pl.BlockSpec(g_kv.shape, full),
                  pl.BlockSpec(wkv.shape, full),
                  pl.BlockSpec(wvt.shape, full),
                  pl.BlockSpec(wvbt.shape, full),
                  pl.BlockSpec(ones_col.shape, full),
                  pl.BlockSpec((1, LANES), full),
                  pl.BlockSpec((1, LANES), full)],
        out_specs=out_specs,
        out_shape=outs,
        compiler_params=_cparams(("parallel",)),
        name="proj",
    )(x2, pos2, mod3, g_mix, wbig, g_q, wq, g_kv, wkv, wvt, wvbt, jnp.asarray(ones_col),
      jnp.asarray(inv).reshape(1, LANES), jnp.asarray(sgn).reshape(1, LANES))


def _mla_kernel(q_ref, k_ref, vt_ref, o_ref, sa, sb, *, tq):
    i = pl.program_id(2)
    q = q_ref[...]
    lane = lax.broadcasted_iota(jnp.int32, (1, 2 * LANES), 1)
    first = (lane < HEAD_DIM) | ((lane >= LANES) & (lane < LANES + ROPE_DIM))
    second = ((lane >= HEAD_DIM) & (lane < LANES)) | ((lane >= LANES + ROPE_DIM) & (lane < LANES + 2 * ROPE_DIM))
    zero = jnp.zeros_like(q)
    qs = (jnp.where(first, q, zero), jnp.where(second, q, zero))

    def scores(blk, buf):
        k = k_ref[pl.ds(pl.multiple_of(blk * tq, tq), tq), :]
        for h, qh in enumerate(qs):
            buf[h] = _dot_nt(k, qh)

    def consume(blk, buf, state, mask):
        start = pl.multiple_of(blk * tq, tq)
        ps, ms, alphas = [], [], []
        for h, (m, _) in enumerate(state):
            st = buf[h]
            if mask is not None:
                st = jnp.where(mask, st, NEG_INF)
            m_new = jnp.maximum(m, jnp.max(st, axis=0, keepdims=True))
            ms.append(m_new)
            alphas.append(jnp.exp2(m - m_new))
            ps.append(jnp.exp2(st - m_new).astype(BF16))
        new = []
        for h, (p, m_new, alpha, (_, acc)) in enumerate(zip(ps, ms, alphas, state)):
            vt = vt_ref[0, h * VT_ROWS:(h + 1) * VT_ROWS, pl.ds(start, tq)]
            new.append((m_new, alpha * acc + _dot(vt, p)))
        return tuple(new)

    def two_blocks(jj, state):
        scores(2 * jj + 1, sb)
        state = consume(2 * jj, sa, state, None)
        scores(2 * jj + 2, sa)
        return consume(2 * jj + 1, sb, state, None)

    init = tuple((jnp.full((1, tq), NEG_INF, F32), jnp.zeros((VT_ROWS, tq), F32)) for _ in range(2))
    scores(0, sa)
    state = lax.fori_loop(0, i // 2, two_blocks, init)
    kc = lax.broadcasted_iota(jnp.int32, (tq, tq), 0) // CHUNK
    qc = lax.broadcasted_iota(jnp.int32, (tq, tq), 1) // CHUNK

    def odd_tail(state):
        scores(i, sb)
        return consume(i, sb, consume(i - 1, sa, state, None), kc <= qc)

    (_, a0), (_, a1) = lax.cond(i % 2 == 1, odd_tail, lambda state: consume(i, sa, state, kc <= qc), state)
    out_t = jnp.concatenate([a[0:HEAD_DIM] * (1.0 / a[HEAD_DIM:HEAD_DIM + 1]) for a in (a0, a1)], axis=0)
    o_ref[...] = out_t.T.astype(o_ref.dtype)


def _mla(qa, ka, vt, bsz, seq, tq):
    t = qa.shape[0]
    pairs = qa.shape[1] // (2 * LANES)
    nq = seq // tq
    return pl.pallas_call(
        functools.partial(_mla_kernel, tq=tq),
        grid=(bsz, pairs, nq),
        in_specs=[pl.BlockSpec((tq, 2 * LANES), lambda b, p, i: (b * nq + i, p)),
                  pl.BlockSpec((seq, 2 * LANES), lambda b, p, i: (b, p)),
                  pl.BlockSpec((1, 2 * VT_ROWS, seq), lambda b, p, i: (b, p, 0))],
        out_specs=pl.BlockSpec((tq, LANES), lambda b, p, i: (b * nq + i, p)),
        out_shape=jax.ShapeDtypeStruct((t, pairs * LANES), BF16),
        scratch_shapes=[pltpu.VMEM((2, tq, tq), F32)] * 2,
        compiler_params=_cparams(("parallel", "parallel", "arbitrary")),
        name="mla",
    )(qa, ka, vt)


def _band_kernel(q_ref, k_ref, vt_ref, bias_ref, o_ref, *, tq, nblk):
    lane = lax.broadcasted_iota(jnp.int32, (1, LANES), 1)

    def scores(n):
        tile = first + n
        q = q_ref[n * tq:(n + 1) * tq, :]
        zero = jnp.zeros_like(q)
        qs = (jnp.where(lane < HEAD_DIM, q, zero), jnp.where(lane >= HEAD_DIM, q, zero))
        starts, valid = [], []
        for j in range(nblk):
            blk = tile - (nblk - 1) + j
            starts.append(pl.multiple_of(jnp.maximum(blk, 0) * tq, tq))
            valid.append(blk >= 0)
        ks = [k_ref[pl.ds(s, tq), :] for s in starts]
        return starts, [[jnp.where(valid[j], _dot_nt(ks[j], qh) + bias_ref[0, h, j * tq:(j + 1) * tq, :], NEG_INF)
                         for j in range(nblk)] for h, qh in enumerate(qs)]

    def finish(starts, sts):
        outs = []
        for h in range(2):
            m = functools.reduce(jnp.maximum, [jnp.max(st, axis=0, keepdims=True) for st in sts[h]])
            acc = functools.reduce(jnp.add, [
                _dot(vt_ref[0, h * VT_ROWS:(h + 1) * VT_ROWS, pl.ds(starts[j], tq)], jnp.exp2(st - m).astype(BF16))
                for j, st in enumerate(sts[h])])
            outs.append(acc[0:HEAD_DIM] * (1.0 / acc[HEAD_DIM:HEAD_DIM + 1]))
        return jnp.concatenate(outs, axis=0).T

    tiles_per_step = q_ref.shape[0] // tq
    first = pl.program_id(2) * tiles_per_step
    pending = scores(0)
    for n in range(tiles_per_step):
        upcoming = scores(n + 1) if n + 1 < tiles_per_step else None
        o_ref[n * tq:(n + 1) * tq, :] = finish(*pending).astype(o_ref.dtype)
        pending = upcoming


def _band_bias(rel_table, tq, nblk):
    nk = nblk * tq
    ring = nk + tq
    d = np.arange(ring)
    key_minus_query = np.where(d < nk, d, d - ring)
    idx = np.clip((nblk - 1) * tq - key_minus_query, -MAX_REL, MAX_REL) + MAX_REL
    per_offset = rel_table.astype(F32)[:, idx]
    heads = rel_table.shape[0]
    toeplitz = jnp.tile(per_offset, (1, tq))[:, :tq * (ring - 1)].reshape(heads, tq, ring - 1)[:, :, :nk]
    qc = np.arange(tq)[:, None] // CHUNK
    kc = np.arange(nk)[None, :] // CHUNK - ((nblk - 1) * tq // CHUNK - LEFT_CHUNKS)
    band = (kc >= qc) & (kc <= qc + LEFT_CHUNKS)
    bias = jnp.where(band[None], toeplitz * LOG2_E, NEG_INF)
    return bias.reshape(heads // 2, 2, tq, nk).swapaxes(2, 3)


def _band(qb, kb, vtb, bias, bsz, seq, tq, nblk, tiles_per_step):
    t, hw = qb.shape
    pairs = hw // LANES
    rows = tq * tiles_per_step
    nq = seq // rows
    return pl.pallas_call(
        functools.partial(_band_kernel, tq=tq, nblk=nblk),
        grid=(bsz, pairs, nq),
        in_specs=[pl.BlockSpec((rows, LANES), lambda b, p, i: (b * nq + i, p)),
                  pl.BlockSpec((seq, LANES), lambda b, p, i: (b, p)),
                  pl.BlockSpec((1, 2 * VT_ROWS, seq), lambda b, p, i: (b, p, 0)),
                  pl.BlockSpec((1,) + bias.shape[1:], lambda b, p, i: (p, 0, 0, 0))],
        out_specs=pl.BlockSpec((rows, LANES), lambda b, p, i: (b * nq + i, p)),
        out_shape=jax.ShapeDtypeStruct((t, hw), BF16),
        compiler_params=_cparams(("parallel", "parallel", "arbitrary")),
        name="band",
    )(qb, kb, vtb, bias)


def _merge_kernel(oa_ref, ob_ref, ga_ref, gb_ref, x_ref, mod_ref, woa_ref, wob_ref, wout_ref, gffn_ref,
                  wr_ref, br_ref, x1_ref, h2x_ref, rt_ref):
    merged = (ga_ref[...].astype(F32) * _dot(oa_ref[...], woa_ref[...])
              + gb_ref[...].astype(F32) * _dot(ob_ref[...], wob_ref[...]))
    mix = _dot(merged.astype(BF16), wout_ref[...])
    x1 = x_ref[...] + mod_ref[0, 2:3, :] * mix
    x1_ref[...] = x1
    h2 = _rms(x1, gffn_ref[...]) * (1.0 + mod_ref[0, 4:5, :]) + mod_ref[0, 3:4, :]

    h_hi = h2.astype(BF16)
    h_lo = (h2 - h_hi.astype(F32)).astype(BF16)
    r_hi = _dot_nt(wr_ref[...], h_hi)
    logits = (r_hi[0:LANES] + _dot_nt(wr_ref[0:LANES, :], h_lo) + r_hi[LANES:2 * LANES]) + br_ref[...]
    blk = EXPERTS_PER_GROUP
    sub = lax.broadcasted_iota(jnp.int32, (blk, logits.shape[1]), 0)
    far = jnp.int32(blk)
    is_g = sub < N_GROUPS
    gl = jnp.where(is_g, logits[0:blk], NEG_INF)
    g_max = jnp.max(gl, axis=0, keepdims=True)
    g_w = 1.0 / jnp.sum(jnp.where(is_g, jnp.exp(gl - g_max), 0.0), axis=0, keepdims=True)
    g_idx = jnp.min(jnp.where(gl == g_max, sub, far), axis=0, keepdims=True)
    el = logits[blk:2 * blk]
    for g in range(1, N_GROUPS):
        el = jnp.where(g_idx == g, logits[(g + 1) * blk:(g + 2) * blk], el)
    e1 = jnp.max(el, axis=0, keepdims=True)
    i1 = jnp.min(jnp.where(el == e1, sub, far), axis=0, keepdims=True)
    el2 = jnp.where(sub == i1, NEG_INF, el)
    e2 = jnp.max(el2, axis=0, keepdims=True)
    i2 = jnp.min(jnp.where(el2 == e2, sub, far), axis=0, keepdims=True)
    ratio = jnp.exp(e2 - e1)
    w1 = g_w / (1.0 + ratio)
    w2 = g_w * ratio / (1.0 + ratio)
    e_lo, e_hi = jnp.minimum(i1, i2), jnp.maximum(i1, i2)
    cls = (g_idx * PAIRS_PER_GROUP + jnp.right_shift(e_lo * (2 * EXPERTS_PER_GROUP - 1 - e_lo), 1)
           + (e_hi - e_lo - 1))
    first_is_lo = i1 < i2
    w_lo, w_hi = jnp.where(first_is_lo, w1, w2), jnp.where(first_is_lo, w2, w1)
    route_t = (jnp.where(sub == ROUTE_CLS, cls.astype(F32), 0.0) + jnp.where(sub == ROUTE_W_LO, w_lo, 0.0)
               + jnp.where(sub == ROUTE_W_HI, w_hi, 0.0))
    rt_ref[...] = route_t
    d = h2.shape[1]
    h2x_ref[:, 0:d] = h2
    h2x_ref[:, d:d + LANES] = jnp.concatenate(
        [route_t, jnp.zeros((LANES - ROUTE_ROWS, route_t.shape[1]), F32)], axis=0).T


def _merge(oa, ob, ga, gb, x2, mod3, woa, wob, wout, g_ffn, wr, br, seq, tm):
    t, d = x2.shape
    hw = oa.shape[1]
    per_b = seq // tm
    row = lambda i: (i, 0)
    full = lambda i: (0, 0)
    return pl.pallas_call(
        _merge_kernel,
        grid=(t // tm,),
        in_specs=[pl.BlockSpec((tm, hw), row), pl.BlockSpec((tm, hw), row),
                  pl.BlockSpec((tm, d), row), pl.BlockSpec((tm, d), row),
                  pl.BlockSpec((tm, d), row),
                  pl.BlockSpec((1,) + mod3.shape[1:], lambda i: (i // per_b, 0, 0)),
                  pl.BlockSpec(woa.shape, full), pl.BlockSpec(wob.shape, full),
                  pl.BlockSpec(wout.shape, full), pl.BlockSpec(g_ffn.shape, full),
                  pl.BlockSpec(wr.shape, full), pl.BlockSpec(br.shape, full)],
        out_specs=[pl.BlockSpec((tm, d), row), pl.BlockSpec((tm, d + LANES), row),
                   pl.BlockSpec((ROUTE_ROWS, tm), lambda i: (0, i))],
        out_shape=[jax.ShapeDtypeStruct((t, d), F32), jax.ShapeDtypeStruct((t, d + LANES), F32),
                   jax.ShapeDtypeStruct((ROUTE_ROWS, t), F32)],
        compiler_params=_cparams(("parallel",)),
        name="merge",
    )(oa, ob, ga, gb, x2, mod3, woa, wob, wout, g_ffn, wr, br)


def _plan_kernel(rt_ref, pos_ref, cnt_ref, carry_ref, off_ref, *, tc):
    phase, i = pl.program_id(0), pl.program_id(1)
    cls = rt_ref[ROUTE_CLS:ROUTE_CLS + 1, :]
    onehot = lax.broadcasted_iota(jnp.int32, (LANES, tc), 0).astype(F32) == cls
    per_class = jnp.sum(onehot.astype(F32), axis=1, keepdims=True)

    @pl.when((phase == 0) & (i == 0))
    def _():
        cnt_ref[...] = jnp.zeros_like(cnt_ref)

    @pl.when(phase == 0)
    def _():
        cnt_ref[...] += per_class

    @pl.when((phase == 1) & (i == 0))
    def _():
        padded = jnp.floor((cnt_ref[...] + (MOE_ROWS - 1)) * (1.0 / MOE_ROWS)) * MOE_ROWS
        hi = jnp.floor(padded * (1.0 / 256.0))
        digits = [jnp.broadcast_to(v, (LANES, LANES)).astype(BF16) for v in (hi, padded - 256.0 * hi)]
        below = (lax.broadcasted_iota(jnp.int32, (LANES, LANES), 1)
                 < lax.broadcasted_iota(jnp.int32, (LANES, LANES), 0)).astype(BF16)
        off = 256.0 * _dot(below, digits[0]) + _dot(below, digits[1])
        off_ref[...] = off[:, 0:1]
        carry_ref[...] = jnp.zeros_like(carry_ref)

    @pl.when(phase == 1)
    def _():
        earlier = (lax.broadcasted_iota(jnp.int32, (tc, tc), 0)
                   < lax.broadcasted_iota(jnp.int32, (tc, tc), 1)).astype(BF16)
        rank = _dot(onehot.astype(BF16), earlier)
        row = rank + (carry_ref[...] + off_ref[...])
        pos_ref[...] = jnp.sum(jnp.where(onehot, row, 0.0), axis=0, keepdims=True).astype(jnp.int32)
        carry_ref[...] += per_class


def _plan(route_t, tc):
    t = route_t.shape[1]
    return pl.pallas_call(
        functools.partial(_plan_kernel, tc=tc),
        grid=(2, t // tc),
        in_specs=[pl.BlockSpec((ROUTE_ROWS, tc), lambda ph, i: (0, i))],
        out_specs=[pl.BlockSpec((1, tc), lambda ph, i: (0, i * ph)),
                   pl.BlockSpec((LANES, 1), lambda ph, i: (0, 0))],
        out_shape=[jax.ShapeDtypeStruct((1, t), jnp.int32), jax.ShapeDtypeStruct((LANES, 1), F32)],
        scratch_shapes=[pltpu.VMEM((LANES, 1), F32), pltpu.VMEM((LANES, 1), F32)],
        compiler_params=_cparams(("arbitrary", "arbitrary")),
        name="plan",
    )(route_t)


def _tile_tables(counts, n_tiles):
    pairs = [(lo, hi) for lo in range(EXPERTS_PER_GROUP) for hi in range(lo + 1, EXPERTS_PER_GROUP)]
    exp_lo = np.array([g * EXPERTS_PER_GROUP + lo for g in range(N_GROUPS) for lo, _ in pairs], np.int32)
    exp_hi = np.array([g * EXPERTS_PER_GROUP + hi for g in range(N_GROUPS) for _, hi in pairs], np.int32)
    cnt = counts.reshape(-1)[:N_CLASSES].astype(jnp.int32)
    tiles_per_class = (cnt + (MOE_ROWS - 1)) // MOE_ROWS
    tile_end = jnp.cumsum(tiles_per_class)
    n_used = tile_end[-1]
    j = jnp.arange(n_tiles, dtype=jnp.int32)
    tile_cls = jnp.sum(tile_end[None, :] <= jnp.minimum(j, n_used - 1)[:, None], axis=1, dtype=jnp.int32)
    last_of_class = jnp.any((j[:, None] == tile_end[None, :] - 1) & (tiles_per_class[None, :] > 0), axis=1)
    zero_fill = (last_of_class | (j >= n_used)).astype(jnp.int32)
    return (jnp.asarray(exp_lo)[tile_cls], jnp.asarray(exp_hi)[tile_cls], n_used.reshape(1).astype(jnp.int32),
            zero_fill)


def _row_copy(src, src_row, dst, dst_row, n, sem):
    return pltpu.make_async_copy(src.at[pl.ds(src_row, n)], dst.at[pl.ds(dst_row, n)], sem)


def _dispatch_kernel(pos_ref, zf_ref, h_ref, hs_ref, zero_ref, sem, *, td, n_tiles):
    i = pl.program_id(0)

    @pl.when(i == 0)
    def _():
        zero_ref[...] = jnp.zeros_like(zero_ref)

        def fill(j, n):
            @pl.when(zf_ref[j] > 0)
            def _():
                _row_copy(zero_ref, 0, hs_ref, pl.multiple_of(j * MOE_ROWS, MOE_ROWS), MOE_ROWS, sem).start()
            return n + zf_ref[j]

        n_fill = lax.fori_loop(0, n_tiles, fill, jnp.int32(0))

        def drain(_, carry):
            _row_copy(zero_ref, 0, hs_ref, 0, MOE_ROWS, sem).wait()
            return carry

        lax.fori_loop(0, n_fill, drain, 0)

    base = i * td

    def send(r, carry):
        _row_copy(h_ref, r, hs_ref, pos_ref[base + r], 1, sem).start()
        return carry

    lax.fori_loop(0, td, send, 0, unroll=8)
    _row_copy(h_ref, 0, hs_ref, 0, td, sem).wait()


def _dispatch(h2x, pos, zero_fill, n_tiles, td):
    t, w = h2x.shape
    return pl.pallas_call(
        functools.partial(_dispatch_kernel, td=td, n_tiles=n_tiles),
        grid_spec=pltpu.PrefetchScalarGridSpec(
            num_scalar_prefetch=2,
            grid=(t // td,),
            in_specs=[pl.BlockSpec((td, w), lambda i, pos, zf: (i, 0))],
            out_specs=pl.BlockSpec(memory_space=pl.ANY),
            scratch_shapes=[pltpu.VMEM((MOE_ROWS, w), F32), pltpu.SemaphoreType.DMA(())]),
        out_shape=jax.ShapeDtypeStruct((n_tiles * MOE_ROWS, w), F32),
        compiler_params=_cparams(("arbitrary",)),
        name="dispatch",
    )(pos, zero_fill, h2x)


def _moe_kernel(elo_ref, ehi_ref, nu_ref, hs_ref, wg_lo, wu_lo, wd_lo, wg_hi, wu_hi, wd_hi, y_ref):
    j = pl.program_id(0)
    d = y_ref.shape[1]

    @pl.when(j < nu_ref[0])
    def _():
        h = hs_ref[:, 0:d].astype(BF16)
        ups = [(_dot(h, wg[0]), _dot(h, wu[0])) for wg, wu in ((wg_lo, wu_lo), (wg_hi, wu_hi))]
        hids = [((a * jax.nn.sigmoid(a)) * u * hs_ref[:, d + lane:d + lane + 1]).astype(BF16)
                for (a, u), lane in zip(ups, (ROUTE_W_LO, ROUTE_W_HI))]
        y_ref[...] = _dot(hids[0], wd_lo[0]) + _dot(hids[1], wd_hi[0])

    @pl.when(j >= nu_ref[0])
    def _():
        y_ref[...] = jnp.zeros_like(y_ref)


def _moe(hs, exp_lo, exp_hi, n_used, wg, wu, wd):
    n_tiles = hs.shape[0] // MOE_ROWS
    _, d, ff = wg.shape
    up = lambda sel: pl.BlockSpec((1, d, ff), lambda j, elo, ehi, nu: ((elo, ehi)[sel][j], 0, 0))
    down = lambda sel: pl.BlockSpec((1, ff, d), lambda j, elo, ehi, nu: ((elo, ehi)[sel][j], 0, 0))
    return pl.pallas_call(
        _moe_kernel,
        grid_spec=pltpu.PrefetchScalarGridSpec(
            num_scalar_prefetch=3,
            grid=(n_tiles,),
            in_specs=[pl.BlockSpec((MOE_ROWS, hs.shape[1]),
                                   lambda j, elo, ehi, nu: (jnp.minimum(j, nu[0] - 1), 0)),
                      up(0), up(0), down(0), up(1), up(1), down(1)],
            out_specs=pl.BlockSpec((MOE_ROWS, d), lambda j, elo, ehi, nu: (j, 0))),
        out_shape=jax.ShapeDtypeStruct((n_tiles * MOE_ROWS, d), F32),
        compiler_params=_cparams(("arbitrary",)),
        name="moe",
    )(exp_lo, exp_hi, n_used, hs, wg, wu, wd, wg, wu, wd)


def _final_kernel(pos_ref, x1_ref, mod_ref, gfin_ref, y_ref, o_ref, rows_ref, sem, *, tf):
    i, n = pl.program_id(0), pl.num_programs(0)

    def fetch(step, slot):
        def one(r, carry):
            _row_copy(y_ref, pos_ref[step * tf + r], rows_ref.at[slot], r, 1, sem.at[slot]).start()
            return carry

        lax.fori_loop(0, tf, one, 0, unroll=8)

    @pl.when(i == 0)
    def _():
        fetch(0, 0)

    @pl.when(i + 1 < n)
    def _():
        fetch(i + 1, (i + 1) % 2)

    slot = i % 2
    _row_copy(y_ref, 0, rows_ref.at[slot], 0, tf, sem.at[slot]).wait()
    x2 = x1_ref[...] + mod_ref[0, 5:6, :] * rows_ref[slot]
    o_ref[...] = _rms(x2, gfin_ref[...])


def _final(y, pos, x1, mod3, g_final, seq, tf):
    t, d = x1.shape
    per_b = seq // tf
    return pl.pallas_call(
        functools.partial(_final_kernel, tf=tf),
        grid_spec=pltpu.PrefetchScalarGridSpec(
            num_scalar_prefetch=1,
            grid=(t // tf,),
            in_specs=[pl.BlockSpec((tf, d), lambda i, pos: (i, 0)),
                      pl.BlockSpec((1,) + mod3.shape[1:], lambda i, pos: (i // per_b, 0, 0)),
                      pl.BlockSpec((1, d), lambda i, pos: (0, 0)),
                      pl.BlockSpec(memory_space=pl.ANY)],
            out_specs=pl.BlockSpec((tf, d), lambda i, pos: (i, 0)),
            scratch_shapes=[pltpu.VMEM((2, tf, d), F32), pltpu.SemaphoreType.DMA((2,))]),
        out_shape=jax.ShapeDtypeStruct((t, d), F32),
        compiler_params=_cparams(("arbitrary",)),
        name="final",
    )(pos, x1, mod3, g_final, y)


def _layout_weights(w_in, w_uq, w_uk, w_uv, heads, q_lora, kv_lora, d):
    hw = heads * HEAD_DIM
    cuts = np.cumsum([q_lora, kv_lora, ROPE_DIM, hw, hw, hw, d, d])
    w_qlat, w_kvlat, w_kr, w_qb, w_kb, w_vb, w_ga, w_gb = jnp.split(w_in, [int(v) for v in cuts[:-1]], axis=1)
    swap = np.concatenate([np.arange(ROPE_DIM // 2, ROPE_DIM), np.arange(ROPE_DIM // 2)])
    w_kr_sw = w_kr[:, swap]
    kr_blk = jnp.concatenate([w_kr, w_kr, w_kr_sw, w_kr_sw], axis=1)
    band_scale = HEAD_DIM ** -0.5 * LOG2_E
    wbig = jnp.concatenate([w_qlat, w_kvlat, kr_blk, w_qb * band_scale, w_kb, w_ga, w_gb], axis=1)

    def transposed_values(w, k):
        return jnp.pad(w.reshape(k, heads, HEAD_DIM).transpose(1, 2, 0),
                       ((0, 0), (0, VT_ROWS - HEAD_DIM), (0, 0))).reshape(heads * VT_ROWS, k)

    qk = HEAD_DIM + ROPE_DIM
    wq3 = w_uq.reshape(q_lora, heads, qk) * (qk ** -0.5 * LOG2_E)
    nope = wq3[:, :, :HEAD_DIM].reshape(q_lora, hw)
    rope = wq3[:, :, HEAD_DIM:]
    pad = jnp.zeros((q_lora, heads // 2, LANES - 2 * ROPE_DIM), w_uq.dtype)

    def lay(r):
        return jnp.concatenate([r.reshape(q_lora, heads // 2, 2 * ROPE_DIM), pad], axis=2).reshape(q_lora, hw)

    wq = jnp.concatenate([nope, lay(rope), lay(rope[:, :, swap])], axis=1)
    return (wbig.astype(BF16), wq.astype(BF16), w_uk.astype(BF16), transposed_values(w_uv, kv_lora).astype(BF16),
            transposed_values(w_vb, d).astype(BF16))


def kernel(x, c, positions, w_ada, b_ada, g_mix, w_in, g_q, w_uq, g_kv, w_uk, w_uv, rel_bias, w_oa, w_ob,
           w_out, g_ffn, w_rg, b_rg, w_re, b_re, w_gate, w_up, w_down, g_final):
    bsz, seq, d = x.shape
    depth = w_ada.shape[0]
    t = bsz * seq
    heads = rel_bias.shape[1]
    q_lora, kv_lora = g_q.shape[1], g_kv.shape[1]
    tq = 256
    tq_mla = 512
    nblk = -(-LEFT_CHUNKS * CHUNK // tq) + 1
    assert seq % tq == 0 and seq % tq_mla == 0 and tq % CHUNK == 0 and heads % 2 == 0
    assert ROUTE_ROWS == EXPERTS_PER_GROUP and N_GROUPS <= EXPERTS_PER_GROUP

    x2 = x.reshape(t, d)
    pos2 = positions.reshape(t, 1)
    out = x2
    for l in range(depth):
        mod3 = _ada(c, w_ada[l], b_ada[l]).reshape(bsz, -1, d)
        wbig, wq, wkv, wvt, wvbt = _layout_weights(w_in[l], w_uq[l], w_uk[l], w_uv[l], heads, q_lora, kv_lora, d)
        qa, ka, vt, qb, kb, vtb, ga, gb = _proj(
            x2, pos2, mod3, g_mix[l].reshape(1, d), wbig, g_q[l].reshape(1, -1), wq, g_kv[l].reshape(1, -1), wkv,
            wvt, wvbt, seq, tm=512)
        oa = _mla(qa, ka, vt, bsz, seq, tq_mla)
        ob = _band(qb, kb, vtb, _band_bias(rel_bias[l], tq, nblk), bsz, seq, tq, nblk, tiles_per_step=4)

        pad_g = EXPERTS_PER_GROUP - N_GROUPS
        n_route = (N_GROUPS + 1) * EXPERTS_PER_GROUP
        w_r = jnp.concatenate([w_rg[l], jnp.zeros((d, pad_g), F32), w_re[l],
                               jnp.zeros((d, LANES - n_route), F32)], axis=1).T
        b_r = jnp.concatenate([b_rg[l], jnp.zeros((pad_g,), F32), b_re[l],
                               jnp.zeros((LANES - n_route,), F32)]).reshape(LANES, 1)
        w_r_hi = w_r.astype(BF16)
        w_r_lo = (w_r - w_r_hi.astype(F32)).astype(BF16)
        x1, h2x, route_t = _merge(oa, ob, ga, gb, x2, mod3, w_oa[l].astype(BF16), w_ob[l].astype(BF16),
                                  w_out[l].astype(BF16), g_ffn[l].reshape(1, d),
                                  jnp.concatenate([w_r_hi, w_r_lo], axis=0), b_r, seq, tm=512)
        n_tiles = t // MOE_ROWS + N_CLASSES
        pos, counts = _plan(route_t, tc=1024)
        pos = pos.reshape(t)
        exp_lo, exp_hi, n_used, zero_fill = _tile_tables(counts, n_tiles)
        hs = _dispatch(h2x, pos, zero_fill, n_tiles, td=2048)
        y = _moe(hs, exp_lo, exp_hi, n_used, w_gate[l].astype(BF16), w_up[l].astype(BF16), w_down[l].astype(BF16))
        assert l == depth - 1, "multi-layer stacks need an un-normalised residual output"
        out = _final(y, pos, x1, mod3, g_final.reshape(1, d), seq, tf=512)
    return out.reshape(bsz, seq, d)
```

```python
import functools

import numpy as np
import jax
import jax.numpy as jnp
from jax import lax
from jax.experimental import pallas as pl
from jax.experimental.pallas import tpu as pltpu

F32 = jnp.float32
BF16 = jnp.bfloat16

NORM_EPS = 1e-6
NEG_INF = -1e30
ROPE_BASE = 10000.0
CHUNK = 64
LEFT_CHUNKS = 8
MAX_REL = 256
N_GROUPS = 4
EXPERTS_PER_GROUP = 8
PAIRS_PER_GROUP = EXPERTS_PER_GROUP * (EXPERTS_PER_GROUP - 1) // 2
N_CLASSES = N_GROUPS * PAIRS_PER_GROUP
MOE_ROWS = 256
ZERO_ROWS = 64
ROUTE_CLS, ROUTE_W_LO, ROUTE_W_HI = 0, 1, 2
ROUTE_ROWS = 8
LANES = 128
HEAD_DIM = 64
ROPE_DIM = 32
VT_ROWS = HEAD_DIM + 16
LOG2_E = 1.4426950408889634

VMEM_LIMIT = 56 * 1024 * 1024


def _cparams(sem):
    return pltpu.CompilerParams(dimension_semantics=sem, vmem_limit_bytes=VMEM_LIMIT)


def _dot(a, b):
    return jnp.dot(a, b, preferred_element_type=F32)


def _dot_nt(a, b):
    return lax.dot_general(a, b, (((1,), (1,)), ((), ())), preferred_element_type=F32)


def _rms(x, g):
    return x * lax.rsqrt(jnp.mean(x * x, axis=-1, keepdims=True) + NORM_EPS) * g


def _ada_kernel(c_ref, w_ref, b_ref, o_ref):
    c = c_ref[...]
    o_ref[...] = _dot(c * jax.nn.sigmoid(c), w_ref[...]) + b_ref[...]


def _ada(c, w_ada, b_ada):
    bsz, d = c.shape
    n = w_ada.shape[1]
    return pl.pallas_call(
        _ada_kernel,
        grid=(n // d,),
        in_specs=[pl.BlockSpec((bsz, d), lambda j: (0, 0)),
                  pl.BlockSpec((d, d), lambda j: (0, j)),
                  pl.BlockSpec((1, d), lambda j: (0, j))],
        out_specs=pl.BlockSpec((bsz, d), lambda j: (0, j)),
        out_shape=jax.ShapeDtypeStruct((bsz, n), F32),
        compiler_params=_cparams(("arbitrary",)),
        name="ada",
    )(c, w_ada, b_ada.reshape(1, n))


def _proj_kernel(x_ref, pos_ref, mod_ref, gmix_ref, wbig_ref, gq_ref, wq_ref, gkv_ref, wkv_ref,
                 wvt_ref, wvbt_ref, one_ref, inv_ref, sgn_ref,
                 qa_ref, ka_ref, vt_ref, qb_ref, kb_ref, vtb_ref, ga_ref, gb_ref, *, q_lora, kv_lora):
    x = x_ref[...]
    h = _rms(x, gmix_ref[...]) * (1.0 + mod_ref[0, 1:2, :]) + mod_ref[0, 0:1, :]
    hb = h.astype(BF16)

    c0 = q_lora + kv_lora + LANES
    head = _dot(hb, wbig_ref[:, 0:c0])
    q_lat = head[:, 0:q_lora]
    kv_lat = head[:, q_lora:q_lora + kv_lora]
    kr_blk = head[:, q_lora + kv_lora:c0]
    hw = qb_ref.shape[1]
    d = x.shape[1]
    g0 = c0 + 2 * hw
    ga_ref[...] = jax.nn.sigmoid(_dot(hb, wbig_ref[:, g0:g0 + d])).astype(BF16)

    ang = pos_ref[...].astype(F32) * inv_ref[...]
    live = lax.broadcasted_iota(jnp.int32, (1, LANES), 1) < 2 * ROPE_DIM
    cos_t = jnp.where(live, jnp.cos(ang), 0.0)
    sin_t = jnp.where(live, jnp.sin(ang) * sgn_ref[...], 0.0)

    qn = _rms(q_lat, gq_ref[...]).astype(BF16)
    q_all = _dot(qn, wq_ref[...])
    kvn = _rms(kv_lat, gkv_ref[...]).astype(BF16)
    k_nope = _dot(kvn, wkv_ref[...])
    vt_ref[0] = (_dot_nt(wvt_ref[...], kvn) + one_ref[...]).astype(BF16)
    gb_ref[...] = jax.nn.sigmoid(_dot(hb, wbig_ref[:, g0 + d:g0 + 2 * d])).astype(BF16)
    k_rot = (kr_blk * cos_t + pltpu.roll(kr_blk, LANES // 2, 1) * sin_t).astype(BF16)
    for p in range(hw // LANES):
        lo, hi = p * LANES, (p + 1) * LANES
        qa_ref[:, 2 * lo:2 * lo + LANES] = q_all[:, lo:hi].astype(BF16)
        qa_ref[:, 2 * lo + LANES:2 * hi] = (
            q_all[:, hw + lo:hw + hi] * cos_t + q_all[:, 2 * hw + lo:2 * hw + hi] * sin_t).astype(BF16)
        ka_ref[:, 2 * lo:2 * lo + LANES] = k_nope[:, lo:hi].astype(BF16)
        ka_ref[:, 2 * lo + LANES:2 * hi] = k_rot

    qb_ref[...] = _dot(hb, wbig_ref[:, c0:c0 + hw]).astype(BF16)
    kb_ref[...] = _dot(hb, wbig_ref[:, c0 + hw:c0 + 2 * hw]).astype(BF16)
    vtb_ref[0] = (_dot_nt(wvbt_ref[...], hb) + one_ref[...]).astype(BF16)


def _proj(x2, pos2, mod3, g_mix, wbig, g_q, wq, g_kv, wkv, wvt, wvbt, seq, tm):
    t, d = x2.shape
    q_lora, kv_lora = g_q.shape[1], g_kv.shape[1]
    hw = wkv.shape[1]
    per_b = seq // tm
    row = lambda i: (i, 0)
    full = lambda i: (0, 0)
    inv = ROPE_BASE ** (-(np.arange(LANES) % (ROPE_DIM // 2)).astype(np.float32) / (ROPE_DIM // 2))
    sgn = np.where((np.arange(LANES) % ROPE_DIM) < ROPE_DIM // 2, -1.0, 1.0).astype(np.float32)
    ones_col = (np.arange(wvt.shape[0]) % VT_ROWS == HEAD_DIM).astype(np.float32).reshape(-1, 1)
    v_t = jax.ShapeDtypeStruct((t // seq, wvt.shape[0], seq), BF16)
    outs = [jax.ShapeDtypeStruct((t, 2 * hw), BF16), jax.ShapeDtypeStruct((t, 2 * hw), BF16), v_t,
            jax.ShapeDtypeStruct((t, hw), BF16), jax.ShapeDtypeStruct((t, hw), BF16), v_t,
            jax.ShapeDtypeStruct((t, d), BF16), jax.ShapeDtypeStruct((t, d), BF16)]
    v_t_spec = pl.BlockSpec((1, wvt.shape[0], tm), lambda i: (i // per_b, 0, i % per_b))
    out_specs = [v_t_spec if o is v_t else pl.BlockSpec((tm, o.shape[1]), row) for o in outs]
    return pl.pallas_call(
        functools.partial(_proj_kernel, q_lora=q_lora, kv_lora=kv_lora),
        grid=(t // tm,),
        in_specs=[pl.BlockSpec((tm, d), row),
                  pl.BlockSpec((tm, 1), row),
                  pl.BlockSpec((1,) + mod3.shape[1:], lambda i: (i // per_b, 0, 0)),
                  pl.BlockSpec(g_mix.shape, full),
                  pl.BlockSpec(wbig.shape, full),
                  pl.BlockSpec(g_q.shape, full),
                  pl.BlockSpec(wq.shape, full),
                  pl.BlockSpec(g_kv.shape, full),
                  pl.BlockSpec(wkv.shape, full),
                  pl.BlockSpec(wvt.shape, full),
                  pl.BlockSpec(wvbt.shape, full),
                  pl.BlockSpec(ones_col.shape, full),
                  pl.BlockSpec((1, LANES), full),
                  pl.BlockSpec((1, LANES), full)],
        out_specs=out_specs,
        out_shape=outs,
        compiler_params=_cparams(("parallel",)),
        name="proj",
    )(x2, pos2, mod3, g_mix, wbig, g_q, wq, g_kv, wkv, wvt, wvbt, jnp.asarray(ones_col),
      jnp.asarray(inv).reshape(1, LANES), jnp.asarray(sgn).reshape(1, LANES))


def _mla_kernel(q_ref, k_ref, vt_ref, o_ref, sa, sb, *, tq):
    i, nq = pl.program_id(2), pl.num_programs(2)
    lane = lax.broadcasted_iota(jnp.int32, (1, 2 * LANES), 1)
    first = (lane < HEAD_DIM) | ((lane >= LANES) & (lane < LANES + ROPE_DIM))
    second = ((lane >= HEAD_DIM) & (lane < LANES)) | ((lane >= LANES + ROPE_DIM) & (lane < LANES + 2 * ROPE_DIM))

    def head_queries(tile):
        q = q_ref[pl.ds(pl.multiple_of(tile * tq, tq), tq), :]
        zero = jnp.zeros_like(q)
        return (jnp.where(first, q, zero), jnp.where(second, q, zero))

    qs = head_queries(i)

    def scores(queries, blk, buf):
        k = k_ref[pl.ds(pl.multiple_of(blk * tq, tq), tq), :]
        for h, qh in enumerate(queries):
            buf[h] = _dot_nt(k, qh)

    def consume(blk, buf, state, mask):
        start = pl.multiple_of(blk * tq, tq)
        ps, ms, alphas = [], [], []
        for h, (m, _) in enumerate(state):
            st = buf[h]
            if mask is not None:
                st = jnp.where(mask, st, NEG_INF)
            m_new = jnp.maximum(m, jnp.max(st, axis=0, keepdims=True))
            ms.append(m_new)
            alphas.append(jnp.exp2(m - m_new))
            ps.append(jnp.exp2(st - m_new).astype(BF16))
        new = []
        for h, (p, m_new, alpha, (_, acc)) in enumerate(zip(ps, ms, alphas, state)):
            vt = vt_ref[0, h * VT_ROWS:(h + 1) * VT_ROWS, pl.ds(start, tq)]
            new.append((m_new, alpha * acc + _dot(vt, p)))
        return tuple(new)

    causal = (lax.broadcasted_iota(jnp.int32, (tq, tq), 0) // CHUNK
              <= lax.broadcasted_iota(jnp.int32, (tq, tq), 1) // CHUNK)
    following = jnp.minimum(i + 1, nq - 1)

    def tile(sx, sy):
        @pl.when(i == 0)
        def _():
            scores(qs, 0, sx)

        def two_blocks(jj, state):
            scores(qs, 2 * jj + 1, sy)
            state = consume(2 * jj, sx, state, None)
            scores(qs, 2 * jj + 2, sx)
            return consume(2 * jj + 1, sy, state, None)

        def even_tail(state):
            scores(head_queries(following), 0, sy)
            return consume(i, sx, state, causal)

        def odd_tail(state):
            scores(qs, i, sy)
            state = consume(i - 1, sx, state, None)
            scores(head_queries(following), 0, sx)
            return consume(i, sy, state, causal)

        init = tuple((jnp.full((1, tq), NEG_INF, F32), jnp.zeros((VT_ROWS, tq), F32)) for _ in range(2))
        state = lax.fori_loop(0, i // 2, two_blocks, init)
        (_, a0), (_, a1) = lax.cond(i % 2 == 1, odd_tail, even_tail, state)
        out_t = jnp.concatenate([a[0:HEAD_DIM] * (1.0 / a[HEAD_DIM:HEAD_DIM + 1]) for a in (a0, a1)], axis=0)
        o_ref[...] = out_t.T.astype(o_ref.dtype)

    swapped = ((i + 1) // 2) % 2

    @pl.when(swapped == 0)
    def _():
        tile(sa, sb)

    @pl.when(swapped == 1)
    def _():
        tile(sb, sa)


def _mla(qa, ka, vt, bsz, seq, tq):
    t = qa.shape[0]
    pairs = qa.shape[1] // (2 * LANES)
    nq = seq // tq
    return pl.pallas_call(
        functools.partial(_mla_kernel, tq=tq),
        grid=(bsz, pairs, nq),
        in_specs=[pl.BlockSpec((seq, 2 * LANES), lambda b, p, i: (b, p)),
                  pl.BlockSpec((seq, 2 * LANES), lambda b, p, i: (b, p)),
                  pl.BlockSpec((1, 2 * VT_ROWS, seq), lambda b, p, i: (b, p, 0))],
        out_specs=pl.BlockSpec((tq, LANES), lambda b, p, i: (b * nq + i, p)),
        out_shape=jax.ShapeDtypeStruct((t, pairs * LANES), BF16),
        scratch_shapes=[pltpu.VMEM((2, tq, tq), F32)] * 2,
        compiler_params=_cparams(("parallel", "parallel", "arbitrary")),
        name="mla",
    )(qa, ka, vt)


def _band_kernel(q_ref, k_ref, vt_ref, bias_ref, o_ref, *, tq, nblk):
    lane = lax.broadcasted_iota(jnp.int32, (1, LANES), 1)

    def scores(n):
        tile = first + n
        q = q_ref[n * tq:(n + 1) * tq, :]
        zero = jnp.zeros_like(q)
        qs = (jnp.where(lane < HEAD_DIM, q, zero), jnp.where(lane >= HEAD_DIM, q, zero))
        starts, valid = [], []
        for j in range(nblk):
            blk = tile - (nblk - 1) + j
            starts.append(pl.multiple_of(jnp.maximum(blk, 0) * tq, tq))
            valid.append(blk >= 0)
        ks = [k_ref[pl.ds(s, tq), :] for s in starts]
        return starts, [[jnp.where(valid[j], _dot_nt(ks[j], qh) + bias_ref[0, h, j * tq:(j + 1) * tq, :], NEG_INF)
                         for j in range(nblk)] for h, qh in enumerate(qs)]

    def finish(starts, sts):
        outs = []
        for h in range(2):
            m = functools.reduce(jnp.maximum, [jnp.max(st, axis=0, keepdims=True) for st in sts[h]])
            acc = functools.reduce(jnp.add, [
                _dot(vt_ref[0, h * VT_ROWS:(h + 1) * VT_ROWS, pl.ds(starts[j], tq)], jnp.exp2(st - m).astype(BF16))
                for j, st in enumerate(sts[h])])
            outs.append(acc[0:HEAD_DIM] * (1.0 / acc[HEAD_DIM:HEAD_DIM + 1]))
        return jnp.concatenate(outs, axis=0).T

    tiles_per_step = q_ref.shape[0] // tq
    first = pl.program_id(2) * tiles_per_step
    pending = scores(0)
    for n in range(tiles_per_step):
        upcoming = scores(n + 1) if n + 1 < tiles_per_step else None
        o_ref[n * tq:(n + 1) * tq, :] = finish(*pending).astype(o_ref.dtype)
        pending = upcoming


def _band_bias(rel_table, tq, nblk):
    nk = nblk * tq
    ring = nk + tq
    d = np.arange(ring)
    key_minus_query = np.where(d < nk, d, d - ring)
    idx = np.clip((nblk - 1) * tq - key_minus_query, -MAX_REL, MAX_REL) + MAX_REL
    per_offset = rel_table.astype(F32)[:, idx]
    heads = rel_table.shape[0]
    toeplitz = jnp.tile(per_offset, (1, tq))[:, :tq * (ring - 1)].reshape(heads, tq, ring - 1)[:, :, :nk]
    qc = np.arange(tq)[:, None] // CHUNK
    kc = np.arange(nk)[None, :] // CHUNK - ((nblk - 1) * tq // CHUNK - LEFT_CHUNKS)
    band = (kc >= qc) & (kc <= qc + LEFT_CHUNKS)
    bias = jnp.where(band[None], toeplitz * LOG2_E, NEG_INF)
    return bias.reshape(heads // 2, 2, tq, nk).swapaxes(2, 3)


def _band(qb, kb, vtb, bias, bsz, seq, tq, nblk, tiles_per_step):
    t, hw = qb.shape
    pairs = hw // LANES
    rows = tq * tiles_per_step
    nq = seq // rows
    return pl.pallas_call(
        functools.partial(_band_kernel, tq=tq, nblk=nblk),
        grid=(bsz, pairs, nq),
        in_specs=[pl.BlockSpec((rows, LANES), lambda b, p, i: (b * nq + i, p)),
                  pl.BlockSpec((seq, LANES), lambda b, p, i: (b, p)),
                  pl.BlockSpec((1, 2 * VT_ROWS, seq), lambda b, p, i: (b, p, 0)),
                  pl.BlockSpec((1,) + bias.shape[1:], lambda b, p, i: (p, 0, 0, 0))],
        out_specs=pl.BlockSpec((rows, LANES), lambda b, p, i: (b * nq + i, p)),
        out_shape=jax.ShapeDtypeStruct((t, hw), BF16),
        compiler_params=_cparams(("parallel", "parallel", "arbitrary")),
        name="band",
    )(qb, kb, vtb, bias)


def _merge_kernel(oa_ref, ob_ref, ga_ref, gb_ref, x_ref, mod_ref, woa_ref, wob_ref, wout_ref, gffn_ref,
                  wr_ref, br_ref, x1_ref, h2x_ref, rt_ref):
    merged = (ga_ref[...].astype(F32) * _dot(oa_ref[...], woa_ref[...])
              + gb_ref[...].astype(F32) * _dot(ob_ref[...], wob_ref[...]))
    mix = _dot(merged.astype(BF16), wout_ref[...])
    x1 = x_ref[...] + mod_ref[0, 2:3, :] * mix
    x1_ref[...] = x1
    h2 = _rms(x1, gffn_ref[...]) * (1.0 + mod_ref[0, 4:5, :]) + mod_ref[0, 3:4, :]

    h_hi = h2.astype(BF16)
    h_lo = (h2 - h_hi.astype(F32)).astype(BF16)
    r_hi = _dot_nt(wr_ref[...], h_hi)
    logits = (r_hi[0:LANES] + _dot_nt(wr_ref[0:LANES, :], h_lo) + r_hi[LANES:2 * LANES]) + br_ref[...]
    blk = EXPERTS_PER_GROUP
    sub = lax.broadcasted_iota(jnp.int32, (blk, logits.shape[1]), 0)
    far = jnp.int32(blk)
    is_g = sub < N_GROUPS
    gl = jnp.where(is_g, logits[0:blk], NEG_INF)
    g_max = jnp.max(gl, axis=0, keepdims=True)
    g_w = 1.0 / jnp.sum(jnp.where(is_g, jnp.exp(gl - g_max), 0.0), axis=0, keepdims=True)
    g_idx = jnp.min(jnp.where(gl == g_max, sub, far), axis=0, keepdims=True)
    el = logits[blk:2 * blk]
    for g in range(1, N_GROUPS):
        el = jnp.where(g_idx == g, logits[(g + 1) * blk:(g + 2) * blk], el)
    e1 = jnp.max(el, axis=0, keepdims=True)
    i1 = jnp.min(jnp.where(el == e1, sub, far), axis=0, keepdims=True)
    el2 = jnp.where(sub == i1, NEG_INF, el)
    e2 = jnp.max(el2, axis=0, keepdims=True)
    i2 = jnp.min(jnp.where(el2 == e2, sub, far), axis=0, keepdims=True)
    ratio = jnp.exp(e2 - e1)
    w1 = g_w / (1.0 + ratio)
    w2 = g_w * ratio / (1.0 + ratio)
    e_lo, e_hi = jnp.minimum(i1, i2), jnp.maximum(i1, i2)
    cls = (g_idx * PAIRS_PER_GROUP + jnp.right_shift(e_lo * (2 * EXPERTS_PER_GROUP - 1 - e_lo), 1)
           + (e_hi - e_lo - 1))
    first_is_lo = i1 < i2
    w_lo, w_hi = jnp.where(first_is_lo, w1, w2), jnp.where(first_is_lo, w2, w1)
    route_t = (jnp.where(sub == ROUTE_CLS, cls.astype(F32), 0.0) + jnp.where(sub == ROUTE_W_LO, w_lo, 0.0)
               + jnp.where(sub == ROUTE_W_HI, w_hi, 0.0))
    rt_ref[...] = route_t
    d = h2.shape[1]
    h2x_ref[:, 0:d] = h2
    h2x_ref[:, d:d + LANES] = jnp.concatenate(
        [route_t, jnp.zeros((LANES - ROUTE_ROWS, route_t.shape[1]), F32)], axis=0).T


def _merge(oa, ob, ga, gb, x2, mod3, woa, wob, wout, g_ffn, wr, br, seq, tm):
    t, d = x2.shape
    hw = oa.shape[1]
    per_b = seq // tm
    row = lambda i: (i, 0)
    full = lambda i: (0, 0)
    return pl.pallas_call(
        _merge_kernel,
        grid=(t // tm,),
        in_specs=[pl.BlockSpec((tm, hw), row), pl.BlockSpec((tm, hw), row),
                  pl.BlockSpec((tm, d), row), pl.BlockSpec((tm, d), row),
                  pl.BlockSpec((tm, d), row),
                  pl.BlockSpec((1,) + mod3.shape[1:], lambda i: (i // per_b, 0, 0)),
                  pl.BlockSpec(woa.shape, full), pl.BlockSpec(wob.shape, full),
                  pl.BlockSpec(wout.shape, full), pl.BlockSpec(g_ffn.shape, full),
                  pl.BlockSpec(wr.shape, full), pl.BlockSpec(br.shape, full)],
        out_specs=[pl.BlockSpec((tm, d), row), pl.BlockSpec((tm, d + LANES), row),
                   pl.BlockSpec((ROUTE_ROWS, tm), lambda i: (0, i))],
        out_shape=[jax.ShapeDtypeStruct((t, d), F32), jax.ShapeDtypeStruct((t, d + LANES), F32),
                   jax.ShapeDtypeStruct((ROUTE_ROWS, t), F32)],
        compiler_params=_cparams(("parallel",)),
        name="merge",
    )(oa, ob, ga, gb, x2, mod3, woa, wob, wout, g_ffn, wr, br)


def _plan_kernel(rt_ref, pos_ref, cnt_ref, carry_ref, off_ref, *, tc):
    phase, i = pl.program_id(0), pl.program_id(1)
    cls = rt_ref[ROUTE_CLS:ROUTE_CLS + 1, :]
    onehot = lax.broadcasted_iota(jnp.int32, (LANES, tc), 0).astype(F32) == cls
    per_class = jnp.sum(onehot.astype(F32), axis=1, keepdims=True)

    @pl.when((phase == 0) & (i == 0))
    def _():
        cnt_ref[...] = jnp.zeros_like(cnt_ref)

    @pl.when(phase == 0)
    def _():
        cnt_ref[...] += per_class

    @pl.when((phase == 1) & (i == 0))
    def _():
        padded = jnp.floor((cnt_ref[...] + (MOE_ROWS - 1)) * (1.0 / MOE_ROWS)) * MOE_ROWS
        hi = jnp.floor(padded * (1.0 / 256.0))
        digits = [jnp.broadcast_to(v, (LANES, LANES)).astype(BF16) for v in (hi, padded - 256.0 * hi)]
        below = (lax.broadcasted_iota(jnp.int32, (LANES, LANES), 1)
                 < lax.broadcasted_iota(jnp.int32, (LANES, LANES), 0)).astype(BF16)
        off = 256.0 * _dot(below, digits[0]) + _dot(below, digits[1])
        off_ref[...] = off[:, 0:1]
        carry_ref[...] = jnp.zeros_like(carry_ref)

    @pl.when(phase == 1)
    def _():
        earlier = (lax.broadcasted_iota(jnp.int32, (tc, tc), 0)
                   < lax.broadcasted_iota(jnp.int32, (tc, tc), 1)).astype(BF16)
        rank = _dot(onehot.astype(BF16), earlier)
        row = rank + (carry_ref[...] + off_ref[...])
        pos_ref[...] = jnp.sum(jnp.where(onehot, row, 0.0), axis=0, keepdims=True).astype(jnp.int32)
        carry_ref[...] += per_class


def _plan(route_t, tc):
    t = route_t.shape[1]
    return pl.pallas_call(
        functools.partial(_plan_kernel, tc=tc),
        grid=(2, t // tc),
        in_specs=[pl.BlockSpec((ROUTE_ROWS, tc), lambda ph, i: (0, i))],
        out_specs=[pl.BlockSpec((1, tc), lambda ph, i: (0, i * ph)),
                   pl.BlockSpec((LANES, 1), lambda ph, i: (0, 0))],
        out_shape=[jax.ShapeDtypeStruct((1, t), jnp.int32), jax.ShapeDtypeStruct((LANES, 1), F32)],
        scratch_shapes=[pltpu.VMEM((LANES, 1), F32), pltpu.VMEM((LANES, 1), F32)],
        compiler_params=_cparams(("arbitrary", "arbitrary")),
        name="plan",
    )(route_t)


def _tile_tables(counts, n_tiles):
    pairs = [(lo, hi) for lo in range(EXPERTS_PER_GROUP) for hi in range(lo + 1, EXPERTS_PER_GROUP)]
    exp_lo = np.array([g * EXPERTS_PER_GROUP + lo for g in range(N_GROUPS) for lo, _ in pairs], np.int32)
    exp_hi = np.array([g * EXPERTS_PER_GROUP + hi for g in range(N_GROUPS) for _, hi in pairs], np.int32)
    cnt = counts.reshape(-1)[:N_CLASSES].astype(jnp.int32)
    tiles_per_class = (cnt + (MOE_ROWS - 1)) // MOE_ROWS
    tile_end = jnp.cumsum(tiles_per_class)
    n_used = tile_end[-1]
    j = jnp.arange(n_tiles, dtype=jnp.int32)
    tile_cls = jnp.sum(tile_end[None, :] <= jnp.minimum(j, n_used - 1)[:, None], axis=1, dtype=jnp.int32)
    valid_end = (tile_end - tiles_per_class) * MOE_ROWS + cnt
    unit = jnp.arange(n_tiles * (MOE_ROWS // ZERO_ROWS), dtype=jnp.int32)
    zero_fill = ((unit + 1) * ZERO_ROWS > valid_end[tile_cls[unit // (MOE_ROWS // ZERO_ROWS)]]).astype(jnp.int32)
    return (jnp.asarray(exp_lo)[tile_cls], jnp.asarray(exp_hi)[tile_cls], n_used.reshape(1).astype(jnp.int32),
            zero_fill)


def _row_copy(src, src_row, dst, dst_row, n, sem):
    return pltpu.make_async_copy(src.at[pl.ds(src_row, n)], dst.at[pl.ds(dst_row, n)], sem)


def _dispatch_kernel(pos_ref, zf_ref, h_ref, hs_ref, zero_ref, sem, *, td, n_tiles):
    i = pl.program_id(0)

    @pl.when(i == 0)
    def _():
        zero_ref[...] = jnp.zeros_like(zero_ref)

        def fill(j, n):
            @pl.when(zf_ref[j] > 0)
            def _():
                _row_copy(zero_ref, 0, hs_ref, pl.multiple_of(j * ZERO_ROWS, ZERO_ROWS), ZERO_ROWS, sem).start()
            return n + zf_ref[j]

        n_fill = lax.fori_loop(0, n_tiles * (MOE_ROWS // ZERO_ROWS), fill, jnp.int32(0))

        def drain(_, carry):
            _row_copy(zero_ref, 0, hs_ref, 0, ZERO_ROWS, sem).wait()
            return carry

        lax.fori_loop(0, n_fill, drain, 0)

    base = i * td

    def send(r, carry):
        _row_copy(h_ref, r, hs_ref, pos_ref[base + r], 1, sem).start()
        return carry

    lax.fori_loop(0, td, send, 0, unroll=8)
    _row_copy(h_ref, 0, hs_ref, 0, td, sem).wait()


def _dispatch(h2x, pos, zero_fill, n_tiles, td):
    t, w = h2x.shape
    return pl.pallas_call(
        functools.partial(_dispatch_kernel, td=td, n_tiles=n_tiles),
        grid_spec=pltpu.PrefetchScalarGridSpec(
            num_scalar_prefetch=2,
            grid=(t // td,),
            in_specs=[pl.BlockSpec((td, w), lambda i, pos, zf: (i, 0))],
            out_specs=pl.BlockSpec(memory_space=pl.ANY),
            scratch_shapes=[pltpu.VMEM((ZERO_ROWS, w), F32), pltpu.SemaphoreType.DMA(())]),
        out_shape=jax.ShapeDtypeStruct((n_tiles * MOE_ROWS, w), F32),
        compiler_params=_cparams(("arbitrary",)),
        name="dispatch",
    )(pos, zero_fill, h2x)


def _moe_kernel(elo_ref, ehi_ref, nu_ref, hs_ref, wg_lo, wu_lo, wd_lo, wg_hi, wu_hi, wd_hi, y_ref):
    j = pl.program_id(0)
    d = y_ref.shape[1]

    @pl.when(j < nu_ref[0])
    def _():
        h = hs_ref[:, 0:d].astype(BF16)
        ups = [(_dot(h, wg[0]), _dot(h, wu[0])) for wg, wu in ((wg_lo, wu_lo), (wg_hi, wu_hi))]
        hids = [((a * jax.nn.sigmoid(a)) * u * hs_ref[:, d + lane:d + lane + 1]).astype(BF16)
                for (a, u), lane in zip(ups, (ROUTE_W_LO, ROUTE_W_HI))]
        y_ref[...] = _dot(hids[0], wd_lo[0]) + _dot(hids[1], wd_hi[0])

    @pl.when(j >= nu_ref[0])
    def _():
        y_ref[...] = jnp.zeros_like(y_ref)


def _moe(hs, exp_lo, exp_hi, n_used, wg, wu, wd):
    n_tiles = hs.shape[0] // MOE_ROWS
    _, d, ff = wg.shape
    up = lambda sel: pl.BlockSpec((1, d, ff), lambda j, elo, ehi, nu: ((elo, ehi)[sel][j], 0, 0))
    down = lambda sel: pl.BlockSpec((1, ff, d), lambda j, elo, ehi, nu: ((elo, ehi)[sel][j], 0, 0))
    return pl.pallas_call(
        _moe_kernel,
        grid_spec=pltpu.PrefetchScalarGridSpec(
            num_scalar_prefetch=3,
            grid=(n_tiles,),
            in_specs=[pl.BlockSpec((MOE_ROWS, hs.shape[1]),
                                   lambda j, elo, ehi, nu: (jnp.minimum(j, nu[0] - 1), 0)),
                      up(0), up(0), down(0), up(1), up(1), down(1)],
            out_specs=pl.BlockSpec((MOE_ROWS, d), lambda j, elo, ehi, nu: (j, 0))),
        out_shape=jax.ShapeDtypeStruct((n_tiles * MOE_ROWS, d), F32),
        compiler_params=_cparams(("arbitrary",)),
        name="moe",
    )(exp_lo, exp_hi, n_used, hs, wg, wu, wd, wg, wu, wd)


def _final_kernel(pos_ref, x1_ref, mod_ref, gfin_ref, y_ref, o_ref, rows_ref, sem, *, tf):
    i, n = pl.program_id(0), pl.num_programs(0)

    def fetch(step, slot):
        def one(r, carry):
            _row_copy(y_ref, pos_ref[step * tf + r], rows_ref.at[slot], r, 1, sem.at[slot]).start()
            return carry

        lax.fori_loop(0, tf, one, 0, unroll=8)

    @pl.when(i == 0)
    def _():
        fetch(0, 0)

    @pl.when(i + 1 < n)
    def _():
        fetch(i + 1, (i + 1) % 2)

    slot = i % 2
    _row_copy(y_ref, 0, rows_ref.at[slot], 0, tf, sem.at[slot]).wait()
    x2 = x1_ref[...] + mod_ref[0, 5:6, :] * rows_ref[slot]
    o_ref[...] = _rms(x2, gfin_ref[...])


def _final(y, pos, x1, mod3, g_final, seq, tf):
    t, d = x1.shape
    per_b = seq // tf
    return pl.pallas_call(
        functools.partial(_final_kernel, tf=tf),
        grid_spec=pltpu.PrefetchScalarGridSpec(
            num_scalar_prefetch=1,
            grid=(t // tf,),
            in_specs=[pl.BlockSpec((tf, d), lambda i, pos: (i, 0)),
                      pl.BlockSpec((1,) + mod3.shape[1:], lambda i, pos: (i // per_b, 0, 0)),
                      pl.BlockSpec((1, d), lambda i, pos: (0, 0)),
                      pl.BlockSpec(memory_space=pl.ANY)],
            out_specs=pl.BlockSpec((tf, d), lambda i, pos: (i, 0)),
            scratch_shapes=[pltpu.VMEM((2, tf, d), F32), pltpu.SemaphoreType.DMA((2,))]),
        out_shape=jax.ShapeDtypeStruct((t, d), F32),
        compiler_params=_cparams(("arbitrary",)),
        name="final",
    )(pos, x1, mod3, g_final, y)


def _layout_weights(w_in, w_uq, w_uk, w_uv, heads, q_lora, kv_lora, d):
    hw = heads * HEAD_DIM
    cuts = np.cumsum([q_lora, kv_lora, ROPE_DIM, hw, hw, hw, d, d])
    w_qlat, w_kvlat, w_kr, w_qb, w_kb, w_vb, w_ga, w_gb = jnp.split(w_in, [int(v) for v in cuts[:-1]], axis=1)
    swap = np.concatenate([np.arange(ROPE_DIM // 2, ROPE_DIM), np.arange(ROPE_DIM // 2)])
    w_kr_sw = w_kr[:, swap]
    kr_blk = jnp.concatenate([w_kr, w_kr, w_kr_sw, w_kr_sw], axis=1)
    band_scale = HEAD_DIM ** -0.5 * LOG2_E
    wbig = jnp.concatenate([w_qlat, w_kvlat, kr_blk, w_qb * band_scale, w_kb, w_ga, w_gb], axis=1)

    def transposed_values(w, k):
        return jnp.pad(w.reshape(k, heads, HEAD_DIM).transpose(1, 2, 0),
                       ((0, 0), (0, VT_ROWS - HEAD_DIM), (0, 0))).reshape(heads * VT_ROWS, k)

    qk = HEAD_DIM + ROPE_DIM
    wq3 = w_uq.reshape(q_lora, heads, qk) * (qk ** -0.5 * LOG2_E)
    nope = wq3[:, :, :HEAD_DIM].reshape(q_lora, hw)
    rope = wq3[:, :, HEAD_DIM:]
    pad = jnp.zeros((q_lora, heads // 2, LANES - 2 * ROPE_DIM), w_uq.dtype)

    def lay(r):
        return jnp.concatenate([r.reshape(q_lora, heads // 2, 2 * ROPE_DIM), pad], axis=2).reshape(q_lora, hw)

    wq = jnp.concatenate([nope, lay(rope), lay(rope[:, :, swap])], axis=1)
    return (wbig.astype(BF16), wq.astype(BF16), w_uk.astype(BF16), transposed_values(w_uv, kv_lora).astype(BF16),
            transposed_values(w_vb, d).astype(BF16))


def kernel(x, c, positions, w_ada, b_ada, g_mix, w_in, g_q, w_uq, g_kv, w_uk, w_uv, rel_bias, w_oa, w_ob,
           w_out, g_ffn, w_rg, b_rg, w_re, b_re, w_gate, w_up, w_down, g_final):
    bsz, seq, d = x.shape
    depth = w_ada.shape[0]
    t = bsz * seq
    heads = rel_bias.shape[1]
    q_lora, kv_lora = g_q.shape[1], g_kv.shape[1]
    tq = 256
    tq_mla = 512
    nblk = -(-LEFT_CHUNKS * CHUNK // tq) + 1
    assert seq % tq == 0 and seq % tq_mla == 0 and tq % CHUNK == 0 and heads % 2 == 0
    assert ROUTE_ROWS == EXPERTS_PER_GROUP and N_GROUPS <= EXPERTS_PER_GROUP

    x2 = x.reshape(t, d)
    pos2 = positions.reshape(t, 1)
    out = x2
    for l in range(depth):
        mod3 = _ada(c, w_ada[l], b_ada[l]).reshape(bsz, -1, d)
        wbig, wq, wkv, wvt, wvbt = _layout_weights(w_in[l], w_uq[l], w_uk[l], w_uv[l], heads, q_lora, kv_lora, d)
        qa, ka, vt, qb, kb, vtb, ga, gb = _proj(
            x2, pos2, mod3, g_mix[l].reshape(1, d), wbig, g_q[l].reshape(1, -1), wq, g_kv[l].reshape(1, -1), wkv,
            wvt, wvbt, seq, tm=512)
        oa = _mla(qa, ka, vt, bsz, seq, tq_mla)
        ob = _band(qb, kb, vtb, _band_bias(rel_bias[l], tq, nblk), bsz, seq, tq, nblk, tiles_per_step=4)

        pad_g = EXPERTS_PER_GROUP - N_GROUPS
        n_route = (N_GROUPS + 1) * EXPERTS_PER_GROUP
        w_r = jnp.concatenate([w_rg[l], jnp.zeros((d, pad_g), F32), w_re[l],
                               jnp.zeros((d, LANES - n_route), F32)], axis=1).T
        b_r = jnp.concatenate([b_rg[l], jnp.zeros((pad_g,), F32), b_re[l],
                               jnp.zeros((LANES - n_route,), F32)]).reshape(LANES, 1)
        w_r_hi = w_r.astype(BF16)
        w_r_lo = (w_r - w_r_hi.astype(F32)).astype(BF16)
        x1, h2x, route_t = _merge(oa, ob, ga, gb, x2, mod3, w_oa[l].astype(BF16), w_ob[l].astype(BF16),
                                  w_out[l].astype(BF16), g_ffn[l].reshape(1, d),
                                  jnp.concatenate([w_r_hi, w_r_lo], axis=0), b_r, seq, tm=512)
        n_tiles = t // MOE_ROWS + N_CLASSES
        pos, counts = _plan(route_t, tc=1024)
        pos = pos.reshape(t)
        exp_lo, exp_hi, n_used, zero_fill = _tile_tables(counts, n_tiles)
        hs = _dispatch(h2x, pos, zero_fill, n_tiles, td=2048)
        y = _moe(hs, exp_lo, exp_hi, n_used, w_gate[l].astype(BF16), w_up[l].astype(BF16), w_down[l].astype(BF16))
        assert l == depth - 1, "multi-layer stacks need an un-normalised residual output"
        out = _final(y, pos, x1, mod3, g_final.reshape(1, d), seq, tf=1024)
    return out.reshape(bsz, seq, d)
```

```python
import functools

import numpy as np
import jax
import jax.numpy as jnp
from jax import lax
from jax.experimental import pallas as pl
from jax.experimental.pallas import tpu as pltpu

F32 = jnp.float32
BF16 = jnp.bfloat16

NORM_EPS = 1e-6
NEG_INF = -1e30
ROPE_BASE = 10000.0
CHUNK = 64
LEFT_CHUNKS = 8
MAX_REL = 256
N_GROUPS = 4
EXPERTS_PER_GROUP = 8
PAIRS_PER_GROUP = EXPERTS_PER_GROUP * (EXPERTS_PER_GROUP - 1) // 2
N_CLASSES = N_GROUPS * PAIRS_PER_GROUP
MOE_ROWS = 256
ZERO_ROWS = 64
ROUTE_CLS, ROUTE_W_LO, ROUTE_W_HI = 0, 1, 2
ROUTE_ROWS = 8
LANES = 128
HEAD_DIM = 64
ROPE_DIM = 32
VT_ROWS = HEAD_DIM + 16
LOG2_E = 1.4426950408889634

VMEM_LIMIT = 56 * 1024 * 1024


def _cparams(sem):
    return pltpu.CompilerParams(dimension_semantics=sem, vmem_limit_bytes=VMEM_LIMIT)


def _dot(a, b):
    return jnp.dot(a, b, preferred_element_type=F32)


def _dot_nt(a, b):
    return lax.dot_general(a, b, (((1,), (1,)), ((), ())), preferred_element_type=F32)


def _rms(x, g):
    return x * lax.rsqrt(jnp.mean(x * x, axis=-1, keepdims=True) + NORM_EPS) * g


def _ada_kernel(c_ref, w_ref, b_ref, o_ref):
    c = c_ref[...]
    o_ref[...] = _dot(c * jax.nn.sigmoid(c), w_ref[...]) + b_ref[...]


def _ada(c, w_ada, b_ada):
    bsz, d = c.shape
    n = w_ada.shape[1]
    return pl.pallas_call(
        _ada_kernel,
        grid=(n // d,),
        in_specs=[pl.BlockSpec((bsz, d), lambda j: (0, 0)),
                  pl.BlockSpec((d, d), lambda j: (0, j)),
                  pl.BlockSpec((1, d), lambda j: (0, j))],
        out_specs=pl.BlockSpec((bsz, d), lambda j: (0, j)),
        out_shape=jax.ShapeDtypeStruct((bsz, n), F32),
        compiler_params=_cparams(("arbitrary",)),
        name="ada",
    )(c, w_ada, b_ada.reshape(1, n))


def _proj_kernel(x_ref, pos_ref, mod_ref, gmix_ref, wbig_ref, gq_ref, wq_ref, gkv_ref, wkv_ref,
                 wvt_ref, wvbt_ref, one_ref, inv_ref, sgn_ref,
                 qa_ref, ka_ref, vt_ref, qb_ref, kb_ref, vtb_ref, ga_ref, gb_ref, *, q_lora, kv_lora):
    x = x_ref[...]
    h = _rms(x, gmix_ref[...]) * (1.0 + mod_ref[0, 1:2, :]) + mod_ref[0, 0:1, :]
    hb = h.astype(BF16)

    c0 = q_lora + kv_lora + LANES
    head = _dot(hb, wbig_ref[:, 0:c0])
    q_lat = head[:, 0:q_lora]
    kv_lat = head[:, q_lora:q_lora + kv_lora]
    kr_blk = head[:, q_lora + kv_lora:c0]
    hw = qb_ref.shape[1]
    d = x.shape[1]
    g0 = c0 + 2 * hw
    ga_ref[...] = jax.nn.sigmoid(_dot(hb, wbig_ref[:, g0:g0 + d])).astype(BF16)

    ang = pos_ref[...].astype(F32) * inv_ref[...]
    live = lax.broadcasted_iota(jnp.int32, (1, LANES), 1) < 2 * ROPE_DIM
    cos_t = jnp.where(live, jnp.cos(ang), 0.0)
    sin_t = jnp.where(live, jnp.sin(ang) * sgn_ref[...], 0.0)

    qn = _rms(q_lat, gq_ref[...]).astype(BF16)
    q_all = _dot(qn, wq_ref[...])
    kvn = _rms(kv_lat, gkv_ref[...]).astype(BF16)
    k_nope = _dot(kvn, wkv_ref[...])
    vt_ref[0] = (_dot_nt(wvt_ref[...], kvn) + one_ref[...]).astype(BF16)
    gb_ref[...] = jax.nn.sigmoid(_dot(hb, wbig_ref[:, g0 + d:g0 + 2 * d])).astype(BF16)
    k_rot = (kr_blk * cos_t + pltpu.roll(kr_blk, LANES // 2, 1) * sin_t).astype(BF16)
    for p in range(hw // LANES):
        lo, hi = p * LANES, (p + 1) * LANES
        qa_ref[:, 2 * lo:2 * lo + LANES] = q_all[:, lo:hi].astype(BF16)
        qa_ref[:, 2 * lo + LANES:2 * hi] = (
            q_all[:, hw + lo:hw + hi] * cos_t + q_all[:, 2 * hw + lo:2 * hw + hi] * sin_t).astype(BF16)
        ka_ref[:, 2 * lo:2 * lo + LANES] = k_nope[:, lo:hi].astype(BF16)
        ka_ref[:, 2 * lo + LANES:2 * hi] = k_rot

    qb_ref[...] = _dot(hb, wbig_ref[:, c0:c0 + hw]).astype(BF16)
    kb_ref[...] = _dot(hb, wbig_ref[:, c0 + hw:c0 + 2 * hw]).astype(BF16)
    vtb_ref[0] = (_dot_nt(wvbt_ref[...], hb) + one_ref[...]).astype(BF16)


def _proj(x2, pos2, mod3, g_mix, wbig, g_q, wq, g_kv, wkv, wvt, wvbt, seq, tm):
    t, d = x2.shape
    q_lora, kv_lora = g_q.shape[1], g_kv.shape[1]
    hw = wkv.shape[1]
    per_b = seq // tm
    row = lambda i: (i, 0)
    full = lambda i: (0, 0)
    inv = ROPE_BASE ** (-(np.arange(LANES) % (ROPE_DIM // 2)).astype(np.float32) / (ROPE_DIM // 2))
    sgn = np.where((np.arange(LANES) % ROPE_DIM) < ROPE_DIM // 2, -1.0, 1.0).astype(np.float32)
    ones_col = (np.arange(wvt.shape[0]) % VT_ROWS == HEAD_DIM).astype(np.float32).reshape(-1, 1)
    v_t = jax.ShapeDtypeStruct((t // seq, wvt.shape[0], seq), BF16)
    outs = [jax.ShapeDtypeStruct((t, 2 * hw), BF16), jax.ShapeDtypeStruct((t, 2 * hw), BF16), v_t,
            jax.ShapeDtypeStruct((t, hw), BF16), jax.ShapeDtypeStruct((t, hw), BF16), v_t,
            jax.ShapeDtypeStruct((t, d), BF16), jax.ShapeDtypeStruct((t, d), BF16)]
    v_t_spec = pl.BlockSpec((1, wvt.shape[0], tm), lambda i: (i // per_b, 0, i % per_b))
    out_specs = [v_t_spec if o is v_t else pl.BlockSpec((tm, o.shape[1]), row) for o in outs]
    return pl.pallas_call(
        functools.partial(_proj_kernel, q_lora=q_lora, kv_lora=kv_lora),
        grid=(t // tm,),
        in_specs=[pl.BlockSpec((tm, d), row),
                  pl.BlockSpec((tm, 1), row),
                  pl.BlockSpec((1,) + mod3.shape[1:], lambda i: (i // per_b, 0, 0)),
                  pl.BlockSpec(g_mix.shape, full),
                  pl.BlockSpec(wbig.shape, full),
                  pl.BlockSpec(g_q.shape, full),
                  pl.BlockSpec(wq.shape, full),
                  pl.BlockSpec(g_kv.shape, full),
                  pl.BlockSpec(wkv.shape, full),
                  pl.BlockSpec(wvt.shape, full),
                  pl.BlockSpec(wvbt.shape, full),
                  pl.BlockSpec(ones_col.shape, full),
                  pl.BlockSpec((1, LANES), full),
                  pl.BlockSpec((1, LANES), full)],
        out_specs=out_specs,
        out_shape=outs,
        compiler_params=_cparams(("parallel",)),
        name="proj",
    )(x2, pos2, mod3, g_mix, wbig, g_q, wq, g_kv, wkv, wvt, wvbt, jnp.asarray(ones_col),
      jnp.asarray(inv).reshape(1, LANES), jnp.asarray(sgn).reshape(1, LANES))


def _mla_kernel(q_ref, k_ref, vt_ref, o_ref, sa, sb, *, tq):
    i, nq = pl.program_id(2), pl.num_programs(2)
    lane = lax.broadcasted_iota(jnp.int32, (1, 2 * LANES), 1)
    first = (lane < HEAD_DIM) | ((lane >= LANES) & (lane < LANES + ROPE_DIM))
    second = ((lane >= HEAD_DIM) & (lane < LANES)) | ((lane >= LANES + ROPE_DIM) & (lane < LANES + 2 * ROPE_DIM))

    def head_queries(tile):
        q = q_ref[pl.ds(pl.multiple_of(tile * tq, tq), tq), :]
        zero = jnp.zeros_like(q)
        return (jnp.where(first, q, zero), jnp.where(second, q, zero))

    qs = head_queries(i)

    def scores(queries, blk, buf):
        k = k_ref[pl.ds(pl.multiple_of(blk * tq, tq), tq), :]
        for h, qh in enumerate(queries):
            buf[h] = _dot_nt(k, qh)

    def consume(blk, buf, state, mask):
        start = pl.multiple_of(blk * tq, tq)
        ps, ms, alphas = [], [], []
        for h, (m, _) in enumerate(state):
            st = buf[h]
            if mask is not None:
                st = jnp.where(mask, st, NEG_INF)
            m_new = jnp.maximum(m, jnp.max(st, axis=0, keepdims=True))
            ms.append(m_new)
            alphas.append(jnp.exp2(m - m_new))
            ps.append(jnp.exp2(st - m_new).astype(BF16))
        new = []
        for h, (p, m_new, alpha, (_, acc)) in enumerate(zip(ps, ms, alphas, state)):
            vt = vt_ref[0, h * VT_ROWS:(h + 1) * VT_ROWS, pl.ds(start, tq)]
            new.append((m_new, alpha * acc + _dot(vt, p)))
        return tuple(new)

    causal = (lax.broadcasted_iota(jnp.int32, (tq, tq), 0) // CHUNK
              <= lax.broadcasted_iota(jnp.int32, (tq, tq), 1) // CHUNK)
    following = jnp.minimum(i + 1, nq - 1)

    def tile(sx, sy):
        @pl.when(i == 0)
        def _():
            scores(qs, 0, sx)

        def two_blocks(jj, state):
            scores(qs, 2 * jj + 1, sy)
            state = consume(2 * jj, sx, state, None)
            scores(qs, 2 * jj + 2, sx)
            return consume(2 * jj + 1, sy, state, None)

        def even_tail(state):
            scores(head_queries(following), 0, sy)
            return consume(i, sx, state, causal)

        def odd_tail(state):
            scores(qs, i, sy)
            state = consume(i - 1, sx, state, None)
            scores(head_queries(following), 0, sx)
            return consume(i, sy, state, causal)

        init = tuple((jnp.full((1, tq), NEG_INF, F32), jnp.zeros((VT_ROWS, tq), F32)) for _ in range(2))
        state = lax.fori_loop(0, i // 2, two_blocks, init)
        (_, a0), (_, a1) = lax.cond(i % 2 == 1, odd_tail, even_tail, state)
        out_t = jnp.concatenate([a[0:HEAD_DIM] * (1.0 / a[HEAD_DIM:HEAD_DIM + 1]) for a in (a0, a1)], axis=0)
        o_ref[...] = out_t.T.astype(o_ref.dtype)

    swapped = ((i + 1) // 2) % 2

    @pl.when(swapped == 0)
    def _():
        tile(sa, sb)

    @pl.when(swapped == 1)
    def _():
        tile(sb, sa)


def _mla(qa, ka, vt, bsz, seq, tq):
    t = qa.shape[0]
    pairs = qa.shape[1] // (2 * LANES)
    nq = seq // tq
    return pl.pallas_call(
        functools.partial(_mla_kernel, tq=tq),
        grid=(bsz, pairs, nq),
        in_specs=[pl.BlockSpec((seq, 2 * LANES), lambda b, p, i: (b, p)),
                  pl.BlockSpec((seq, 2 * LANES), lambda b, p, i: (b, p)),
                  pl.BlockSpec((1, 2 * VT_ROWS, seq), lambda b, p, i: (b, p, 0))],
        out_specs=pl.BlockSpec((tq, LANES), lambda b, p, i: (b * nq + i, p)),
        out_shape=jax.ShapeDtypeStruct((t, pairs * LANES), BF16),
        scratch_shapes=[pltpu.VMEM((2, tq, tq), F32)] * 2,
        compiler_params=_cparams(("parallel", "parallel", "arbitrary")),
        name="mla",
    )(qa, ka, vt)


def _band_kernel(q_ref, k_ref, vt_ref, bias_ref, o_ref, *, tq, nblk):
    lane = lax.broadcasted_iota(jnp.int32, (1, LANES), 1)

    def scores(n):
        tile = first + n
        q = q_ref[n * tq:(n + 1) * tq, :]
        zero = jnp.zeros_like(q)
        qs = (jnp.where(lane < HEAD_DIM, q, zero), jnp.where(lane >= HEAD_DIM, q, zero))
        starts, valid = [], []
        for j in range(nblk):
            blk = tile - (nblk - 1) + j
            starts.append(pl.multiple_of(jnp.maximum(blk, 0) * tq, tq))
            valid.append(blk >= 0)
        ks = [k_ref[pl.ds(s, tq), :] for s in starts]
        return starts, [[jnp.where(valid[j], _dot_nt(ks[j], qh) + bias_ref[0, h, j * tq:(j + 1) * tq, :], NEG_INF)
                         for j in range(nblk)] for h, qh in enumerate(qs)]

    def finish(starts, sts):
        outs = []
        for h in range(2):
            m = functools.reduce(jnp.maximum, [jnp.max(st, axis=0, keepdims=True) for st in sts[h]])
            acc = functools.reduce(jnp.add, [
                _dot(vt_ref[0, h * VT_ROWS:(h + 1) * VT_ROWS, pl.ds(starts[j], tq)], jnp.exp2(st - m).astype(BF16))
                for j, st in enumerate(sts[h])])
            outs.append(acc[0:HEAD_DIM] * (1.0 / acc[HEAD_DIM:HEAD_DIM + 1]))
        return jnp.concatenate(outs, axis=0).T

    tiles_per_step = q_ref.shape[0] // tq
    first = pl.program_id(2) * tiles_per_step
    pending = scores(0)
    for n in range(tiles_per_step):
        upcoming = scores(n + 1) if n + 1 < tiles_per_step else None
        o_ref[n * tq:(n + 1) * tq, :] = finish(*pending).astype(o_ref.dtype)
        pending = upcoming


def _band_bias(rel_table, tq, nblk):
    nk = nblk * tq
    ring = nk + tq
    d = np.arange(ring)
    key_minus_query = np.where(d < nk, d, d - ring)
    idx = np.clip((nblk - 1) * tq - key_minus_query, -MAX_REL, MAX_REL) + MAX_REL
    per_offset = rel_table.astype(F32)[:, idx]
    heads = rel_table.shape[0]
    toeplitz = jnp.tile(per_offset, (1, tq))[:, :tq * (ring - 1)].reshape(heads, tq, ring - 1)[:, :, :nk]
    qc = np.arange(tq)[:, None] // CHUNK
    kc = np.arange(nk)[None, :] // CHUNK - ((nblk - 1) * tq // CHUNK - LEFT_CHUNKS)
    band = (kc >= qc) & (kc <= qc + LEFT_CHUNKS)
    bias = jnp.where(band[None], toeplitz * LOG2_E, NEG_INF)
    return bias.reshape(heads // 2, 2, tq, nk).swapaxes(2, 3)


def _band(qb, kb, vtb, bias, bsz, seq, tq, nblk, tiles_per_step):
    t, hw = qb.shape
    pairs = hw // LANES
    rows = tq * tiles_per_step
    nq = seq // rows
    return pl.pallas_call(
        functools.partial(_band_kernel, tq=tq, nblk=nblk),
        grid=(bsz, pairs, nq),
        in_specs=[pl.BlockSpec((rows, LANES), lambda b, p, i: (b * nq + i, p)),
                  pl.BlockSpec((seq, LANES), lambda b, p, i: (b, p)),
                  pl.BlockSpec((1, 2 * VT_ROWS, seq), lambda b, p, i: (b, p, 0)),
                  pl.BlockSpec((1,) + bias.shape[1:], lambda b, p, i: (p, 0, 0, 0))],
        out_specs=pl.BlockSpec((rows, LANES), lambda b, p, i: (b * nq + i, p)),
        out_shape=jax.ShapeDtypeStruct((t, hw), BF16),
        compiler_params=_cparams(("parallel", "parallel", "arbitrary")),
        name="band",
    )(qb, kb, vtb, bias)


def _merge_kernel(oa_ref, ob_ref, ga_ref, gb_ref, x_ref, mod_ref, woa_ref, wob_ref, wout_ref, gffn_ref,
                  wr_ref, br_ref, x1_ref, h2x_ref, rt_ref):
    merged = (ga_ref[...].astype(F32) * _dot(oa_ref[...], woa_ref[...])
              + gb_ref[...].astype(F32) * _dot(ob_ref[...], wob_ref[...]))
    mix = _dot(merged.astype(BF16), wout_ref[...])
    x1 = x_ref[...] + mod_ref[0, 2:3, :] * mix
    x1_ref[...] = x1
    h2 = _rms(x1, gffn_ref[...]) * (1.0 + mod_ref[0, 4:5, :]) + mod_ref[0, 3:4, :]

    h_hi = h2.astype(BF16)
    h_lo = (h2 - h_hi.astype(F32)).astype(BF16)
    r_hi = _dot_nt(wr_ref[...], h_hi)
    logits = (r_hi[0:LANES] + _dot_nt(wr_ref[0:LANES, :], h_lo) + r_hi[LANES:2 * LANES]) + br_ref[...]
    blk = EXPERTS_PER_GROUP
    sub = lax.broadcasted_iota(jnp.int32, (blk, logits.shape[1]), 0)
    far = jnp.int32(blk)
    is_g = sub < N_GROUPS
    gl = jnp.where(is_g, logits[0:blk], NEG_INF)
    g_max = jnp.max(gl, axis=0, keepdims=True)
    g_w = 1.0 / jnp.sum(jnp.where(is_g, jnp.exp(gl - g_max), 0.0), axis=0, keepdims=True)
    g_idx = jnp.min(jnp.where(gl == g_max, sub, far), axis=0, keepdims=True)
    el = logits[blk:2 * blk]
    for g in range(1, N_GROUPS):
        el = jnp.where(g_idx == g, logits[(g + 1) * blk:(g + 2) * blk], el)
    e1 = jnp.max(el, axis=0, keepdims=True)
    i1 = jnp.min(jnp.where(el == e1, sub, far), axis=0, keepdims=True)
    el2 = jnp.where(sub == i1, NEG_INF, el)
    e2 = jnp.max(el2, axis=0, keepdims=True)
    i2 = jnp.min(jnp.where(el2 == e2, sub, far), axis=0, keepdims=True)
    ratio = jnp.exp(e2 - e1)
    w1 = g_w / (1.0 + ratio)
    w2 = g_w * ratio / (1.0 + ratio)
    e_lo, e_hi = jnp.minimum(i1, i2), jnp.maximum(i1, i2)
    cls = (g_idx * PAIRS_PER_GROUP + jnp.right_shift(e_lo * (2 * EXPERTS_PER_GROUP - 1 - e_lo), 1)
           + (e_hi - e_lo - 1))
    first_is_lo = i1 < i2
    w_lo, w_hi = jnp.where(first_is_lo, w1, w2), jnp.where(first_is_lo, w2, w1)
    route_t = (jnp.where(sub == ROUTE_CLS, cls.astype(F32), 0.0) + jnp.where(sub == ROUTE_W_LO, w_lo, 0.0)
               + jnp.where(sub == ROUTE_W_HI, w_hi, 0.0))
    rt_ref[...] = route_t
    d = h2.shape[1]
    h2x_ref[:, 0:d] = h2
    h2x_ref[:, d:d + LANES] = jnp.concatenate(
        [route_t, jnp.zeros((LANES - ROUTE_ROWS, route_t.shape[1]), F32)], axis=0).T


def _merge(oa, ob, ga, gb, x2, mod3, woa, wob, wout, g_ffn, wr, br, seq, tm):
    t, d = x2.shape
    hw = oa.shape[1]
    per_b = seq // tm
    row = lambda i: (i, 0)
    full = lambda i: (0, 0)
    return pl.pallas_call(
        _merge_kernel,
        grid=(t // tm,),
        in_specs=[pl.BlockSpec((tm, hw), row), pl.BlockSpec((tm, hw), row),
                  pl.BlockSpec((tm, d), row), pl.BlockSpec((tm, d), row),
                  pl.BlockSpec((tm, d), row),
                  pl.BlockSpec((1,) + mod3.shape[1:], lambda i: (i // per_b, 0, 0)),
                  pl.BlockSpec(woa.shape, full), pl.BlockSpec(wob.shape, full),
                  pl.BlockSpec(wout.shape, full), pl.BlockSpec(g_ffn.shape, full),
                  pl.BlockSpec(wr.shape, full), pl.BlockSpec(br.shape, full)],
        out_specs=[pl.BlockSpec((tm, d), row), pl.BlockSpec((tm, d + LANES), row),
                   pl.BlockSpec((ROUTE_ROWS, tm), lambda i: (0, i))],
        out_shape=[jax.ShapeDtypeStruct((t, d), F32), jax.ShapeDtypeStruct((t, d + LANES), F32),
                   jax.ShapeDtypeStruct((ROUTE_ROWS, t), F32)],
        compiler_params=_cparams(("parallel",)),
        name="merge",
    )(oa, ob, ga, gb, x2, mod3, woa, wob, wout, g_ffn, wr, br)


def _plan_kernel(rt_ref, pos_ref, cnt_ref, carry_ref, off_ref, *, tc):
    phase, i = pl.program_id(0), pl.program_id(1)
    cls = rt_ref[ROUTE_CLS:ROUTE_CLS + 1, :]
    onehot = lax.broadcasted_iota(jnp.int32, (LANES, tc), 0).astype(F32) == cls
    per_class = jnp.sum(onehot.astype(F32), axis=1, keepdims=True)

    @pl.when((phase == 0) & (i == 0))
    def _():
        cnt_ref[...] = jnp.zeros_like(cnt_ref)

    @pl.when(phase == 0)
    def _():
        cnt_ref[...] += per_class

    @pl.when((phase == 1) & (i == 0))
    def _():
        padded = jnp.floor((cnt_ref[...] + (MOE_ROWS - 1)) * (1.0 / MOE_ROWS)) * MOE_ROWS
        hi = jnp.floor(padded * (1.0 / 256.0))
        digits = [jnp.broadcast_to(v, (LANES, LANES)).astype(BF16) for v in (hi, padded - 256.0 * hi)]
        below = (lax.broadcasted_iota(jnp.int32, (LANES, LANES), 1)
                 < lax.broadcasted_iota(jnp.int32, (LANES, LANES), 0)).astype(BF16)
        off = 256.0 * _dot(below, digits[0]) + _dot(below, digits[1])
        off_ref[...] = off[:, 0:1]
        carry_ref[...] = jnp.zeros_like(carry_ref)

    @pl.when(phase == 1)
    def _():
        earlier = (lax.broadcasted_iota(jnp.int32, (tc, tc), 0)
                   < lax.broadcasted_iota(jnp.int32, (tc, tc), 1)).astype(BF16)
        rank = _dot(onehot.astype(BF16), earlier)
        row = rank + (carry_ref[...] + off_ref[...])
        pos_ref[...] = jnp.sum(jnp.where(onehot, row, 0.0), axis=0, keepdims=True).astype(jnp.int32)
        carry_ref[...] += per_class


def _plan(route_t, tc):
    t = route_t.shape[1]
    return pl.pallas_call(
        functools.partial(_plan_kernel, tc=tc),
        grid=(2, t // tc),
        in_specs=[pl.BlockSpec((ROUTE_ROWS, tc), lambda ph, i: (0, i))],
        out_specs=[pl.BlockSpec((1, tc), lambda ph, i: (0, i * ph)),
                   pl.BlockSpec((LANES, 1), lambda ph, i: (0, 0))],
        out_shape=[jax.ShapeDtypeStruct((1, t), jnp.int32), jax.ShapeDtypeStruct((LANES, 1), F32)],
        scratch_shapes=[pltpu.VMEM((LANES, 1), F32), pltpu.VMEM((LANES, 1), F32)],
        compiler_params=_cparams(("arbitrary", "arbitrary")),
        name="plan",
    )(route_t)


def _tile_tables(counts, n_tiles):
    pairs = [(lo, hi) for lo in range(EXPERTS_PER_GROUP) for hi in range(lo + 1, EXPERTS_PER_GROUP)]
    exp_lo = np.array([g * EXPERTS_PER_GROUP + lo for g in range(N_GROUPS) for lo, _ in pairs], np.int32)
    exp_hi = np.array([g * EXPERTS_PER_GROUP + hi for g in range(N_GROUPS) for _, hi in pairs], np.int32)
    cnt = counts.reshape(-1)[:N_CLASSES].astype(jnp.int32)
    tiles_per_class = (cnt + (MOE_ROWS - 1)) // MOE_ROWS
    tile_end = jnp.cumsum(tiles_per_class)
    n_used = tile_end[-1]
    j = jnp.arange(n_tiles, dtype=jnp.int32)
    tile_cls = jnp.sum(tile_end[None, :] <= jnp.minimum(j, n_used - 1)[:, None], axis=1, dtype=jnp.int32)
    valid_end = (tile_end - tiles_per_class) * MOE_ROWS + cnt
    unit = jnp.arange(n_tiles * (MOE_ROWS // ZERO_ROWS), dtype=jnp.int32)
    zero_fill = ((unit + 1) * ZERO_ROWS > valid_end[tile_cls[unit // (MOE_ROWS // ZERO_ROWS)]]).astype(jnp.int32)
    return (jnp.asarray(exp_lo)[tile_cls], jnp.asarray(exp_hi)[tile_cls], n_used.reshape(1).astype(jnp.int32),
            zero_fill)


def _row_copy(src, src_row, dst, dst_row, n, sem):
    return pltpu.make_async_copy(src.at[pl.ds(src_row, n)], dst.at[pl.ds(dst_row, n)], sem)


def _dispatch_kernel(pos_ref, zf_ref, h_ref, hs_ref, zero_ref, sem, *, td, n_tiles):
    i = pl.program_id(0)

    @pl.when(i == 0)
    def _():
        zero_ref[...] = jnp.zeros_like(zero_ref)

        def fill(j, n):
            @pl.when(zf_ref[j] > 0)
            def _():
                _row_copy(zero_ref, 0, hs_ref, pl.multiple_of(j * ZERO_ROWS, ZERO_ROWS), ZERO_ROWS, sem).start()
            return n + zf_ref[j]

        n_fill = lax.fori_loop(0, n_tiles * (MOE_ROWS // ZERO_ROWS), fill, jnp.int32(0))

        def drain(_, carry):
            _row_copy(zero_ref, 0, hs_ref, 0, ZERO_ROWS, sem).wait()
            return carry

        lax.fori_loop(0, n_fill, drain, 0)

    base = i * td

    for r in range(td):
        _row_copy(h_ref, r, hs_ref, pos_ref[base + r], 1, sem).start()
    _row_copy(h_ref, 0, hs_ref, 0, td, sem).wait()


def _dispatch(h2x, pos, zero_fill, n_tiles, td):
    t, w = h2x.shape
    return pl.pallas_call(
        functools.partial(_dispatch_kernel, td=td, n_tiles=n_tiles),
        grid_spec=pltpu.PrefetchScalarGridSpec(
            num_scalar_prefetch=2,
            grid=(t // td,),
            in_specs=[pl.BlockSpec((td, w), lambda i, pos, zf: (i, 0))],
            out_specs=pl.BlockSpec(memory_space=pl.ANY),
            scratch_shapes=[pltpu.VMEM((ZERO_ROWS, w), F32), pltpu.SemaphoreType.DMA(())]),
        out_shape=jax.ShapeDtypeStruct((n_tiles * MOE_ROWS, w), F32),
        compiler_params=_cparams(("arbitrary",)),
        name="dispatch",
    )(pos, zero_fill, h2x)


def _moe_kernel(elo_ref, ehi_ref, nu_ref, hs_ref, wg_lo, wu_lo, wd_lo, wg_hi, wu_hi, wd_hi, y_ref):
    j = pl.program_id(0)
    d = y_ref.shape[1]

    @pl.when(j < nu_ref[0])
    def _():
        h = hs_ref[:, 0:d].astype(BF16)
        ups = [(_dot(h, wg[0]), _dot(h, wu[0])) for wg, wu in ((wg_lo, wu_lo), (wg_hi, wu_hi))]
        hids = [((a * jax.nn.sigmoid(a)) * u * hs_ref[:, d + lane:d + lane + 1]).astype(BF16)
                for (a, u), lane in zip(ups, (ROUTE_W_LO, ROUTE_W_HI))]
        y_ref[...] = _dot(hids[0], wd_lo[0]) + _dot(hids[1], wd_hi[0])

    @pl.when(j >= nu_ref[0])
    def _():
        y_ref[...] = jnp.zeros_like(y_ref)


def _moe(hs, exp_lo, exp_hi, n_used, wg, wu, wd):
    n_tiles = hs.shape[0] // MOE_ROWS
    _, d, ff = wg.shape
    up = lambda sel: pl.BlockSpec((1, d, ff), lambda j, elo, ehi, nu: ((elo, ehi)[sel][j], 0, 0))
    down = lambda sel: pl.BlockSpec((1, ff, d), lambda j, elo, ehi, nu: ((elo, ehi)[sel][j], 0, 0))
    return pl.pallas_call(
        _moe_kernel,
        grid_spec=pltpu.PrefetchScalarGridSpec(
            num_scalar_prefetch=3,
            grid=(n_tiles,),
            in_specs=[pl.BlockSpec((MOE_ROWS, hs.shape[1]),
                                   lambda j, elo, ehi, nu: (jnp.minimum(j, nu[0] - 1), 0)),
                      up(0), up(0), down(0), up(1), up(1), down(1)],
            out_specs=pl.BlockSpec((MOE_ROWS, d), lambda j, elo, ehi, nu: (j, 0))),
        out_shape=jax.ShapeDtypeStruct((n_tiles * MOE_ROWS, d), F32),
        compiler_params=_cparams(("arbitrary",)),
        name="moe",
    )(exp_lo, exp_hi, n_used, hs, wg, wu, wd, wg, wu, wd)


def _final_kernel(pos_ref, x1_ref, mod_ref, gfin_ref, y_ref, o_ref, rows_ref, sem, *, tf):
    i, n = pl.program_id(0), pl.num_programs(0)

    def fetch(step, slot):
        for r in range(tf):
            _row_copy(y_ref, pos_ref[step * tf + r], rows_ref.at[slot], r, 1, sem.at[slot]).start()

    @pl.when(i == 0)
    def _():
        fetch(0, 0)

    @pl.when(i + 1 < n)
    def _():
        fetch(i + 1, (i + 1) % 2)

    slot = i % 2
    _row_copy(y_ref, 0, rows_ref.at[slot], 0, tf, sem.at[slot]).wait()
    x2 = x1_ref[...] + mod_ref[0, 5:6, :] * rows_ref[slot]
    o_ref[...] = _rms(x2, gfin_ref[...])


def _final(y, pos, x1, mod3, g_final, seq, tf):
    t, d = x1.shape
    per_b = seq // tf
    return pl.pallas_call(
        functools.partial(_final_kernel, tf=tf),
        grid_spec=pltpu.PrefetchScalarGridSpec(
            num_scalar_prefetch=1,
            grid=(t // tf,),
            in_specs=[pl.BlockSpec((tf, d), lambda i, pos: (i, 0)),
                      pl.BlockSpec((1,) + mod3.shape[1:], lambda i, pos: (i // per_b, 0, 0)),
                      pl.BlockSpec((1, d), lambda i, pos: (0, 0)),
                      pl.BlockSpec(memory_space=pl.ANY)],
            out_specs=pl.BlockSpec((tf, d), lambda i, pos: (i, 0)),
            scratch_shapes=[pltpu.VMEM((2, tf, d), F32), pltpu.SemaphoreType.DMA((2,))]),
        out_shape=jax.ShapeDtypeStruct((t, d), F32),
        compiler_params=_cparams(("arbitrary",)),
        name="final",
    )(pos, x1, mod3, g_final, y)


def _layout_weights(w_in, w_uq, w_uk, w_uv, heads, q_lora, kv_lora, d):
    hw = heads * HEAD_DIM
    cuts = np.cumsum([q_lora, kv_lora, ROPE_DIM, hw, hw, hw, d, d])
    w_qlat, w_kvlat, w_kr, w_qb, w_kb, w_vb, w_ga, w_gb = jnp.split(w_in, [int(v) for v in cuts[:-1]], axis=1)
    swap = np.concatenate([np.arange(ROPE_DIM // 2, ROPE_DIM), np.arange(ROPE_DIM // 2)])
    w_kr_sw = w_kr[:, swap]
    kr_blk = jnp.concatenate([w_kr, w_kr, w_kr_sw, w_kr_sw], axis=1)
    band_scale = HEAD_DIM ** -0.5 * LOG2_E
    wbig = jnp.concatenate([w_qlat, w_kvlat, kr_blk, w_qb * band_scale, w_kb, w_ga, w_gb], axis=1)

    def transposed_values(w, k):
        return jnp.pad(w.reshape(k, heads, HEAD_DIM).transpose(1, 2, 0),
                       ((0, 0), (0, VT_ROWS - HEAD_DIM), (0, 0))).reshape(heads * VT_ROWS, k)

    qk = HEAD_DIM + ROPE_DIM
    wq3 = w_uq.reshape(q_lora, heads, qk) * (qk ** -0.5 * LOG2_E)
    nope = wq3[:, :, :HEAD_DIM].reshape(q_lora, hw)
    rope = wq3[:, :, HEAD_DIM:]
    pad = jnp.zeros((q_lora, heads // 2, LANES - 2 * ROPE_DIM), w_uq.dtype)

    def lay(r):
        return jnp.concatenate([r.reshape(q_lora, heads // 2, 2 * ROPE_DIM), pad], axis=2).reshape(q_lora, hw)

    wq = jnp.concatenate([nope, lay(rope), lay(rope[:, :, swap])], axis=1)
    return (wbig.astype(BF16), wq.astype(BF16), w_uk.astype(BF16), transposed_values(w_uv, kv_lora).astype(BF16),
            transposed_values(w_vb, d).astype(BF16))


def kernel(x, c, positions, w_ada, b_ada, g_mix, w_in, g_q, w_uq, g_kv, w_uk, w_uv, rel_bias, w_oa, w_ob,
           w_out, g_ffn, w_rg, b_rg, w_re, b_re, w_gate, w_up, w_down, g_final):
    bsz, seq, d = x.shape
    depth = w_ada.shape[0]
    t = bsz * seq
    heads = rel_bias.shape[1]
    q_lora, kv_lora = g_q.shape[1], g_kv.shape[1]
    tq = 256
    tq_mla = 512
    nblk = -(-LEFT_CHUNKS * CHUNK // tq) + 1
    assert seq % tq == 0 and seq % tq_mla == 0 and tq % CHUNK == 0 and heads % 2 == 0
    assert ROUTE_ROWS == EXPERTS_PER_GROUP and N_GROUPS <= EXPERTS_PER_GROUP

    x2 = x.reshape(t, d)
    pos2 = positions.reshape(t, 1)
    out = x2
    for l in range(depth):
        mod3 = _ada(c, w_ada[l], b_ada[l]).reshape(bsz, -1, d)
        wbig, wq, wkv, wvt, wvbt = _layout_weights(w_in[l], w_uq[l], w_uk[l], w_uv[l], heads, q_lora, kv_lora, d)
        qa, ka, vt, qb, kb, vtb, ga, gb = _proj(
            x2, pos2, mod3, g_mix[l].reshape(1, d), wbig, g_q[l].reshape(1, -1), wq, g_kv[l].reshape(1, -1), wkv,
            wvt, wvbt, seq, tm=512)
        oa = _mla(qa, ka, vt, bsz, seq, tq_mla)
        ob = _band(qb, kb, vtb, _band_bias(rel_bias[l], tq, nblk), bsz, seq, tq, nblk, tiles_per_step=4)

        pad_g = EXPERTS_PER_GROUP - N_GROUPS
        n_route = (N_GROUPS + 1) * EXPERTS_PER_GROUP
        w_r = jnp.concatenate([w_rg[l], jnp.zeros((d, pad_g), F32), w_re[l],
                               jnp.zeros((d, LANES - n_route), F32)], axis=1).T
        b_r = jnp.concatenate([b_rg[l], jnp.zeros((pad_g,), F32), b_re[l],
                               jnp.zeros((LANES - n_route,), F32)]).reshape(LANES, 1)
        w_r_hi = w_r.astype(BF16)
        w_r_lo = (w_r - w_r_hi.astype(F32)).astype(BF16)
        x1, h2x, route_t = _merge(oa, ob, ga, gb, x2, mod3, w_oa[l].astype(BF16), w_ob[l].astype(BF16),
                                  w_out[l].astype(BF16), g_ffn[l].reshape(1, d),
                                  jnp.concatenate([w_r_hi, w_r_lo], axis=0), b_r, seq, tm=512)
        n_tiles = t // MOE_ROWS + N_CLASSES
        pos, counts = _plan(route_t, tc=1024)
        pos = pos.reshape(t)
        exp_lo, exp_hi, n_used, zero_fill = _tile_tables(counts, n_tiles)
        hs = _dispatch(h2x, pos, zero_fill, n_tiles, td=1024)
        y = _moe(hs, exp_lo, exp_hi, n_used, w_gate[l].astype(BF16), w_up[l].astype(BF16), w_down[l].astype(BF16))
        assert l == depth - 1, "multi-layer stacks need an un-normalised residual output"
        out = _final(y, pos, x1, mod3, g_final.reshape(1, d), seq, tf=512)
    return out.reshape(bsz, seq, d)
```

```python
import functools

import numpy as np
import jax
import jax.numpy as jnp
from jax import lax
from jax.experimental import pallas as pl
from jax.experimental.pallas import tpu as pltpu

F32 = jnp.float32
BF16 = jnp.bfloat16

NORM_EPS = 1e-6
NEG_INF = -1e30
ROPE_BASE = 10000.0
CHUNK = 64
LEFT_CHUNKS = 8
MAX_REL = 256
N_GROUPS = 4
EXPERTS_PER_GROUP = 8
PAIRS_PER_GROUP = EXPERTS_PER_GROUP * (EXPERTS_PER_GROUP - 1) // 2
N_CLASSES = N_GROUPS * PAIRS_PER_GROUP
MOE_ROWS = 256
ZERO_ROWS = 64
ROUTE_CLS, ROUTE_W_LO, ROUTE_W_HI = 0, 1, 2
ROUTE_ROWS = 8
LANES = 128
HEAD_DIM = 64
ROPE_DIM = 32
VT_ROWS = HEAD_DIM + 16
LOG2_E = 1.4426950408889634

VMEM_LIMIT = 56 * 1024 * 1024


def _cparams(sem):
    return pltpu.CompilerParams(dimension_semantics=sem, vmem_limit_bytes=VMEM_LIMIT)


def _dot(a, b):
    return jnp.dot(a, b, preferred_element_type=F32)


def _dot_nt(a, b):
    return lax.dot_general(a, b, (((1,), (1,)), ((), ())), preferred_element_type=F32)


def _rms(x, g):
    return x * lax.rsqrt(jnp.mean(x * x, axis=-1, keepdims=True) + NORM_EPS) * g


def _ada_kernel(c_ref, w_ref, b_ref, o_ref):
    c = c_ref[...]
    o_ref[...] = _dot(c * jax.nn.sigmoid(c), w_ref[...]) + b_ref[...]


def _ada(c, w_ada, b_ada):
    bsz, d = c.shape
    n = w_ada.shape[1]
    return pl.pallas_call(
        _ada_kernel,
        grid=(n // d,),
        in_specs=[pl.BlockSpec((bsz, d), lambda j: (0, 0)),
                  pl.BlockSpec((d, d), lambda j: (0, j)),
                  pl.BlockSpec((1, d), lambda j: (0, j))],
        out_specs=pl.BlockSpec((bsz, d), lambda j: (0, j)),
        out_shape=jax.ShapeDtypeStruct((bsz, n), F32),
        compiler_params=_cparams(("arbitrary",)),
        name="ada",
    )(c, w_ada, b_ada.reshape(1, n))


def _proj_kernel(x_ref, pos_ref, mod_ref, gmix_ref, wbig_ref, gq_ref, wq_ref, gkv_ref, wkv_ref,
                 wvt_ref, wvbt_ref, one_ref, inv_ref, sgn_ref,
                 qa_ref, ka_ref, vt_ref, qb_ref, kb_ref, vtb_ref, ga_ref, gb_ref, *, q_lora, kv_lora):
    x = x_ref[...]
    h = _rms(x, gmix_ref[...]) * (1.0 + mod_ref[0, 1:2, :]) + mod_ref[0, 0:1, :]
    hb = h.astype(BF16)

    c0 = q_lora + kv_lora + LANES
    head = _dot(hb, wbig_ref[:, 0:c0])
    q_lat = head[:, 0:q_lora]
    kv_lat = head[:, q_lora:q_lora + kv_lora]
    kr_blk = head[:, q_lora + kv_lora:c0]
    hw = qb_ref.shape[1]
    d = x.shape[1]
    g0 = c0 + 2 * hw
    ga_ref[...] = jax.nn.sigmoid(_dot(hb, wbig_ref[:, g0:g0 + d])).astype(BF16)

    ang = inv_ref[...] * pos_ref[...].astype(F32)
    live = lax.broadcasted_iota(jnp.int32, (LANES, 1), 0) < 2 * ROPE_DIM
    cos_t = jnp.where(live, jnp.cos(ang), 0.0).T
    sin_t = jnp.where(live, jnp.sin(ang) * sgn_ref[...], 0.0).T

    qn = _rms(q_lat, gq_ref[...]).astype(BF16)
    q_all = _dot(qn, wq_ref[...])
    kvn = _rms(kv_lat, gkv_ref[...]).astype(BF16)
    k_nope = _dot(kvn, wkv_ref[...])
    vt_ref[0] = (_dot_nt(wvt_ref[...], kvn) + one_ref[...]).astype(BF16)
    gb_ref[...] = jax.nn.sigmoid(_dot(hb, wbig_ref[:, g0 + d:g0 + 2 * d])).astype(BF16)
    k_rot = (kr_blk * cos_t + pltpu.roll(kr_blk, LANES // 2, 1) * sin_t).astype(BF16)
    for p in range(hw // LANES):
        lo, hi = p * LANES, (p + 1) * LANES
        qa_ref[:, 2 * lo:2 * lo + LANES] = q_all[:, lo:hi].astype(BF16)
        qa_ref[:, 2 * lo + LANES:2 * hi] = (
            q_all[:, hw + lo:hw + hi] * cos_t + q_all[:, 2 * hw + lo:2 * hw + hi] * sin_t).astype(BF16)
        ka_ref[:, 2 * lo:2 * lo + LANES] = k_nope[:, lo:hi].astype(BF16)
        ka_ref[:, 2 * lo + LANES:2 * hi] = k_rot

    qb_ref[...] = _dot(hb, wbig_ref[:, c0:c0 + hw]).astype(BF16)
    kb_ref[...] = _dot(hb, wbig_ref[:, c0 + hw:c0 + 2 * hw]).astype(BF16)
    vtb_ref[0] = (_dot_nt(wvbt_ref[...], hb) + one_ref[...]).astype(BF16)


def _proj(x2, pos2, mod3, g_mix, wbig, g_q, wq, g_kv, wkv, wvt, wvbt, seq, tm):
    t, d = x2.shape
    q_lora, kv_lora = g_q.shape[1], g_kv.shape[1]
    hw = wkv.shape[1]
    per_b = seq // tm
    row = lambda i: (i, 0)
    full = lambda i: (0, 0)
    inv = ROPE_BASE ** (-(np.arange(LANES) % (ROPE_DIM // 2)).astype(np.float32) / (ROPE_DIM // 2))
    sgn = np.where((np.arange(LANES) % ROPE_DIM) < ROPE_DIM // 2, -1.0, 1.0).astype(np.float32)
    ones_col = (np.arange(wvt.shape[0]) % VT_ROWS == HEAD_DIM).astype(np.float32).reshape(-1, 1)
    v_t = jax.ShapeDtypeStruct((t // seq, wvt.shape[0], seq), BF16)
    outs = [jax.ShapeDtypeStruct((t, 2 * hw), BF16), jax.ShapeDtypeStruct((t, 2 * hw), BF16), v_t,
            jax.ShapeDtypeStruct((t, hw), BF16), jax.ShapeDtypeStruct((t, hw), BF16), v_t,
            jax.ShapeDtypeStruct((t, d), BF16), jax.ShapeDtypeStruct((t, d), BF16)]
    v_t_spec = pl.BlockSpec((1, wvt.shape[0], tm), lambda i: (i // per_b, 0, i % per_b))
    out_specs = [v_t_spec if o is v_t else pl.BlockSpec((tm, o.shape[1]), row) for o in outs]
    return pl.pallas_call(
        functools.partial(_proj_kernel, q_lora=q_lora, kv_lora=kv_lora),
        grid=(t // tm,),
        in_specs=[pl.BlockSpec((tm, d), row),
                  pl.BlockSpec((1, tm), lambda i: (0, i)),
                  pl.BlockSpec((1,) + mod3.shape[1:], lambda i: (i // per_b, 0, 0)),
                  pl.BlockSpec(g_mix.shape, full),
                  pl.BlockSpec(wbig.shape, full),
                  pl.BlockSpec(g_q.shape, full),
                  pl.BlockSpec(wq.shape, full),
                  pl.BlockSpec(g_kv.shape, full),
                  pl.BlockSpec(wkv.shape, full),
                  pl.BlockSpec(wvt.shape, full),
                  pl.BlockSpec(wvbt.shape, full),
                  pl.BlockSpec(ones_col.shape, full),
                  pl.BlockSpec((LANES, 1), full),
                  pl.BlockSpec((LANES, 1), full)],
        out_specs=out_specs,
        out_shape=outs,
        compiler_params=_cparams(("parallel",)),
        name="proj",
    )(x2, pos2, mod3, g_mix, wbig, g_q, wq, g_kv, wkv, wvt, wvbt, jnp.asarray(ones_col),
      jnp.asarray(inv).reshape(LANES, 1), jnp.asarray(sgn).reshape(LANES, 1))


def _mla_kernel(q_ref, k_ref, vt_ref, o_ref, sa, sb, *, tq):
    i, nq = pl.program_id(2), pl.num_programs(2)
    lane = lax.broadcasted_iota(jnp.int32, (1, 2 * LANES), 1)
    first = (lane < HEAD_DIM) | ((lane >= LANES) & (lane < LANES + ROPE_DIM))
    second = ((lane >= HEAD_DIM) & (lane < LANES)) | ((lane >= LANES + ROPE_DIM) & (lane < LANES + 2 * ROPE_DIM))

    def head_queries(tile):
        q = q_ref[pl.ds(pl.multiple_of(tile * tq, tq), tq), :]
        zero = jnp.zeros_like(q)
        return (jnp.where(first, q, zero), jnp.where(second, q, zero))

    qs = head_queries(i)

    def scores(queries, blk, buf):
        k = k_ref[pl.ds(pl.multiple_of(blk * tq, tq), tq), :]
        for h, qh in enumerate(queries):
            buf[h] = _dot_nt(k, qh)

    def consume(blk, buf, state, mask):
        start = pl.multiple_of(blk * tq, tq)
        ps, ms, alphas = [], [], []
        for h, (m, _) in enumerate(state):
            st = buf[h]
            if mask is not None:
                st = jnp.where(mask, st, NEG_INF)
            m_new = jnp.maximum(m, jnp.max(st, axis=0, keepdims=True))
            ms.append(m_new)
            alphas.append(jnp.exp2(m - m_new))
            ps.append(jnp.exp2(st - m_new).astype(BF16))
        new = []
        for h, (p, m_new, alpha, (_, acc)) in enumerate(zip(ps, ms, alphas, state)):
            vt = vt_ref[0, h * VT_ROWS:(h + 1) * VT_ROWS, pl.ds(start, tq)]
            new.append((m_new, alpha * acc + _dot(vt, p)))
        return tuple(new)

    causal = (lax.broadcasted_iota(jnp.int32, (tq, tq), 0) // CHUNK
              <= lax.broadcasted_iota(jnp.int32, (tq, tq), 1) // CHUNK)
    following = jnp.minimum(i + 1, nq - 1)

    def tile(sx, sy):
        @pl.when(i == 0)
        def _():
            scores(qs, 0, sx)

        def two_blocks(jj, state):
            scores(qs, 2 * jj + 1, sy)
            state = consume(2 * jj, sx, state, None)
            scores(qs, 2 * jj + 2, sx)
            return consume(2 * jj + 1, sy, state, None)

        def even_tail(state):
            scores(head_queries(following), 0, sy)
            return consume(i, sx, state, causal)

        def odd_tail(state):
            scores(qs, i, sy)
            state = consume(i - 1, sx, state, None)
            scores(head_queries(following), 0, sx)
            return consume(i, sy, state, causal)

        init = tuple((jnp.full((1, tq), NEG_INF, F32), jnp.zeros((VT_ROWS, tq), F32)) for _ in range(2))
        state = lax.fori_loop(0, i // 2, two_blocks, init)
        (_, a0), (_, a1) = lax.cond(i % 2 == 1, odd_tail, even_tail, state)
        out_t = jnp.concatenate([a[0:HEAD_DIM] * (1.0 / a[HEAD_DIM:HEAD_DIM + 1]) for a in (a0, a1)], axis=0)
        o_ref[...] = out_t.T.astype(o_ref.dtype)

    swapped = ((i + 1) // 2) % 2

    @pl.when(swapped == 0)
    def _():
        tile(sa, sb)

    @pl.when(swapped == 1)
    def _():
        tile(sb, sa)


def _mla(qa, ka, vt, bsz, seq, tq):
    t = qa.shape[0]
    pairs = qa.shape[1] // (2 * LANES)
    nq = seq // tq
    return pl.pallas_call(
        functools.partial(_mla_kernel, tq=tq),
        grid=(bsz, pairs, nq),
        in_specs=[pl.BlockSpec((seq, 2 * LANES), lambda b, p, i: (b, p)),
                  pl.BlockSpec((seq, 2 * LANES), lambda b, p, i: (b, p)),
                  pl.BlockSpec((1, 2 * VT_ROWS, seq), lambda b, p, i: (b, p, 0))],
        out_specs=pl.BlockSpec((tq, LANES), lambda b, p, i: (b * nq + i, p)),
        out_shape=jax.ShapeDtypeStruct((t, pairs * LANES), BF16),
        scratch_shapes=[pltpu.VMEM((2, tq, tq), F32)] * 2,
        compiler_params=_cparams(("parallel", "parallel", "arbitrary")),
        name="mla",
    )(qa, ka, vt)


def _band_kernel(q_ref, k_ref, vt_ref, bias_ref, o_ref, *, tq, nblk):
    lane = lax.broadcasted_iota(jnp.int32, (1, LANES), 1)

    def scores(n, edge):
        tile = first + n
        q = q_ref[n * tq:(n + 1) * tq, :]
        zero = jnp.zeros_like(q)
        qs = (jnp.where(lane < HEAD_DIM, q, zero), jnp.where(lane >= HEAD_DIM, q, zero))
        blks = [tile - (nblk - 1) + j for j in range(nblk)]
        starts = [pl.multiple_of((jnp.maximum(blk, 0) if edge else blk) * tq, tq) for blk in blks]
        ks = [k_ref[pl.ds(s, tq), :] for s in starts]
        sts = [[_dot_nt(ks[j], qh) + bias_ref[0, h, j * tq:(j + 1) * tq, :] for j in range(nblk)]
               for h, qh in enumerate(qs)]
        if edge:
            sts = [[jnp.where(blks[j] >= 0, st, NEG_INF) for j, st in enumerate(row)] for row in sts]
        return starts, sts

    def finish(starts, sts):
        outs = []
        for h in range(2):
            m = functools.reduce(jnp.maximum, [jnp.max(st, axis=0, keepdims=True) for st in sts[h]])
            acc = functools.reduce(jnp.add, [
                _dot(vt_ref[0, h * VT_ROWS:(h + 1) * VT_ROWS, pl.ds(starts[j], tq)], jnp.exp2(st - m).astype(BF16))
                for j, st in enumerate(sts[h])])
            outs.append(acc[0:HEAD_DIM] * (1.0 / acc[HEAD_DIM:HEAD_DIM + 1]))
        return jnp.concatenate(outs, axis=0).T

    tiles_per_step = q_ref.shape[0] // tq
    assert tiles_per_step >= nblk - 1
    first = pl.program_id(2) * tiles_per_step

    def tiles(edge):
        pending = scores(0, edge)
        for n in range(tiles_per_step):
            upcoming = scores(n + 1, edge) if n + 1 < tiles_per_step else None
            o_ref[n * tq:(n + 1) * tq, :] = finish(*pending).astype(o_ref.dtype)
            pending = upcoming

    for edge in (True, False):
        @pl.when((pl.program_id(2) == 0) == edge)
        def _():
            tiles(edge)


def _band_bias(rel_table, tq, nblk):
    nk = nblk * tq
    ring = nk + tq
    d = np.arange(ring)
    query_minus_key = np.where(d < tq, d, d - ring) + (nblk - 1) * tq
    idx = np.clip(query_minus_key, -MAX_REL, MAX_REL) + MAX_REL
    select = (idx[:, None] == np.arange(rel_table.shape[1])[None, :]).astype(np.float32)
    per_offset = jnp.dot(rel_table.astype(F32), jnp.asarray(select).T, precision=lax.Precision.HIGHEST)
    heads = rel_table.shape[0]
    toeplitz = jnp.tile(per_offset, (1, nk))[:, :nk * (ring - 1)].reshape(heads, nk, ring - 1)[:, :, :tq]
    qc = np.arange(tq)[None, :] // CHUNK
    kc = np.arange(nk)[:, None] // CHUNK - ((nblk - 1) * tq // CHUNK - LEFT_CHUNKS)
    band = (kc >= qc) & (kc <= qc + LEFT_CHUNKS)
    bias = jnp.where(band[None], toeplitz * LOG2_E, NEG_INF)
    return bias.reshape(heads // 2, 2, nk, tq)


def _band(qb, kb, vtb, bias, bsz, seq, tq, nblk, tiles_per_step):
    t, hw = qb.shape
    pairs = hw // LANES
    rows = tq * tiles_per_step
    nq = seq // rows
    return pl.pallas_call(
        functools.partial(_band_kernel, tq=tq, nblk=nblk),
        grid=(bsz, pairs, nq),
        in_specs=[pl.BlockSpec((rows, LANES), lambda b, p, i: (b * nq + i, p)),
                  pl.BlockSpec((seq, LANES), lambda b, p, i: (b, p)),
                  pl.BlockSpec((1, 2 * VT_ROWS, seq), lambda b, p, i: (b, p, 0)),
                  pl.BlockSpec((1,) + bias.shape[1:], lambda b, p, i: (p, 0, 0, 0))],
        out_specs=pl.BlockSpec((rows, LANES), lambda b, p, i: (b * nq + i, p)),
        out_shape=jax.ShapeDtypeStruct((t, hw), BF16),
        compiler_params=_cparams(("parallel", "parallel", "arbitrary")),
        name="band",
    )(qb, kb, vtb, bias)


def _merge_kernel(oa_ref, ob_ref, ga_ref, gb_ref, x_ref, mod_ref, woa_ref, wob_ref, wout_ref, gffn_ref,
                  wr_ref, br_ref, x1_ref, h2x_ref, rt_ref):
    merged = (ga_ref[...].astype(F32) * _dot(oa_ref[...], woa_ref[...])
              + gb_ref[...].astype(F32) * _dot(ob_ref[...], wob_ref[...]))
    mix = _dot(merged.astype(BF16), wout_ref[...])
    x1 = x_ref[...] + mod_ref[0, 2:3, :] * mix
    x1_ref[...] = x1
    h2 = _rms(x1, gffn_ref[...]) * (1.0 + mod_ref[0, 4:5, :]) + mod_ref[0, 3:4, :]

    h_hi = h2.astype(BF16)
    h_lo = (h2 - h_hi.astype(F32)).astype(BF16)
    r_hi = _dot_nt(wr_ref[...], h_hi)
    logits = (r_hi[0:LANES] + _dot_nt(wr_ref[0:LANES, :], h_lo) + r_hi[LANES:2 * LANES]) + br_ref[...]
    blk = EXPERTS_PER_GROUP
    sub = lax.broadcasted_iota(jnp.int32, (blk, logits.shape[1]), 0)
    far = jnp.int32(blk)
    is_g = sub < N_GROUPS
    gl = jnp.where(is_g, logits[0:blk], NEG_INF)
    g_max = jnp.max(gl, axis=0, keepdims=True)
    g_w = 1.0 / jnp.sum(jnp.where(is_g, jnp.exp(gl - g_max), 0.0), axis=0, keepdims=True)
    g_idx = jnp.min(jnp.where(gl == g_max, sub, far), axis=0, keepdims=True)
    el = logits[blk:2 * blk]
    for g in range(1, N_GROUPS):
        el = jnp.where(g_idx == g, logits[(g + 1) * blk:(g + 2) * blk], el)
    e1 = jnp.max(el, axis=0, keepdims=True)
    i1 = jnp.min(jnp.where(el == e1, sub, far), axis=0, keepdims=True)
    el2 = jnp.where(sub == i1, NEG_INF, el)
    e2 = jnp.max(el2, axis=0, keepdims=True)
    i2 = jnp.min(jnp.where(el2 == e2, sub, far), axis=0, keepdims=True)
    ratio = jnp.exp(e2 - e1)
    w1 = g_w / (1.0 + ratio)
    w2 = g_w * ratio / (1.0 + ratio)
    e_lo, e_hi = jnp.minimum(i1, i2), jnp.maximum(i1, i2)
    cls = (g_idx * PAIRS_PER_GROUP + jnp.right_shift(e_lo * (2 * EXPERTS_PER_GROUP - 1 - e_lo), 1)
           + (e_hi - e_lo - 1))
    first_is_lo = i1 < i2
    w_lo, w_hi = jnp.where(first_is_lo, w1, w2), jnp.where(first_is_lo, w2, w1)
    route_t = (jnp.where(sub == ROUTE_CLS, cls.astype(F32), 0.0) + jnp.where(sub == ROUTE_W_LO, w_lo, 0.0)
               + jnp.where(sub == ROUTE_W_HI, w_hi, 0.0))
    rt_ref[...] = route_t
    d = h2.shape[1]
    h2x_ref[:, 0:d] = h2
    h2x_ref[:, d:d + LANES] = jnp.concatenate(
        [route_t, jnp.zeros((LANES - ROUTE_ROWS, route_t.shape[1]), F32)], axis=0).T


def _merge(oa, ob, ga, gb, x2, mod3, woa, wob, wout, g_ffn, wr, br, seq, tm):
    t, d = x2.shape
    hw = oa.shape[1]
    per_b = seq // tm
    row = lambda i: (i, 0)
    full = lambda i: (0, 0)
    return pl.pallas_call(
        _merge_kernel,
        grid=(t // tm,),
        in_specs=[pl.BlockSpec((tm, hw), row), pl.BlockSpec((tm, hw), row),
                  pl.BlockSpec((tm, d), row), pl.BlockSpec((tm, d), row),
                  pl.BlockSpec((tm, d), row),
                  pl.BlockSpec((1,) + mod3.shape[1:], lambda i: (i // per_b, 0, 0)),
                  pl.BlockSpec(woa.shape, full), pl.BlockSpec(wob.shape, full),
                  pl.BlockSpec(wout.shape, full), pl.BlockSpec(g_ffn.shape, full),
                  pl.BlockSpec(wr.shape, full), pl.BlockSpec(br.shape, full)],
        out_specs=[pl.BlockSpec((tm, d), row), pl.BlockSpec((tm, d + LANES), row),
                   pl.BlockSpec((ROUTE_ROWS, tm), lambda i: (0, i))],
        out_shape=[jax.ShapeDtypeStruct((t, d), F32), jax.ShapeDtypeStruct((t, d + LANES), F32),
                   jax.ShapeDtypeStruct((ROUTE_ROWS, t), F32)],
        compiler_params=_cparams(("parallel",)),
        name="merge",
    )(oa, ob, ga, gb, x2, mod3, woa, wob, wout, g_ffn, wr, br)


def _plan_kernel(rt_ref, pos_ref, cnt_ref, carry_ref, off_ref, *, tc):
    phase, i = pl.program_id(0), pl.program_id(1)
    cls = rt_ref[ROUTE_CLS:ROUTE_CLS + 1, :]
    onehot = lax.broadcasted_iota(jnp.int32, (LANES, tc), 0).astype(F32) == cls
    per_class = jnp.sum(onehot.astype(F32), axis=1, keepdims=True)

    @pl.when((phase == 0) & (i == 0))
    def _():
        cnt_ref[...] = jnp.zeros_like(cnt_ref)

    @pl.when(phase == 0)
    def _():
        cnt_ref[...] += per_class

    @pl.when((phase == 1) & (i == 0))
    def _():
        padded = jnp.floor((cnt_ref[...] + (MOE_ROWS - 1)) * (1.0 / MOE_ROWS)) * MOE_ROWS
        hi = jnp.floor(padded * (1.0 / 256.0))
        digits = [jnp.broadcast_to(v, (LANES, LANES)).astype(BF16) for v in (hi, padded - 256.0 * hi)]
        below = (lax.broadcasted_iota(jnp.int32, (LANES, LANES), 1)
                 < lax.broadcasted_iota(jnp.int32, (LANES, LANES), 0)).astype(BF16)
        off = 256.0 * _dot(below, digits[0]) + _dot(below, digits[1])
        off_ref[...] = off[:, 0:1]
        carry_ref[...] = jnp.zeros_like(carry_ref)

    @pl.when(phase == 1)
    def _():
        earlier = (lax.broadcasted_iota(jnp.int32, (tc, tc), 0)
                   < lax.broadcasted_iota(jnp.int32, (tc, tc), 1)).astype(BF16)
        rank = _dot(onehot.astype(BF16), earlier)
        row = rank + (carry_ref[...] + off_ref[...])
        pos_ref[...] = jnp.sum(jnp.where(onehot, row, 0.0), axis=0, keepdims=True).astype(jnp.int32)
        carry_ref[...] += per_class


def _plan(route_t, tc):
    t = route_t.shape[1]
    return pl.pallas_call(
        functools.partial(_plan_kernel, tc=tc),
        grid=(2, t // tc),
        in_specs=[pl.BlockSpec((ROUTE_ROWS, tc), lambda ph, i: (0, i))],
        out_specs=[pl.BlockSpec((1, tc), lambda ph, i: (0, i * ph)),
                   pl.BlockSpec((LANES, 1), lambda ph, i: (0, 0))],
        out_shape=[jax.ShapeDtypeStruct((1, t), jnp.int32), jax.ShapeDtypeStruct((LANES, 1), F32)],
        scratch_shapes=[pltpu.VMEM((LANES, 1), F32), pltpu.VMEM((LANES, 1), F32)],
        compiler_params=_cparams(("arbitrary", "arbitrary")),
        name="plan",
    )(route_t)


def _tile_tables(counts, n_tiles):
    pairs = [(lo, hi) for lo in range(EXPERTS_PER_GROUP) for hi in range(lo + 1, EXPERTS_PER_GROUP)]
    exp_lo = np.array([g * EXPERTS_PER_GROUP + lo for g in range(N_GROUPS) for lo, _ in pairs], np.int32)
    exp_hi = np.array([g * EXPERTS_PER_GROUP + hi for g in range(N_GROUPS) for _, hi in pairs], np.int32)
    cnt = counts.reshape(-1)[:N_CLASSES].astype(jnp.int32)
    tiles_per_class = (cnt + (MOE_ROWS - 1)) // MOE_ROWS
    tile_end = jnp.cumsum(tiles_per_class)
    n_used = tile_end[-1]
    j = jnp.arange(n_tiles, dtype=jnp.int32)
    tile_cls = jnp.sum(tile_end[None, :] <= jnp.minimum(j, n_used - 1)[:, None], axis=1, dtype=jnp.int32)
    valid_end = (tile_end - tiles_per_class) * MOE_ROWS + cnt
    of_class = tile_cls[:, None] == jnp.arange(N_CLASSES, dtype=jnp.int32)[None, :]

    def per_tile(table):
        return jnp.sum(jnp.where(of_class, table[None, :], 0), axis=1, dtype=jnp.int32)

    units = MOE_ROWS // ZERO_ROWS
    unit_end = (jnp.arange(n_tiles * units, dtype=jnp.int32) + 1) * ZERO_ROWS
    zero_fill = (unit_end.reshape(n_tiles, units) > per_tile(valid_end)[:, None]).astype(jnp.int32).reshape(-1)
    return (per_tile(jnp.asarray(exp_lo)), per_tile(jnp.asarray(exp_hi)), n_used.reshape(1).astype(jnp.int32),
            zero_fill)


def _row_copy(src, src_row, dst, dst_row, n, sem):
    return pltpu.make_async_copy(src.at[pl.ds(src_row, n)], dst.at[pl.ds(dst_row, n)], sem)


def _dispatch_kernel(pos_ref, zf_ref, h_ref, hs_ref, zero_ref, sem, *, td, n_tiles):
    i = pl.program_id(0)

    @pl.when(i == 0)
    def _():
        zero_ref[...] = jnp.zeros_like(zero_ref)

        def fill(j, n):
            @pl.when(zf_ref[j] > 0)
            def _():
                _row_copy(zero_ref, 0, hs_ref, pl.multiple_of(j * ZERO_ROWS, ZERO_ROWS), ZERO_ROWS, sem).start()
            return n + zf_ref[j]

        n_fill = lax.fori_loop(0, n_tiles * (MOE_ROWS // ZERO_ROWS), fill, jnp.int32(0))

        def drain(_, carry):
            _row_copy(zero_ref, 0, hs_ref, 0, ZERO_ROWS, sem).wait()
            return carry

        lax.fori_loop(0, n_fill, drain, 0)

    base = i * td

    for r in range(td):
        _row_copy(h_ref, r, hs_ref, pos_ref[base + r], 1, sem).start()
    _row_copy(h_ref, 0, hs_ref, 0, td, sem).wait()


def _dispatch(h2x, pos, zero_fill, n_tiles, td):
    t, w = h2x.shape
    return pl.pallas_call(
        functools.partial(_dispatch_kernel, td=td, n_tiles=n_tiles),
        grid_spec=pltpu.PrefetchScalarGridSpec(
            num_scalar_prefetch=2,
            grid=(t // td,),
            in_specs=[pl.BlockSpec((td, w), lambda i, pos, zf: (i, 0))],
            out_specs=pl.BlockSpec(memory_space=pl.ANY),
            scratch_shapes=[pltpu.VMEM((ZERO_ROWS, w), F32), pltpu.SemaphoreType.DMA(())]),
        out_shape=jax.ShapeDtypeStruct((n_tiles * MOE_ROWS, w), F32),
        compiler_params=_cparams(("arbitrary",)),
        name="dispatch",
    )(pos, zero_fill, h2x)


def _moe_kernel(elo_ref, ehi_ref, nu_ref, hs_ref, wg_lo, wu_lo, wd_lo, wg_hi, wu_hi, wd_hi, y_ref):
    j = pl.program_id(0)
    d = y_ref.shape[1]

    @pl.when(j < nu_ref[0])
    def _():
        h = hs_ref[:, 0:d].astype(BF16)
        ups = [(_dot(h, wg[0]), _dot(h, wu[0])) for wg, wu in ((wg_lo, wu_lo), (wg_hi, wu_hi))]
        hids = [((a * jax.nn.sigmoid(a)) * u * hs_ref[:, d + lane:d + lane + 1]).astype(BF16)
                for (a, u), lane in zip(ups, (ROUTE_W_LO, ROUTE_W_HI))]
        y_ref[...] = _dot(hids[0], wd_lo[0]) + _dot(hids[1], wd_hi[0])

    @pl.when(j >= nu_ref[0])
    def _():
        y_ref[...] = jnp.zeros_like(y_ref)


def _moe(hs, exp_lo, exp_hi, n_used, wg, wu, wd):
    n_tiles = hs.shape[0] // MOE_ROWS
    _, d, ff = wg.shape
    up = lambda sel: pl.BlockSpec((1, d, ff), lambda j, elo, ehi, nu: ((elo, ehi)[sel][j], 0, 0))
    down = lambda sel: pl.BlockSpec((1, ff, d), lambda j, elo, ehi, nu: ((elo, ehi)[sel][j], 0, 0))
    return pl.pallas_call(
        _moe_kernel,
        grid_spec=pltpu.PrefetchScalarGridSpec(
            num_scalar_prefetch=3,
            grid=(n_tiles,),
            in_specs=[pl.BlockSpec((MOE_ROWS, hs.shape[1]),
                                   lambda j, elo, ehi, nu: (jnp.minimum(j, nu[0] - 1), 0)),
                      up(0), up(0), down(0), up(1), up(1), down(1)],
            out_specs=pl.BlockSpec((MOE_ROWS, d), lambda j, elo, ehi, nu: (j, 0))),
        out_shape=jax.ShapeDtypeStruct((n_tiles * MOE_ROWS, d), F32),
        compiler_params=_cparams(("arbitrary",)),
        name="moe",
    )(exp_lo, exp_hi, n_used, hs, wg, wu, wd, wg, wu, wd)


def _final_kernel(pos_ref, x1_ref, mod_ref, gfin_ref, y_ref, o_ref, rows_ref, sem, *, tf):
    i, n = pl.program_id(0), pl.num_programs(0)

    def fetch(step, slot):
        for r in range(tf):
            _row_copy(y_ref, pos_ref[step * tf + r], rows_ref.at[slot], r, 1, sem.at[slot]).start()

    @pl.when(i == 0)
    def _():
        fetch(0, 0)

    @pl.when(i + 1 < n)
    def _():
        fetch(i + 1, (i + 1) % 2)

    slot = i % 2
    _row_copy(y_ref, 0, rows_ref.at[slot], 0, tf, sem.at[slot]).wait()
    x2 = x1_ref[...] + mod_ref[0, 5:6, :] * rows_ref[slot]
    o_ref[...] = _rms(x2, gfin_ref[...])


def _final(y, pos, x1, mod3, g_final, seq, tf):
    t, d = x1.shape
    per_b = seq // tf
    return pl.pallas_call(
        functools.partial(_final_kernel, tf=tf),
        grid_spec=pltpu.PrefetchScalarGridSpec(
            num_scalar_prefetch=1,
            grid=(t // tf,),
            in_specs=[pl.BlockSpec((tf, d), lambda i, pos: (i, 0)),
                      pl.BlockSpec((1,) + mod3.shape[1:], lambda i, pos: (i // per_b, 0, 0)),
                      pl.BlockSpec((1, d), lambda i, pos: (0, 0)),
                      pl.BlockSpec(memory_space=pl.ANY)],
            out_specs=pl.BlockSpec((tf, d), lambda i, pos: (i, 0)),
            scratch_shapes=[pltpu.VMEM((2, tf, d), F32), pltpu.SemaphoreType.DMA((2,))]),
        out_shape=jax.ShapeDtypeStruct((t, d), F32),
        compiler_params=_cparams(("arbitrary",)),
        name="final",
    )(pos, x1, mod3, g_final, y)


def _layout_weights(w_in, w_uq, w_uk, w_uv, heads, q_lora, kv_lora, d):
    hw = heads * HEAD_DIM
    cuts = np.cumsum([q_lora, kv_lora, ROPE_DIM, hw, hw, hw, d, d])
    w_qlat, w_kvlat, w_kr, w_qb, w_kb, w_vb, w_ga, w_gb = jnp.split(w_in, [int(v) for v in cuts[:-1]], axis=1)
    swap = np.concatenate([np.arange(ROPE_DIM // 2, ROPE_DIM), np.arange(ROPE_DIM // 2)])
    w_kr_sw = w_kr[:, swap]
    kr_blk = jnp.concatenate([w_kr, w_kr, w_kr_sw, w_kr_sw], axis=1)
    band_scale = HEAD_DIM ** -0.5 * LOG2_E
    wbig = jnp.concatenate([w_qlat, w_kvlat, kr_blk, w_qb * band_scale, w_kb, w_ga, w_gb], axis=1)

    def transposed_values(w, k):
        return jnp.pad(w.reshape(k, heads, HEAD_DIM).transpose(1, 2, 0),
                       ((0, 0), (0, VT_ROWS - HEAD_DIM), (0, 0))).reshape(heads * VT_ROWS, k)

    qk = HEAD_DIM + ROPE_DIM
    wq3 = w_uq.reshape(q_lora, heads, qk) * (qk ** -0.5 * LOG2_E)
    nope = wq3[:, :, :HEAD_DIM].reshape(q_lora, hw)
    rope = wq3[:, :, HEAD_DIM:]
    pad = jnp.zeros((q_lora, heads // 2, LANES - 2 * ROPE_DIM), w_uq.dtype)

    def lay(r):
        return jnp.concatenate([r.reshape(q_lora, heads // 2, 2 * ROPE_DIM), pad], axis=2).reshape(q_lora, hw)

    wq = jnp.concatenate([nope, lay(rope), lay(rope[:, :, swap])], axis=1)
    return (wbig.astype(BF16), wq.astype(BF16), w_uk.astype(BF16), transposed_values(w_uv, kv_lora).astype(BF16),
            transposed_values(w_vb, d).astype(BF16))


def kernel(x, c, positions, w_ada, b_ada, g_mix, w_in, g_q, w_uq, g_kv, w_uk, w_uv, rel_bias, w_oa, w_ob,
           w_out, g_ffn, w_rg, b_rg, w_re, b_re, w_gate, w_up, w_down, g_final):
    bsz, seq, d = x.shape
    depth = w_ada.shape[0]
    t = bsz * seq
    heads = rel_bias.shape[1]
    q_lora, kv_lora = g_q.shape[1], g_kv.shape[1]
    tq = 256
    tq_mla = 512
    nblk = -(-LEFT_CHUNKS * CHUNK // tq) + 1
    assert seq % tq == 0 and seq % tq_mla == 0 and tq % CHUNK == 0 and heads % 2 == 0
    assert ROUTE_ROWS == EXPERTS_PER_GROUP and N_GROUPS <= EXPERTS_PER_GROUP

    x2 = x.reshape(t, d)
    pos2 = positions.reshape(1, t)
    out = x2
    for l in range(depth):
        mod3 = _ada(c, w_ada[l], b_ada[l]).reshape(bsz, -1, d)
        wbig, wq, wkv, wvt, wvbt = _layout_weights(w_in[l], w_uq[l], w_uk[l], w_uv[l], heads, q_lora, kv_lora, d)
        qa, ka, vt, qb, kb, vtb, ga, gb = _proj(
            x2, pos2, mod3, g_mix[l].reshape(1, d), wbig, g_q[l].reshape(1, -1), wq, g_kv[l].reshape(1, -1), wkv,
            wvt, wvbt, seq, tm=512)
        oa = _mla(qa, ka, vt, bsz, seq, tq_mla)
        ob = _band(qb, kb, vtb, _band_bias(rel_bias[l], tq, nblk), bsz, seq, tq, nblk, tiles_per_step=4)

        pad_g = EXPERTS_PER_GROUP - N_GROUPS
        n_route = (N_GROUPS + 1) * EXPERTS_PER_GROUP
        w_r = jnp.concatenate([w_rg[l], jnp.zeros((d, pad_g), F32), w_re[l],
                               jnp.zeros((d, LANES - n_route), F32)], axis=1).T
        b_r = jnp.concatenate([b_rg[l], jnp.zeros((pad_g,), F32), b_re[l],
                               jnp.zeros((LANES - n_route,), F32)]).reshape(LANES, 1)
        w_r_hi = w_r.astype(BF16)
        w_r_lo = (w_r - w_r_hi.astype(F32)).astype(BF16)
        x1, h2x, route_t = _merge(oa, ob, ga, gb, x2, mod3, w_oa[l].astype(BF16), w_ob[l].astype(BF16),
                                  w_out[l].astype(BF16), g_ffn[l].reshape(1, d),
                                  jnp.concatenate([w_r_hi, w_r_lo], axis=0), b_r, seq, tm=512)
        n_tiles = t // MOE_ROWS + N_CLASSES
        pos, counts = _plan(route_t, tc=1024)
        pos = pos.reshape(t)
        exp_lo, exp_hi, n_used, zero_fill = _tile_tables(counts, n_tiles)
        hs = _dispatch(h2x, pos, zero_fill, n_tiles, td=2048)
        y = _moe(hs, exp_lo, exp_hi, n_used, w_gate[l].astype(BF16), w_up[l].astype(BF16), w_down[l].astype(BF16))
        assert l == depth - 1, "multi-layer stacks need an un-normalised residual output"
        out = _final(y, pos, x1, mod3, g_final.reshape(1, d), seq, tf=512)
    return out.reshape(bsz, seq, d)
```

```python
import functools

import numpy as np
import jax
import jax.numpy as jnp
from jax import lax
from jax.experimental import pallas as pl
from jax.experimental.pallas import tpu as pltpu

F32 = jnp.float32
BF16 = jnp.bfloat16

NORM_EPS = 1e-6
NEG_INF = -1e30
ROPE_BASE = 10000.0
CHUNK = 64
LEFT_CHUNKS = 8
MAX_REL = 256
N_GROUPS = 4
EXPERTS_PER_GROUP = 8
PAIRS_PER_GROUP = EXPERTS_PER_GROUP * (EXPERTS_PER_GROUP - 1) // 2
N_CLASSES = N_GROUPS * PAIRS_PER_GROUP
MOE_ROWS = 256
ZERO_ROWS = 64
ROUTE_CLS, ROUTE_W_LO, ROUTE_W_HI = 0, 1, 2
ROUTE_ROWS = 8
LANES = 128
HEAD_DIM = 64
ROPE_DIM = 32
VT_ROWS = HEAD_DIM + 16
LOG2_E = 1.4426950408889634

VMEM_LIMIT = 56 * 1024 * 1024


def _cparams(sem):
    return pltpu.CompilerParams(dimension_semantics=sem, vmem_limit_bytes=VMEM_LIMIT)


def _dot(a, b):
    return jnp.dot(a, b, preferred_element_type=F32)


def _dot_nt(a, b):
    return lax.dot_general(a, b, (((1,), (1,)), ((), ())), preferred_element_type=F32)


def _rms(x, g):
    return x * lax.rsqrt(jnp.mean(x * x, axis=-1, keepdims=True) + NORM_EPS) * g


def _ada_kernel(c_ref, w_ref, b_ref, o_ref):
    c = c_ref[...]
    o_ref[...] = _dot(c * jax.nn.sigmoid(c), w_ref[...]) + b_ref[...]


def _ada(c, w_ada, b_ada):
    bsz, d = c.shape
    n = w_ada.shape[1]
    return pl.pallas_call(
        _ada_kernel,
        grid=(n // d,),
        in_specs=[pl.BlockSpec((bsz, d), lambda j: (0, 0)),
                  pl.BlockSpec((d, d), lambda j: (0, j)),
                  pl.BlockSpec((1, d), lambda j: (0, j))],
        out_specs=pl.BlockSpec((bsz, d), lambda j: (0, j)),
        out_shape=jax.ShapeDtypeStruct((bsz, n), F32),
        compiler_params=_cparams(("arbitrary",)),
        name="ada",
    )(c, w_ada, b_ada.reshape(1, n))


def _proj_kernel(x_ref, pos_ref, mod_ref, gmix_ref, wbig_ref, gq_ref, wq_ref, gkv_ref, wkv_ref,
                 wvt_ref, wvbt_ref, one_ref, inv_ref, sgn_ref,
                 qa_ref, ka_ref, vt_ref, qb_ref, kb_ref, vtb_ref, ga_ref, gb_ref, *, q_lora, kv_lora):
    x = x_ref[...]
    h = _rms(x, gmix_ref[...]) * (1.0 + mod_ref[0, 1:2, :]) + mod_ref[0, 0:1, :]
    hb = h.astype(BF16)

    c0 = q_lora + kv_lora + LANES
    head = _dot(hb, wbig_ref[:, 0:c0])
    q_lat = head[:, 0:q_lora]
    kv_lat = head[:, q_lora:q_lora + kv_lora]
    kr_blk = head[:, q_lora + kv_lora:c0]
    hw = qb_ref.shape[1]
    d = x.shape[1]
    g0 = c0 + 2 * hw
    ga_ref[...] = jax.nn.sigmoid(_dot(hb, wbig_ref[:, g0:g0 + d])).astype(BF16)

    ang = inv_ref[...] * pos_ref[0].astype(F32)
    live = lax.broadcasted_iota(jnp.int32, (LANES, 1), 0) < 2 * ROPE_DIM
    cos_t = jnp.where(live, jnp.cos(ang), 0.0).T
    sin_t = jnp.where(live, jnp.sin(ang) * sgn_ref[...], 0.0).T

    qn = _rms(q_lat, gq_ref[...]).astype(BF16)
    q_all = _dot(qn, wq_ref[...])
    kvn = _rms(kv_lat, gkv_ref[...]).astype(BF16)
    k_nope = _dot(kvn, wkv_ref[...])
    vt_ref[0] = (_dot_nt(wvt_ref[...], kvn) + one_ref[...]).astype(BF16)
    gb_ref[...] = jax.nn.sigmoid(_dot(hb, wbig_ref[:, g0 + d:g0 + 2 * d])).astype(BF16)
    k_rot = (kr_blk * cos_t + pltpu.roll(kr_blk, LANES // 2, 1) * sin_t).astype(BF16)
    for p in range(hw // LANES):
        lo, hi = p * LANES, (p + 1) * LANES
        qa_ref[:, 2 * lo:2 * lo + LANES] = q_all[:, lo:hi].astype(BF16)
        qa_ref[:, 2 * lo + LANES:2 * hi] = (
            q_all[:, hw + lo:hw + hi] * cos_t + q_all[:, 2 * hw + lo:2 * hw + hi] * sin_t).astype(BF16)
        ka_ref[:, 2 * lo:2 * lo + LANES] = k_nope[:, lo:hi].astype(BF16)
        ka_ref[:, 2 * lo + LANES:2 * hi] = k_rot

    qb_ref[...] = _dot(hb, wbig_ref[:, c0:c0 + hw]).astype(BF16)
    kb_ref[...] = _dot(hb, wbig_ref[:, c0 + hw:c0 + 2 * hw]).astype(BF16)
    vtb_ref[0] = (_dot_nt(wvbt_ref[...], hb) + one_ref[...]).astype(BF16)


def _proj(x2, pos2, mod3, g_mix, wbig, g_q, wq, g_kv, wkv, wvt, wvbt, seq, tm):
    t, d = x2.shape
    q_lora, kv_lora = g_q.shape[1], g_kv.shape[1]
    hw = wkv.shape[1]
    per_b = seq // tm
    row = lambda i: (i, 0)
    full = lambda i: (0, 0)
    inv = ROPE_BASE ** (-(np.arange(LANES) % (ROPE_DIM // 2)).astype(np.float32) / (ROPE_DIM // 2))
    sgn = np.where((np.arange(LANES) % ROPE_DIM) < ROPE_DIM // 2, -1.0, 1.0).astype(np.float32)
    ones_col = (np.arange(wvt.shape[0]) % VT_ROWS == HEAD_DIM).astype(np.float32).reshape(-1, 1)
    v_t = jax.ShapeDtypeStruct((t // seq, wvt.shape[0], seq), BF16)
    outs = [jax.ShapeDtypeStruct((t, 2 * hw), BF16), jax.ShapeDtypeStruct((t, 2 * hw), BF16), v_t,
            jax.ShapeDtypeStruct((t, hw), BF16), jax.ShapeDtypeStruct((t, hw), BF16), v_t,
            jax.ShapeDtypeStruct((t, d), BF16), jax.ShapeDtypeStruct((t, d), BF16)]
    v_t_spec = pl.BlockSpec((1, wvt.shape[0], tm), lambda i: (i // per_b, 0, i % per_b))
    out_specs = [v_t_spec if o is v_t else pl.BlockSpec((tm, o.shape[1]), row) for o in outs]
    return pl.pallas_call(
        functools.partial(_proj_kernel, q_lora=q_lora, kv_lora=kv_lora),
        grid=(t // tm,),
        in_specs=[pl.BlockSpec((tm, d), row),
                  pl.BlockSpec((1, 1, tm), lambda i: (i // per_b, 0, i % per_b)),
                  pl.BlockSpec((1,) + mod3.shape[1:], lambda i: (i // per_b, 0, 0)),
                  pl.BlockSpec(g_mix.shape, full),
                  pl.BlockSpec(wbig.shape, full),
                  pl.BlockSpec(g_q.shape, full),
                  pl.BlockSpec(wq.shape, full),
                  pl.BlockSpec(g_kv.shape, full),
                  pl.BlockSpec(wkv.shape, full),
                  pl.BlockSpec(wvt.shape, full),
                  pl.BlockSpec(wvbt.shape, full),
                  pl.BlockSpec(ones_col.shape, full),
                  pl.BlockSpec((LANES, 1), full),
                  pl.BlockSpec((LANES, 1), full)],
        out_specs=out_specs,
        out_shape=outs,
        compiler_params=_cparams(("parallel",)),
        name="proj",
    )(x2, pos2, mod3, g_mix, wbig, g_q, wq, g_kv, wkv, wvt, wvbt, jnp.asarray(ones_col),
      jnp.asarray(inv).reshape(LANES, 1), jnp.asarray(sgn).reshape(LANES, 1))


def _mla_kernel(q_ref, k_ref, vt_ref, o_ref, sa, sb, *, tq):
    i, nq = pl.program_id(2), pl.num_programs(2)
    lane = lax.broadcasted_iota(jnp.int32, (1, 2 * LANES), 1)
    first = (lane < HEAD_DIM) | ((lane >= LANES) & (lane < LANES + ROPE_DIM))
    second = ((lane >= HEAD_DIM) & (lane < LANES)) | ((lane >= LANES + ROPE_DIM) & (lane < LANES + 2 * ROPE_DIM))

    def head_queries(tile):
        q = q_ref[pl.ds(pl.multiple_of(tile * tq, tq), tq), :]
        zero = jnp.zeros_like(q)
        return (jnp.where(first, q, zero), jnp.where(second, q, zero))

    qs = head_queries(i)

    def scores(queries, blk, buf):
        k = k_ref[pl.ds(pl.multiple_of(blk * tq, tq), tq), :]
        for h, qh in enumerate(queries):
            buf[h] = _dot_nt(k, qh)

    def consume(blk, buf, state, mask):
        start = pl.multiple_of(blk * tq, tq)
        ps, ms, alphas = [], [], []
        for h, (m, _) in enumerate(state):
            st = buf[h]
            if mask is not None:
                st = jnp.where(mask, st, NEG_INF)
            m_new = jnp.maximum(m, jnp.max(st, axis=0, keepdims=True))
            ms.append(m_new)
            alphas.append(jnp.exp2(m - m_new))
            ps.append(jnp.exp2(st - m_new).astype(BF16))
        new = []
        for h, (p, m_new, alpha, (_, acc)) in enumerate(zip(ps, ms, alphas, state)):
            vt = vt_ref[0, h * VT_ROWS:(h + 1) * VT_ROWS, pl.ds(start, tq)]
            new.append((m_new, alpha * acc + _dot(vt, p)))
        return tuple(new)

    causal = (lax.broadcasted_iota(jnp.int32, (tq, tq), 0) // CHUNK
              <= lax.broadcasted_iota(jnp.int32, (tq, tq), 1) // CHUNK)
    following = jnp.minimum(i + 1, nq - 1)

    def tile(sx, sy):
        @pl.when(i == 0)
        def _():
            scores(qs, 0, sx)

        def two_blocks(jj, state):
            scores(qs, 2 * jj + 1, sy)
            state = consume(2 * jj, sx, state, None)
            scores(qs, 2 * jj + 2, sx)
            return consume(2 * jj + 1, sy, state, None)

        def even_tail(state):
            scores(head_queries(following), 0, sy)
            return consume(i, sx, state, causal)

        def odd_tail(state):
            scores(qs, i, sy)
            state = consume(i - 1, sx, state, None)
            scores(head_queries(following), 0, sx)
            return consume(i, sy, state, causal)

        init = tuple((jnp.full((1, tq), NEG_INF, F32), jnp.zeros((VT_ROWS, tq), F32)) for _ in range(2))
        state = lax.fori_loop(0, i // 2, two_blocks, init)
        (_, a0), (_, a1) = lax.cond(i % 2 == 1, odd_tail, even_tail, state)
        out_t = jnp.concatenate([a[0:HEAD_DIM] * (1.0 / a[HEAD_DIM:HEAD_DIM + 1]) for a in (a0, a1)], axis=0)
        o_ref[...] = out_t.T.astype(o_ref.dtype)

    swapped = ((i + 1) // 2) % 2

    @pl.when(swapped == 0)
    def _():
        tile(sa, sb)

    @pl.when(swapped == 1)
    def _():
        tile(sb, sa)


def _mla(qa, ka, vt, bsz, seq, tq):
    t = qa.shape[0]
    pairs = qa.shape[1] // (2 * LANES)
    nq = seq // tq
    return pl.pallas_call(
        functools.partial(_mla_kernel, tq=tq),
        grid=(bsz, pairs, nq),
        in_specs=[pl.BlockSpec((seq, 2 * LANES), lambda b, p, i: (b, p)),
                  pl.BlockSpec((seq, 2 * LANES), lambda b, p, i: (b, p)),
                  pl.BlockSpec((1, 2 * VT_ROWS, seq), lambda b, p, i: (b, p, 0))],
        out_specs=pl.BlockSpec((tq, LANES), lambda b, p, i: (b * nq + i, p)),
        out_shape=jax.ShapeDtypeStruct((t, pairs * LANES), BF16),
        scratch_shapes=[pltpu.VMEM((2, tq, tq), F32)] * 2,
        compiler_params=_cparams(("parallel", "parallel", "arbitrary")),
        name="mla",
    )(qa, ka, vt)


def _band_kernel(q_ref, k_ref, vt_ref, bias_ref, o_ref, *, tq, nblk):
    lane = lax.broadcasted_iota(jnp.int32, (1, LANES), 1)

    def scores(n, edge):
        tile = first + n
        q = q_ref[n * tq:(n + 1) * tq, :]
        zero = jnp.zeros_like(q)
        qs = (jnp.where(lane < HEAD_DIM, q, zero), jnp.where(lane >= HEAD_DIM, q, zero))
        blks = [tile - (nblk - 1) + j for j in range(nblk)]
        starts = [pl.multiple_of((jnp.maximum(blk, 0) if edge else blk) * tq, tq) for blk in blks]
        ks = [k_ref[pl.ds(s, tq), :] for s in starts]
        sts = [[_dot_nt(ks[j], qh) + bias_ref[0, h, j * tq:(j + 1) * tq, :] for j in range(nblk)]
               for h, qh in enumerate(qs)]
        if edge:
            sts = [[jnp.where(blks[j] >= 0, st, NEG_INF) for j, st in enumerate(row)] for row in sts]
        return starts, sts

    def finish(starts, sts):
        outs = []
        for h in range(2):
            m = functools.reduce(jnp.maximum, [jnp.max(st, axis=0, keepdims=True) for st in sts[h]])
            acc = functools.reduce(jnp.add, [
                _dot(vt_ref[0, h * VT_ROWS:(h + 1) * VT_ROWS, pl.ds(starts[j], tq)], jnp.exp2(st - m).astype(BF16))
                for j, st in enumerate(sts[h])])
            outs.append(acc[0:HEAD_DIM] * (1.0 / acc[HEAD_DIM:HEAD_DIM + 1]))
        return jnp.concatenate(outs, axis=0).T

    tiles_per_step = q_ref.shape[0] // tq
    assert tiles_per_step >= nblk - 1
    first = pl.program_id(2) * tiles_per_step

    def tiles(edge):
        pending = scores(0, edge)
        for n in range(tiles_per_step):
            upcoming = scores(n + 1, edge) if n + 1 < tiles_per_step else None
            o_ref[n * tq:(n + 1) * tq, :] = finish(*pending).astype(o_ref.dtype)
            pending = upcoming

    for edge in (True, False):
        @pl.when((pl.program_id(2) == 0) == edge)
        def _():
            tiles(edge)


def _band_bias(rel_table, tq, nblk):
    nk = nblk * tq
    ring = nk + tq
    d = np.arange(ring)
    key_minus_query = np.where(d < nk, d, d - ring)
    idx = np.clip((nblk - 1) * tq - key_minus_query, -MAX_REL, MAX_REL) + MAX_REL
    per_offset = rel_table.astype(F32)[:, idx]
    heads = rel_table.shape[0]
    toeplitz = jnp.tile(per_offset, (1, tq))[:, :tq * (ring - 1)].reshape(heads, tq, ring - 1)[:, :, :nk]
    qc = np.arange(tq)[:, None] // CHUNK
    kc = np.arange(nk)[None, :] // CHUNK - ((nblk - 1) * tq // CHUNK - LEFT_CHUNKS)
    band = (kc >= qc) & (kc <= qc + LEFT_CHUNKS)
    bias = jnp.where(band[None], toeplitz * LOG2_E, NEG_INF)
    return bias.reshape(heads // 2, 2, tq, nk).swapaxes(2, 3)


def _band(qb, kb, vtb, bias, bsz, seq, tq, nblk, tiles_per_step):
    t, hw = qb.shape
    pairs = hw // LANES
    rows = tq * tiles_per_step
    nq = seq // rows
    return pl.pallas_call(
        functools.partial(_band_kernel, tq=tq, nblk=nblk),
        grid=(bsz, pairs, nq),
        in_specs=[pl.BlockSpec((rows, LANES), lambda b, p, i: (b * nq + i, p)),
                  pl.BlockSpec((seq, LANES), lambda b, p, i: (b, p)),
                  pl.BlockSpec((1, 2 * VT_ROWS, seq), lambda b, p, i: (b, p, 0)),
                  pl.BlockSpec((1,) + bias.shape[1:], lambda b, p, i: (p, 0, 0, 0))],
        out_specs=pl.BlockSpec((rows, LANES), lambda b, p, i: (b * nq + i, p)),
        out_shape=jax.ShapeDtypeStruct((t, hw), BF16),
        compiler_params=_cparams(("parallel", "parallel", "arbitrary")),
        name="band",
    )(qb, kb, vtb, bias)


def _merge_kernel(oa_ref, ob_ref, ga_ref, gb_ref, x_ref, mod_ref, woa_ref, wob_ref, wout_ref, gffn_ref,
                  wr_ref, br_ref, x1_ref, h2x_ref, rt_ref):
    merged = (ga_ref[...].astype(F32) * _dot(oa_ref[...], woa_ref[...])
              + gb_ref[...].astype(F32) * _dot(ob_ref[...], wob_ref[...]))
    mix = _dot(merged.astype(BF16), wout_ref[...])
    x1 = x_ref[...] + mod_ref[0, 2:3, :] * mix
    x1_ref[...] = x1
    h2 = _rms(x1, gffn_ref[...]) * (1.0 + mod_ref[0, 4:5, :]) + mod_ref[0, 3:4, :]

    h_hi = h2.astype(BF16)
    h_lo = (h2 - h_hi.astype(F32)).astype(BF16)
    r_hi = _dot_nt(wr_ref[...], h_hi)
    logits = (r_hi[0:LANES] + _dot_nt(wr_ref[0:LANES, :], h_lo) + r_hi[LANES:2 * LANES]) + br_ref[...]
    blk = EXPERTS_PER_GROUP
    sub = lax.broadcasted_iota(jnp.int32, (blk, logits.shape[1]), 0)
    far = jnp.int32(blk)
    is_g = sub < N_GROUPS
    gl = jnp.where(is_g, logits[0:blk], NEG_INF)
    g_max = jnp.max(gl, axis=0, keepdims=True)
    g_w = 1.0 / jnp.sum(jnp.where(is_g, jnp.exp(gl - g_max), 0.0), axis=0, keepdims=True)
    g_idx = jnp.min(jnp.where(gl == g_max, sub, far), axis=0, keepdims=True)
    el = logits[blk:2 * blk]
    for g in range(1, N_GROUPS):
        el = jnp.where(g_idx == g, logits[(g + 1) * blk:(g + 2) * blk], el)
    e1 = jnp.max(el, axis=0, keepdims=True)
    i1 = jnp.min(jnp.where(el == e1, sub, far), axis=0, keepdims=True)
    el2 = jnp.where(sub == i1, NEG_INF, el)
    e2 = jnp.max(el2, axis=0, keepdims=True)
    i2 = jnp.min(jnp.where(el2 == e2, sub, far), axis=0, keepdims=True)
    ratio = jnp.exp(e2 - e1)
    w1 = g_w / (1.0 + ratio)
    w2 = g_w * ratio / (1.0 + ratio)
    e_lo, e_hi = jnp.minimum(i1, i2), jnp.maximum(i1, i2)
    cls = (g_idx * PAIRS_PER_GROUP + jnp.right_shift(e_lo * (2 * EXPERTS_PER_GROUP - 1 - e_lo), 1)
           + (e_hi - e_lo - 1))
    first_is_lo = i1 < i2
    w_lo, w_hi = jnp.where(first_is_lo, w1, w2), jnp.where(first_is_lo, w2, w1)
    route_t = (jnp.where(sub == ROUTE_CLS, cls.astype(F32), 0.0) + jnp.where(sub == ROUTE_W_LO, w_lo, 0.0)
               + jnp.where(sub == ROUTE_W_HI, w_hi, 0.0))
    rt_ref[...] = route_t
    d = h2.shape[1]
    h2x_ref[:, 0:d] = h2
    h2x_ref[:, d:d + LANES] = jnp.concatenate(
        [route_t, jnp.zeros((LANES - ROUTE_ROWS, route_t.shape[1]), F32)], axis=0).T


def _merge(oa, ob, ga, gb, x2, mod3, woa, wob, wout, g_ffn, wr, br, seq, tm):
    t, d = x2.shape
    hw = oa.shape[1]
    per_b = seq // tm
    row = lambda i: (i, 0)
    full = lambda i: (0, 0)
    return pl.pallas_call(
        _merge_kernel,
        grid=(t // tm,),
        in_specs=[pl.BlockSpec((tm, hw), row), pl.BlockSpec((tm, hw), row),
                  pl.BlockSpec((tm, d), row), pl.BlockSpec((tm, d), row),
                  pl.BlockSpec((tm, d), row),
                  pl.BlockSpec((1,) + mod3.shape[1:], lambda i: (i // per_b, 0, 0)),
                  pl.BlockSpec(woa.shape, full), pl.BlockSpec(wob.shape, full),
                  pl.BlockSpec(wout.shape, full), pl.BlockSpec(g_ffn.shape, full),
                  pl.BlockSpec(wr.shape, full), pl.BlockSpec(br.shape, full)],
        out_specs=[pl.BlockSpec((tm, d), row), pl.BlockSpec((tm, d + LANES), row),
                   pl.BlockSpec((ROUTE_ROWS, tm), lambda i: (0, i))],
        out_shape=[jax.ShapeDtypeStruct((t, d), F32), jax.ShapeDtypeStruct((t, d + LANES), F32),
                   jax.ShapeDtypeStruct((ROUTE_ROWS, t), F32)],
        compiler_params=_cparams(("parallel",)),
        name="merge",
    )(oa, ob, ga, gb, x2, mod3, woa, wob, wout, g_ffn, wr, br)


def _plan_kernel(rt_ref, pos_ref, cnt_ref, carry_ref, off_ref, *, tc):
    phase, i = pl.program_id(0), pl.program_id(1)
    cls = rt_ref[ROUTE_CLS:ROUTE_CLS + 1, :]
    onehot = lax.broadcasted_iota(jnp.int32, (LANES, tc), 0).astype(F32) == cls
    per_class = jnp.sum(onehot.astype(F32), axis=1, keepdims=True)

    @pl.when((phase == 0) & (i == 0))
    def _():
        cnt_ref[...] = jnp.zeros_like(cnt_ref)

    @pl.when(phase == 0)
    def _():
        cnt_ref[...] += per_class

    @pl.when((phase == 1) & (i == 0))
    def _():
        padded = jnp.floor((cnt_ref[...] + (MOE_ROWS - 1)) * (1.0 / MOE_ROWS)) * MOE_ROWS
        hi = jnp.floor(padded * (1.0 / 256.0))
        digits = [jnp.broadcast_to(v, (LANES, LANES)).astype(BF16) for v in (hi, padded - 256.0 * hi)]
        below = (lax.broadcasted_iota(jnp.int32, (LANES, LANES), 1)
                 < lax.broadcasted_iota(jnp.int32, (LANES, LANES), 0)).astype(BF16)
        off = 256.0 * _dot(below, digits[0]) + _dot(below, digits[1])
        off_ref[...] = off[:, 0:1]
        carry_ref[...] = jnp.zeros_like(carry_ref)

    @pl.when(phase == 1)
    def _():
        earlier = (lax.broadcasted_iota(jnp.int32, (tc, tc), 0)
                   < lax.broadcasted_iota(jnp.int32, (tc, tc), 1)).astype(BF16)
        rank = _dot(onehot.astype(BF16), earlier)
        row = rank + (carry_ref[...] + off_ref[...])
        pos_ref[...] = jnp.sum(jnp.where(onehot, row, 0.0), axis=0, keepdims=True).astype(jnp.int32)
        carry_ref[...] += per_class


def _plan(route_t, tc):
    t = route_t.shape[1]
    return pl.pallas_call(
        functools.partial(_plan_kernel, tc=tc),
        grid=(2, t // tc),
        in_specs=[pl.BlockSpec((ROUTE_ROWS, tc), lambda ph, i: (0, i))],
        out_specs=[pl.BlockSpec((1, tc), lambda ph, i: (0, i * ph)),
                   pl.BlockSpec((LANES, 1), lambda ph, i: (0, 0))],
        out_shape=[jax.ShapeDtypeStruct((1, t), jnp.int32), jax.ShapeDtypeStruct((LANES, 1), F32)],
        scratch_shapes=[pltpu.VMEM((LANES, 1), F32), pltpu.VMEM((LANES, 1), F32)],
        compiler_params=_cparams(("arbitrary", "arbitrary")),
        name="plan",
    )(route_t)


def _tile_tables(counts, n_tiles):
    pairs = [(lo, hi) for lo in range(EXPERTS_PER_GROUP) for hi in range(lo + 1, EXPERTS_PER_GROUP)]
    exp_lo = np.array([g * EXPERTS_PER_GROUP + lo for g in range(N_GROUPS) for lo, _ in pairs], np.int32)
    exp_hi = np.array([g * EXPERTS_PER_GROUP + hi for g in range(N_GROUPS) for _, hi in pairs], np.int32)
    cnt = counts.reshape(-1)[:N_CLASSES].astype(jnp.int32)
    tiles_per_class = (cnt + (MOE_ROWS - 1)) // MOE_ROWS
    tile_end = jnp.cumsum(tiles_per_class)
    n_used = tile_end[-1]
    j = jnp.arange(n_tiles, dtype=jnp.int32)
    tile_cls = jnp.sum(tile_end[None, :] <= jnp.minimum(j, n_used - 1)[:, None], axis=1, dtype=jnp.int32)
    valid_end = (tile_end - tiles_per_class) * MOE_ROWS + cnt
    of_class = tile_cls[:, None] == jnp.arange(N_CLASSES, dtype=jnp.int32)[None, :]

    def per_tile(table):
        return jnp.sum(jnp.where(of_class, table[None, :], 0), axis=1, dtype=jnp.int32)

    units = MOE_ROWS // ZERO_ROWS
    unit_end = (jnp.arange(n_tiles * units, dtype=jnp.int32) + 1) * ZERO_ROWS
    zero_fill = (unit_end.reshape(n_tiles, units) > per_tile(valid_end)[:, None]).astype(jnp.int32).reshape(-1)
    return (per_tile(jnp.asarray(exp_lo)), per_tile(jnp.asarray(exp_hi)), n_used.reshape(1).astype(jnp.int32),
            zero_fill)


def _row_copy(src, src_row, dst, dst_row, n, sem):
    return pltpu.make_async_copy(src.at[pl.ds(src_row, n)], dst.at[pl.ds(dst_row, n)], sem)


def _dispatch_kernel(pos_ref, zf_ref, h_ref, hs_ref, zero_ref, sem, *, td, n_tiles):
    i = pl.program_id(0)

    @pl.when(i == 0)
    def _():
        zero_ref[...] = jnp.zeros_like(zero_ref)

        def fill(j, n):
            @pl.when(zf_ref[j] > 0)
            def _():
                _row_copy(zero_ref, 0, hs_ref, pl.multiple_of(j * ZERO_ROWS, ZERO_ROWS), ZERO_ROWS, sem).start()
            return n + zf_ref[j]

        n_fill = lax.fori_loop(0, n_tiles * (MOE_ROWS // ZERO_ROWS), fill, jnp.int32(0))

        def drain(_, carry):
            _row_copy(zero_ref, 0, hs_ref, 0, ZERO_ROWS, sem).wait()
            return carry

        lax.fori_loop(0, n_fill, drain, 0)

    base = i * td

    for r in range(td):
        _row_copy(h_ref, r, hs_ref, pos_ref[base + r], 1, sem).start()
    _row_copy(h_ref, 0, hs_ref, 0, td, sem).wait()


def _dispatch(h2x, pos, zero_fill, n_tiles, td):
    t, w = h2x.shape
    return pl.pallas_call(
        functools.partial(_dispatch_kernel, td=td, n_tiles=n_tiles),
        grid_spec=pltpu.PrefetchScalarGridSpec(
            num_scalar_prefetch=2,
            grid=(t // td,),
            in_specs=[pl.BlockSpec((td, w), lambda i, pos, zf: (i, 0))],
            out_specs=pl.BlockSpec(memory_space=pl.ANY),
            scratch_shapes=[pltpu.VMEM((ZERO_ROWS, w), F32), pltpu.SemaphoreType.DMA(())]),
        out_shape=jax.ShapeDtypeStruct((n_tiles * MOE_ROWS, w), F32),
        compiler_params=_cparams(("arbitrary",)),
        name="dispatch",
    )(pos, zero_fill, h2x)


def _moe_kernel(elo_ref, ehi_ref, nu_ref, hs_ref, wg_lo, wu_lo, wd_lo, wg_hi, wu_hi, wd_hi, y_ref):
    j = pl.program_id(0)
    d = y_ref.shape[1]

    @pl.when(j < nu_ref[0])
    def _():
        h = hs_ref[:, 0:d].astype(BF16)
        ups = [(_dot(h, wg[0]), _dot(h, wu[0])) for wg, wu in ((wg_lo, wu_lo), (wg_hi, wu_hi))]
        hids = [((a * jax.nn.sigmoid(a)) * u * hs_ref[:, d + lane:d + lane + 1]).astype(BF16)
                for (a, u), lane in zip(ups, (ROUTE_W_LO, ROUTE_W_HI))]
        y_ref[...] = _dot(hids[0], wd_lo[0]) + _dot(hids[1], wd_hi[0])

    @pl.when(j >= nu_ref[0])
    def _():
        y_ref[...] = jnp.zeros_like(y_ref)


def _moe(hs, exp_lo, exp_hi, n_used, wg, wu, wd):
    n_tiles = hs.shape[0] // MOE_ROWS
    _, d, ff = wg.shape
    up = lambda sel: pl.BlockSpec((1, d, ff), lambda j, elo, ehi, nu: ((elo, ehi)[sel][j], 0, 0))
    down = lambda sel: pl.BlockSpec((1, ff, d), lambda j, elo, ehi, nu: ((elo, ehi)[sel][j], 0, 0))
    return pl.pallas_call(
        _moe_kernel,
        grid_spec=pltpu.PrefetchScalarGridSpec(
            num_scalar_prefetch=3,
            grid=(n_tiles,),
            in_specs=[pl.BlockSpec((MOE_ROWS, hs.shape[1]),
                                   lambda j, elo, ehi, nu: (jnp.minimum(j, nu[0] - 1), 0)),
                      up(0), up(0), down(0), up(1), up(1), down(1)],
            out_specs=pl.BlockSpec((MOE_ROWS, d), lambda j, elo, ehi, nu: (j, 0))),
        out_shape=jax.ShapeDtypeStruct((n_tiles * MOE_ROWS, d), F32),
        compiler_params=_cparams(("arbitrary",)),
        name="moe",
    )(exp_lo, exp_hi, n_used, hs, wg, wu, wd, wg, wu, wd)


def _final_kernel(pos_ref, x1_ref, mod_ref, gfin_ref, y_ref, o_ref, rows_ref, sem, *, tf):
    i, n = pl.program_id(0), pl.num_programs(0)

    def fetch(step, slot):
        for r in range(tf):
            _row_copy(y_ref, pos_ref[step * tf + r], rows_ref.at[slot], r, 1, sem.at[slot]).start()

    @pl.when(i == 0)
    def _():
        fetch(0, 0)

    @pl.when(i + 1 < n)
    def _():
        fetch(i + 1, (i + 1) % 2)

    slot = i % 2
    _row_copy(y_ref, 0, rows_ref.at[slot], 0, tf, sem.at[slot]).wait()
    x2 = x1_ref[...] + mod_ref[0, 5:6, :] * rows_ref[slot]
    o_ref[...] = _rms(x2, gfin_ref[...])


def _final(y, pos, x1, mod3, g_final, seq, tf):
    t, d = x1.shape
    per_b = seq // tf
    return pl.pallas_call(
        functools.partial(_final_kernel, tf=tf),
        grid_spec=pltpu.PrefetchScalarGridSpec(
            num_scalar_prefetch=1,
            grid=(t // tf,),
            in_specs=[pl.BlockSpec((tf, d), lambda i, pos: (i, 0)),
                      pl.BlockSpec((1,) + mod3.shape[1:], lambda i, pos: (i // per_b, 0, 0)),
                      pl.BlockSpec((1, d), lambda i, pos: (0, 0)),
                      pl.BlockSpec(memory_space=pl.ANY)],
            out_specs=pl.BlockSpec((tf, d), lambda i, pos: (i, 0)),
            scratch_shapes=[pltpu.VMEM((2, tf, d), F32), pltpu.SemaphoreType.DMA((2,))]),
        out_shape=jax.ShapeDtypeStruct((t, d), F32),
        compiler_params=_cparams(("arbitrary",)),
        name="final",
    )(pos, x1, mod3, g_final, y)


def _layout_weights(w_in, w_uq, w_uk, w_uv, heads, q_lora, kv_lora, d):
    hw = heads * HEAD_DIM
    cuts = np.cumsum([q_lora, kv_lora, ROPE_DIM, hw, hw, hw, d, d])
    w_qlat, w_kvlat, w_kr, w_qb, w_kb, w_vb, w_ga, w_gb = jnp.split(w_in, [int(v) for v in cuts[:-1]], axis=1)
    swap = np.concatenate([np.arange(ROPE_DIM // 2, ROPE_DIM), np.arange(ROPE_DIM // 2)])
    w_kr_sw = w_kr[:, swap]
    kr_blk = jnp.concatenate([w_kr, w_kr, w_kr_sw, w_kr_sw], axis=1)
    band_scale = HEAD_DIM ** -0.5 * LOG2_E
    wbig = jnp.concatenate([w_qlat, w_kvlat, kr_blk, w_qb * band_scale, w_kb, w_ga, w_gb], axis=1)

    def transposed_values(w, k):
        return jnp.pad(w.reshape(k, heads, HEAD_DIM).transpose(1, 2, 0),
                       ((0, 0), (0, VT_ROWS - HEAD_DIM), (0, 0))).reshape(heads * VT_ROWS, k)

    qk = HEAD_DIM + ROPE_DIM
    wq3 = w_uq.reshape(q_lora, heads, qk) * (qk ** -0.5 * LOG2_E)
    nope = wq3[:, :, :HEAD_DIM].reshape(q_lora, hw)
    rope = wq3[:, :, HEAD_DIM:]
    pad = jnp.zeros((q_lora, heads // 2, LANES - 2 * ROPE_DIM), w_uq.dtype)

    def lay(r):
        return jnp.concatenate([r.reshape(q_lora, heads // 2, 2 * ROPE_DIM), pad], axis=2).reshape(q_lora, hw)

    wq = jnp.concatenate([nope, lay(rope), lay(rope[:, :, swap])], axis=1)
    return (wbig.astype(BF16), wq.astype(BF16), w_uk.astype(BF16), transposed_values(w_uv, kv_lora).astype(BF16),
            transposed_values(w_vb, d).astype(BF16))


def kernel(x, c, positions, w_ada, b_ada, g_mix, w_in, g_q, w_uq, g_kv, w_uk, w_uv, rel_bias, w_oa, w_ob,
           w_out, g_ffn, w_rg, b_rg, w_re, b_re, w_gate, w_up, w_down, g_final):
    bsz, seq, d = x.shape
    depth = w_ada.shape[0]
    t = bsz * seq
    heads = rel_bias.shape[1]
    q_lora, kv_lora = g_q.shape[1], g_kv.shape[1]
    tq = 256
    tq_mla = 512
    nblk = -(-LEFT_CHUNKS * CHUNK // tq) + 1
    assert seq % tq == 0 and seq % tq_mla == 0 and tq % CHUNK == 0 and heads % 2 == 0
    assert ROUTE_ROWS == EXPERTS_PER_GROUP and N_GROUPS <= EXPERTS_PER_GROUP

    x2 = x.reshape(t, d)
    pos2 = positions.reshape(bsz, 1, seq)
    out = x2
    for l in range(depth):
        mod3 = _ada(c, w_ada[l], b_ada[l]).reshape(bsz, -1, d)
        wbig, wq, wkv, wvt, wvbt = _layout_weights(w_in[l], w_uq[l], w_uk[l], w_uv[l], heads, q_lora, kv_lora, d)
        qa, ka, vt, qb, kb, vtb, ga, gb = _proj(
            x2, pos2, mod3, g_mix[l].reshape(1, d), wbig, g_q[l].reshape(1, -1), wq, g_kv[l].reshape(1, -1), wkv,
            wvt, wvbt, seq, tm=512)
        oa = _mla(qa, ka, vt, bsz, seq, tq_mla)
        ob = _band(qb, kb, vtb, _band_bias(rel_bias[l], tq, nblk), bsz, seq, tq, nblk, tiles_per_step=4)

        pad_g = EXPERTS_PER_GROUP - N_GROUPS
        n_route = (N_GROUPS + 1) * EXPERTS_PER_GROUP
        w_r = jnp.concatenate([w_rg[l], jnp.zeros((d, pad_g), F32), w_re[l],
                               jnp.zeros((d, LANES - n_route), F32)], axis=1).T
        b_r = jnp.concatenate([b_rg[l], jnp.zeros((pad_g,), F32), b_re[l],
                               jnp.zeros((LANES - n_route,), F32)]).reshape(LANES, 1)
        w_r_hi = w_r.astype(BF16)
        w_r_lo = (w_r - w_r_hi.astype(F32)).astype(BF16)
        x1, h2x, route_t = _merge(oa, ob, ga, gb, x2, mod3, w_oa[l].astype(BF16), w_ob[l].astype(BF16),
                                  w_out[l].astype(BF16), g_ffn[l].reshape(1, d),
                                  jnp.concatenate([w_r_hi, w_r_lo], axis=0), b_r, seq, tm=512)
        n_tiles = t // MOE_ROWS + N_CLASSES
        pos, counts = _plan(route_t, tc=1024)
        pos = pos.reshape(t)
        exp_lo, exp_hi, n_used, zero_fill = _tile_tables(counts, n_tiles)
        hs = _dispatch(h2x, pos, zero_fill, n_tiles, td=2048)
        y = _moe(hs, exp_lo, exp_hi, n_used, w_gate[l].astype(BF16), w_up[l].astype(BF16), w_down[l].astype(BF16))
        assert l == depth - 1, "multi-layer stacks need an un-normalised residual output"
        out = _final(y, pos, x1, mod3, g_final.reshape(1, d), seq, tf=512)
    return out.reshape(bsz, seq, d)
```

```python
import functools

import numpy as np
import jax
import jax.numpy as jnp
from jax import lax
from jax.experimental import pallas as pl
from jax.experimental.pallas import tpu as pltpu

F32 = jnp.float32
BF16 = jnp.bfloat16

NORM_EPS = 1e-6
NEG_INF = -1e30
ROPE_BASE = 10000.0
CHUNK = 64
LEFT_CHUNKS = 8
MAX_REL = 256
N_GROUPS = 4
EXPERTS_PER_GROUP = 8
PAIRS_PER_GROUP = EXPERTS_PER_GROUP * (EXPERTS_PER_GROUP - 1) // 2
N_CLASSES = N_GROUPS * PAIRS_PER_GROUP
MOE_ROWS = 256
ZERO_ROWS = 64
ROUTE_CLS, ROUTE_W_LO, ROUTE_W_HI = 0, 1, 2
ROUTE_ROWS = 8
LANES = 128
HEAD_DIM = 64
ROPE_DIM = 32
VT_ROWS = HEAD_DIM + 16
LOG2_E = 1.4426950408889634

VMEM_LIMIT = 56 * 1024 * 1024

TM_PROJ = 512
TM_MERGE = 512
TQ_MLA = 512
TQ_BAND = 256
BAND_TILES_PER_STEP = 4
TC_PLAN = 1024
TD_DISPATCH = 2048
TF_FINAL = 512


def _cparams(sem):
    return pltpu.CompilerParams(dimension_semantics=sem, vmem_limit_bytes=VMEM_LIMIT)


def _dot(a, b):
    return jnp.dot(a, b, preferred_element_type=F32)


def _dot_nt(a, b):
    return lax.dot_general(a, b, (((1,), (1,)), ((), ())), preferred_element_type=F32)


def _rms(x, g):
    return x * lax.rsqrt(jnp.mean(x * x, axis=-1, keepdims=True) + NORM_EPS) * g


def _ada_kernel(c_ref, w_ref, b_ref, o_ref):
    c = c_ref[...]
    o_ref[...] = _dot(c * jax.nn.sigmoid(c), w_ref[...]) + b_ref[...]


def _ada(c, w_ada, b_ada):
    bsz, d = c.shape
    n = w_ada.shape[1]
    return pl.pallas_call(
        _ada_kernel,
        grid=(n // d,),
        in_specs=[pl.BlockSpec((bsz, d), lambda j: (0, 0)),
                  pl.BlockSpec((d, d), lambda j: (0, j)),
                  pl.BlockSpec((1, d), lambda j: (0, j))],
        out_specs=pl.BlockSpec((bsz, d), lambda j: (0, j)),
        out_shape=jax.ShapeDtypeStruct((bsz, n), F32),
        compiler_params=_cparams(("arbitrary",)),
        name="ada",
    )(c, w_ada, b_ada.reshape(1, n))


def _proj_kernel(x_ref, pos_ref, mod_ref, gmix_ref, wbig_ref, gq_ref, wq_ref, gkv_ref, wkv_ref,
                 wvt_ref, wvbt_ref, one_ref, inv_ref, sgn_ref,
                 qa_ref, ka_ref, vt_ref, qb_ref, kb_ref, vtb_ref, ga_ref, gb_ref, *, q_lora, kv_lora):
    x = x_ref[...]
    h = _rms(x, gmix_ref[...]) * (1.0 + mod_ref[0, 1:2, :]) + mod_ref[0, 0:1, :]
    hb = h.astype(BF16)

    c0 = q_lora + kv_lora + LANES
    head = _dot(hb, wbig_ref[:, 0:c0])
    q_lat = head[:, 0:q_lora]
    kv_lat = head[:, q_lora:q_lora + kv_lora]
    kr_blk = head[:, q_lora + kv_lora:c0]
    hw = qb_ref.shape[1]
    d = x.shape[1]
    g0 = c0 + 2 * hw
    ga_ref[...] = jax.nn.sigmoid(_dot(hb, wbig_ref[:, g0:g0 + d])).astype(BF16)

    ang = inv_ref[...] * pos_ref[0].astype(F32)
    live = lax.broadcasted_iota(jnp.int32, (LANES, 1), 0) < 2 * ROPE_DIM
    cos_t = jnp.where(live, jnp.cos(ang), 0.0).T
    sin_t = jnp.where(live, jnp.sin(ang) * sgn_ref[...], 0.0).T

    qn = _rms(q_lat, gq_ref[...]).astype(BF16)
    q_all = _dot(qn, wq_ref[...])
    kvn = _rms(kv_lat, gkv_ref[...]).astype(BF16)
    k_nope = _dot(kvn, wkv_ref[...])
    vt_ref[0] = (_dot_nt(wvt_ref[...], kvn) + one_ref[...]).astype(BF16)
    gb_ref[...] = jax.nn.sigmoid(_dot(hb, wbig_ref[:, g0 + d:g0 + 2 * d])).astype(BF16)
    k_rot = (kr_blk * cos_t + pltpu.roll(kr_blk, LANES // 2, 1) * sin_t).astype(BF16)
    for p in range(hw // LANES):
        lo, hi = p * LANES, (p + 1) * LANES
        qa_ref[:, 2 * lo:2 * lo + LANES] = q_all[:, lo:hi].astype(BF16)
        qa_ref[:, 2 * lo + LANES:2 * hi] = (
            q_all[:, hw + lo:hw + hi] * cos_t + q_all[:, 2 * hw + lo:2 * hw + hi] * sin_t).astype(BF16)
        ka_ref[:, 2 * lo:2 * lo + LANES] = k_nope[:, lo:hi].astype(BF16)
        ka_ref[:, 2 * lo + LANES:2 * hi] = k_rot

    qb_ref[...] = _dot(hb, wbig_ref[:, c0:c0 + hw]).astype(BF16)
    kb_ref[...] = _dot(hb, wbig_ref[:, c0 + hw:c0 + 2 * hw]).astype(BF16)
    vtb_ref[0] = (_dot_nt(wvbt_ref[...], hb) + one_ref[...]).astype(BF16)


def _proj(x2, pos2, mod3, g_mix, wbig, g_q, wq, g_kv, wkv, wvt, wvbt, seq, tm):
    t, d = x2.shape
    q_lora, kv_lora = g_q.shape[1], g_kv.shape[1]
    hw = wkv.shape[1]
    per_b = seq // tm
    row = lambda i: (i, 0)
    full = lambda i: (0, 0)
    inv = ROPE_BASE ** (-(np.arange(LANES) % (ROPE_DIM // 2)).astype(np.float32) / (ROPE_DIM // 2))
    sgn = np.where((np.arange(LANES) % ROPE_DIM) < ROPE_DIM // 2, -1.0, 1.0).astype(np.float32)
    ones_col = (np.arange(wvt.shape[0]) % VT_ROWS == HEAD_DIM).astype(np.float32).reshape(-1, 1)
    v_t = jax.ShapeDtypeStruct((t // seq, wvt.shape[0], seq), BF16)
    outs = [jax.ShapeDtypeStruct((t, 2 * hw), BF16), jax.ShapeDtypeStruct((t, 2 * hw), BF16), v_t,
            jax.ShapeDtypeStruct((t, hw), BF16), jax.ShapeDtypeStruct((t, hw), BF16), v_t,
            jax.ShapeDtypeStruct((t, d), BF16), jax.ShapeDtypeStruct((t, d), BF16)]
    v_t_spec = pl.BlockSpec((1, wvt.shape[0], tm), lambda i: (i // per_b, 0, i % per_b))
    out_specs = [v_t_spec if o is v_t else pl.BlockSpec((tm, o.shape[1]), row) for o in outs]
    return pl.pallas_call(
        functools.partial(_proj_kernel, q_lora=q_lora, kv_lora=kv_lora),
        grid=(t // tm,),
        in_specs=[pl.BlockSpec((tm, d), row),
                  pl.BlockSpec((1, 1, tm), lambda i: (i // per_b, 0, i % per_b)),
                  pl.BlockSpec((1,) + mod3.shape[1:], lambda i: (i // per_b, 0, 0)),
                  pl.BlockSpec(g_mix.shape, full),
                  pl.BlockSpec(wbig.shape, full),
                  pl.BlockSpec(g_q.shape, full),
                  pl.BlockSpec(wq.shape, full),
                  pl.BlockSpec(g_kv.shape, full),
                  pl.BlockSpec(wkv.shape, full),
                  pl.BlockSpec(wvt.shape, full),
                  pl.BlockSpec(wvbt.shape, full),
                  pl.BlockSpec(ones_col.shape, full),
                  pl.BlockSpec((LANES, 1), full),
                  pl.BlockSpec((LANES, 1), full)],
        out_specs=out_specs,
        out_shape=outs,
        compiler_params=_cparams(("parallel",)),
        name="proj",
    )(x2, pos2, mod3, g_mix, wbig, g_q, wq, g_kv, wkv, wvt, wvbt, jnp.asarray(ones_col),
      jnp.asarray(inv).reshape(LANES, 1), jnp.asarray(sgn).reshape(LANES, 1))


def _mla_kernel(q_ref, k_ref, vt_ref, o_ref, sa, sb, *, tq):
    i, nq = pl.program_id(2), pl.num_programs(2)
    lane = lax.broadcasted_iota(jnp.int32, (1, 2 * LANES), 1)
    first = (lane < HEAD_DIM) | ((lane >= LANES) & (lane < LANES + ROPE_DIM))
    second = ((lane >= HEAD_DIM) & (lane < LANES)) | ((lane >= LANES + ROPE_DIM) & (lane < LANES + 2 * ROPE_DIM))

    def head_queries(tile):
        q = q_ref[pl.ds(pl.multiple_of(tile * tq, tq), tq), :]
        zero = jnp.zeros_like(q)
        return (jnp.where(first, q, zero), jnp.where(second, q, zero))

    qs = head_queries(i)

    def scores(queries, blk, buf):
        k = k_ref[pl.ds(pl.multiple_of(blk * tq, tq), tq), :]
        for h, qh in enumerate(queries):
            buf[h] = _dot_nt(k, qh)

    def consume(blk, buf, state, mask):
        start = pl.multiple_of(blk * tq, tq)
        ps, ms, alphas = [], [], []
        for h, (m, _) in enumerate(state):
            st = buf[h]
            if mask is not None:
                st = jnp.where(mask, st, NEG_INF)
            m_new = jnp.maximum(m, jnp.max(st, axis=0, keepdims=True))
            ms.append(m_new)
            alphas.append(jnp.exp2(m - m_new))
            ps.append(jnp.exp2(st - m_new).astype(BF16))
        new = []
        for h, (p, m_new, alpha, (_, acc)) in enumerate(zip(ps, ms, alphas, state)):
            vt = vt_ref[0, h * VT_ROWS:(h + 1) * VT_ROWS, pl.ds(start, tq)]
            new.append((m_new, alpha * acc + _dot(vt, p)))
        return tuple(new)

    causal = (lax.broadcasted_iota(jnp.int32, (tq, tq), 0) // CHUNK
              <= lax.broadcasted_iota(jnp.int32, (tq, tq), 1) // CHUNK)
    following = jnp.minimum(i + 1, nq - 1)

    def tile(sx, sy):
        @pl.when(i == 0)
        def _():
            scores(qs, 0, sx)

        def two_blocks(jj, state):
            scores(qs, 2 * jj + 1, sy)
            state = consume(2 * jj, sx, state, None)
            scores(qs, 2 * jj + 2, sx)
            return consume(2 * jj + 1, sy, state, None)

        def even_tail(state):
            scores(head_queries(following), 0, sy)
            return consume(i, sx, state, causal)

        def odd_tail(state):
            scores(qs, i, sy)
            state = consume(i - 1, sx, state, None)
            scores(head_queries(following), 0, sx)
            return consume(i, sy, state, causal)

        init = tuple((jnp.full((1, tq), NEG_INF, F32), jnp.zeros((VT_ROWS, tq), F32)) for _ in range(2))
        state = lax.fori_loop(0, i // 4, lambda jj, st: two_blocks(2 * jj + 1, two_blocks(2 * jj, st)), init)
        state = lax.fori_loop(2 * (i // 4), i // 2, two_blocks, state)
        (_, a0), (_, a1) = lax.cond(i % 2 == 1, odd_tail, even_tail, state)
        out_t = jnp.concatenate([a[0:HEAD_DIM] * (1.0 / a[HEAD_DIM:HEAD_DIM + 1]) for a in (a0, a1)], axis=0)
        o_ref[...] = out_t.T.astype(o_ref.dtype)

    swapped = ((i + 1) // 2) % 2

    @pl.when(swapped == 0)
    def _():
        tile(sa, sb)

    @pl.when(swapped == 1)
    def _():
        tile(sb, sa)


def _mla(qa, ka, vt, bsz, seq, tq):
    t = qa.shape[0]
    pairs = qa.shape[1] // (2 * LANES)
    nq = seq // tq
    return pl.pallas_call(
        functools.partial(_mla_kernel, tq=tq),
        grid=(bsz, pairs, nq),
        in_specs=[pl.BlockSpec((seq, 2 * LANES), lambda b, p, i: (b, p)),
                  pl.BlockSpec((seq, 2 * LANES), lambda b, p, i: (b, p)),
                  pl.BlockSpec((1, 2 * VT_ROWS, seq), lambda b, p, i: (b, p, 0))],
        out_specs=pl.BlockSpec((tq, LANES), lambda b, p, i: (b * nq + i, p)),
        out_shape=jax.ShapeDtypeStruct((t, pairs * LANES), BF16),
        scratch_shapes=[pltpu.VMEM((2, tq, tq), F32)] * 2,
        compiler_params=_cparams(("parallel", "parallel", "arbitrary")),
        name="mla",
    )(qa, ka, vt)


def _band_kernel(q_ref, k_ref, vt_ref, bias_ref, o_ref, *, tq, nblk):
    lane = lax.broadcasted_iota(jnp.int32, (1, LANES), 1)

    def scores(n, edge):
        tile = first + n
        q = q_ref[n * tq:(n + 1) * tq, :]
        zero = jnp.zeros_like(q)
        qs = (jnp.where(lane < HEAD_DIM, q, zero), jnp.where(lane >= HEAD_DIM, q, zero))
        blks = [tile - (nblk - 1) + j for j in range(nblk)]
        starts = [pl.multiple_of((jnp.maximum(blk, 0) if edge else blk) * tq, tq) for blk in blks]
        ks = [k_ref[pl.ds(s, tq), :] for s in starts]
        sts = [[_dot_nt(ks[j], qh) + bias_ref[0, h, j * tq:(j + 1) * tq, :] for j in range(nblk)]
               for h, qh in enumerate(qs)]
        if edge:
            sts = [[jnp.where(blks[j] >= 0, st, NEG_INF) for j, st in enumerate(row)] for row in sts]
        return starts, sts

    def finish(starts, sts):
        outs = []
        for h in range(2):
            m = functools.reduce(jnp.maximum, [jnp.max(st, axis=0, keepdims=True) for st in sts[h]])
            acc = functools.reduce(jnp.add, [
                _dot(vt_ref[0, h * VT_ROWS:(h + 1) * VT_ROWS, pl.ds(starts[j], tq)], jnp.exp2(st - m).astype(BF16))
                for j, st in enumerate(sts[h])])
            outs.append(acc[0:HEAD_DIM] * (1.0 / acc[HEAD_DIM:HEAD_DIM + 1]))
        return jnp.concatenate(outs, axis=0).T

    tiles_per_step = q_ref.shape[0] // tq
    assert tiles_per_step >= nblk - 1
    first = pl.program_id(2) * tiles_per_step

    def tiles(edge):
        pending = scores(0, edge)
        for n in range(tiles_per_step):
            upcoming = scores(n + 1, edge) if n + 1 < tiles_per_step else None
            o_ref[n * tq:(n + 1) * tq, :] = finish(*pending).astype(o_ref.dtype)
            pending = upcoming

    for edge in (True, False):
        @pl.when((pl.program_id(2) == 0) == edge)
        def _():
            tiles(edge)


def _band_bias(rel_table, tq, nblk):
    nk = nblk * tq
    ring = nk + tq
    d = np.arange(ring)
    key_minus_query = np.where(d < nk, d, d - ring)
    idx = np.clip((nblk - 1) * tq - key_minus_query, -MAX_REL, MAX_REL) + MAX_REL
    per_offset = rel_table.astype(F32)[:, idx]
    heads = rel_table.shape[0]
    toeplitz = jnp.tile(per_offset, (1, tq))[:, :tq * (ring - 1)].reshape(heads, tq, ring - 1)[:, :, :nk]
    qc = np.arange(tq)[:, None] // CHUNK
    kc = np.arange(nk)[None, :] // CHUNK - ((nblk - 1) * tq // CHUNK - LEFT_CHUNKS)
    band = (kc >= qc) & (kc <= qc + LEFT_CHUNKS)
    bias = jnp.where(band[None], toeplitz * LOG2_E, NEG_INF)
    return bias.reshape(heads // 2, 2, tq, nk).swapaxes(2, 3)


def _band(qb, kb, vtb, bias, bsz, seq, tq, nblk, tiles_per_step):
    t, hw = qb.shape
    pairs = hw // LANES
    rows = tq * tiles_per_step
    nq = seq // rows
    return pl.pallas_call(
        functools.partial(_band_kernel, tq=tq, nblk=nblk),
        grid=(bsz, pairs, nq),
        in_specs=[pl.BlockSpec((rows, LANES), lambda b, p, i: (b * nq + i, p)),
                  pl.BlockSpec((seq, LANES), lambda b, p, i: (b, p)),
                  pl.BlockSpec((1, 2 * VT_ROWS, seq), lambda b, p, i: (b, p, 0)),
                  pl.BlockSpec((1,) + bias.shape[1:], lambda b, p, i: (p, 0, 0, 0))],
        out_specs=pl.BlockSpec((rows, LANES), lambda b, p, i: (b * nq + i, p)),
        out_shape=jax.ShapeDtypeStruct((t, hw), BF16),
        compiler_params=_cparams(("parallel", "parallel", "arbitrary")),
        name="band",
    )(qb, kb, vtb, bias)


def _merge_kernel(oa_ref, ob_ref, ga_ref, gb_ref, x_ref, mod_ref, woa_ref, wob_ref, wout_ref, gffn_ref,
                  wr_ref, br_ref, x1_ref, h2x_ref, rt_ref):
    merged = (ga_ref[...].astype(F32) * _dot(oa_ref[...], woa_ref[...])
              + gb_ref[...].astype(F32) * _dot(ob_ref[...], wob_ref[...]))
    mix = _dot(merged.astype(BF16), wout_ref[...])
    x1 = x_ref[...] + mod_ref[0, 2:3, :] * mix
    x1_ref[...] = x1
    h2 = _rms(x1, gffn_ref[...]) * (1.0 + mod_ref[0, 4:5, :]) + mod_ref[0, 3:4, :]

    h_hi = h2.astype(BF16)
    h_lo = (h2 - h_hi.astype(F32)).astype(BF16)
    r_hi = _dot_nt(wr_ref[...], h_hi)
    logits = (r_hi[0:LANES] + _dot_nt(wr_ref[0:LANES, :], h_lo) + r_hi[LANES:2 * LANES]) + br_ref[...]
    blk = EXPERTS_PER_GROUP
    sub = lax.broadcasted_iota(jnp.int32, (blk, logits.shape[1]), 0)
    far = jnp.int32(blk)
    is_g = sub < N_GROUPS
    gl = jnp.where(is_g, logits[0:blk], NEG_INF)
    g_max = jnp.max(gl, axis=0, keepdims=True)
    g_w = 1.0 / jnp.sum(jnp.where(is_g, jnp.exp(gl - g_max), 0.0), axis=0, keepdims=True)
    g_idx = jnp.min(jnp.where(gl == g_max, sub, far), axis=0, keepdims=True)
    el = logits[blk:2 * blk]
    for g in range(1, N_GROUPS):
        el = jnp.where(g_idx == g, logits[(g + 1) * blk:(g + 2) * blk], el)
    e1 = jnp.max(el, axis=0, keepdims=True)
    i1 = jnp.min(jnp.where(el == e1, sub, far), axis=0, keepdims=True)
    el2 = jnp.where(sub == i1, NEG_INF, el)
    e2 = jnp.max(el2, axis=0, keepdims=True)
    i2 = jnp.min(jnp.where(el2 == e2, sub, far), axis=0, keepdims=True)
    ratio = jnp.exp(e2 - e1)
    w1 = g_w / (1.0 + ratio)
    w2 = g_w * ratio / (1.0 + ratio)
    e_lo, e_hi = jnp.minimum(i1, i2), jnp.maximum(i1, i2)
    cls = (g_idx * PAIRS_PER_GROUP + jnp.right_shift(e_lo * (2 * EXPERTS_PER_GROUP - 1 - e_lo), 1)
           + (e_hi - e_lo - 1))
    first_is_lo = i1 < i2
    w_lo, w_hi = jnp.where(first_is_lo, w1, w2), jnp.where(first_is_lo, w2, w1)
    route_t = (jnp.where(sub == ROUTE_CLS, cls.astype(F32), 0.0) + jnp.where(sub == ROUTE_W_LO, w_lo, 0.0)
               + jnp.where(sub == ROUTE_W_HI, w_hi, 0.0))
    rt_ref[...] = route_t
    d = h2.shape[1]
    h2x_ref[:, 0:d] = h2
    h2x_ref[:, d:d + LANES] = jnp.concatenate(
        [route_t, jnp.zeros((LANES - ROUTE_ROWS, route_t.shape[1]), F32)], axis=0).T


def _merge(oa, ob, ga, gb, x2, mod3, woa, wob, wout, g_ffn, wr, br, seq, tm):
    t, d = x2.shape
    hw = oa.shape[1]
    per_b = seq // tm
    row = lambda i: (i, 0)
    full = lambda i: (0, 0)
    return pl.pallas_call(
        _merge_kernel,
        grid=(t // tm,),
        in_specs=[pl.BlockSpec((tm, hw), row), pl.BlockSpec((tm, hw), row),
                  pl.BlockSpec((tm, d), row), pl.BlockSpec((tm, d), row),
                  pl.BlockSpec((tm, d), row),
                  pl.BlockSpec((1,) + mod3.shape[1:], lambda i: (i // per_b, 0, 0)),
                  pl.BlockSpec(woa.shape, full), pl.BlockSpec(wob.shape, full),
                  pl.BlockSpec(wout.shape, full), pl.BlockSpec(g_ffn.shape, full),
                  pl.BlockSpec(wr.shape, full), pl.BlockSpec(br.shape, full)],
        out_specs=[pl.BlockSpec((tm, d), row), pl.BlockSpec((tm, d + LANES), row),
                   pl.BlockSpec((ROUTE_ROWS, tm), lambda i: (0, i))],
        out_shape=[jax.ShapeDtypeStruct((t, d), F32), jax.ShapeDtypeStruct((t, d + LANES), F32),
                   jax.ShapeDtypeStruct((ROUTE_ROWS, t), F32)],
        compiler_params=_cparams(("parallel",)),
        name="merge",
    )(oa, ob, ga, gb, x2, mod3, woa, wob, wout, g_ffn, wr, br)


def _plan_kernel(rt_ref, pos_ref, cnt_ref, carry_ref, off_ref, *, tc):
    phase, i = pl.program_id(0), pl.program_id(1)
    cls = rt_ref[ROUTE_CLS:ROUTE_CLS + 1, :]
    onehot = lax.broadcasted_iota(jnp.int32, (LANES, tc), 0).astype(F32) == cls
    per_class = jnp.sum(onehot.astype(F32), axis=1, keepdims=True)

    @pl.when((phase == 0) & (i == 0))
    def _():
        cnt_ref[...] = jnp.zeros_like(cnt_ref)

    @pl.when(phase == 0)
    def _():
        cnt_ref[...] += per_class

    @pl.when((phase == 1) & (i == 0))
    def _():
        padded = jnp.floor((cnt_ref[...] + (MOE_ROWS - 1)) * (1.0 / MOE_ROWS)) * MOE_ROWS
        hi = jnp.floor(padded * (1.0 / 256.0))
        digits = [jnp.broadcast_to(v, (LANES, LANES)).astype(BF16) for v in (hi, padded - 256.0 * hi)]
        below = (lax.broadcasted_iota(jnp.int32, (LANES, LANES), 1)
                 < lax.broadcasted_iota(jnp.int32, (LANES, LANES), 0)).astype(BF16)
        off = 256.0 * _dot(below, digits[0]) + _dot(below, digits[1])
        off_ref[...] = off[:, 0:1]
        carry_ref[...] = jnp.zeros_like(carry_ref)

    @pl.when(phase == 1)
    def _():
        earlier = (lax.broadcasted_iota(jnp.int32, (tc, tc), 0)
                   < lax.broadcasted_iota(jnp.int32, (tc, tc), 1)).astype(BF16)
        rank = _dot(onehot.astype(BF16), earlier)
        row = rank + (carry_ref[...] + off_ref[...])
        pos_ref[...] = jnp.sum(jnp.where(onehot, row, 0.0), axis=0, keepdims=True).astype(jnp.int32)
        carry_ref[...] += per_class


def _plan(route_t, tc):
    t = route_t.shape[1]
    return pl.pallas_call(
        functools.partial(_plan_kernel, tc=tc),
        grid=(2, t // tc),
        in_specs=[pl.BlockSpec((ROUTE_ROWS, tc), lambda ph, i: (0, i))],
        out_specs=[pl.BlockSpec((1, tc), lambda ph, i: (0, i * ph)),
                   pl.BlockSpec((LANES, 1), lambda ph, i: (0, 0))],
        out_shape=[jax.ShapeDtypeStruct((1, t), jnp.int32), jax.ShapeDtypeStruct((LANES, 1), F32)],
        scratch_shapes=[pltpu.VMEM((LANES, 1), F32), pltpu.VMEM((LANES, 1), F32)],
        compiler_params=_cparams(("arbitrary", "arbitrary")),
        name="plan",
    )(route_t)


def _tile_tables(counts, n_tiles):
    pairs = [(lo, hi) for lo in range(EXPERTS_PER_GROUP) for hi in range(lo + 1, EXPERTS_PER_GROUP)]
    exp_lo = np.array([g * EXPERTS_PER_GROUP + lo for g in range(N_GROUPS) for lo, _ in pairs], np.int32)
    exp_hi = np.array([g * EXPERTS_PER_GROUP + hi for g in range(N_GROUPS) for _, hi in pairs], np.int32)
    cnt = counts.reshape(-1)[:N_CLASSES].astype(jnp.int32)
    tiles_per_class = (cnt + (MOE_ROWS - 1)) // MOE_ROWS
    tile_end = jnp.cumsum(tiles_per_class)
    n_used = tile_end[-1]
    j = jnp.arange(n_tiles, dtype=jnp.int32)
    tile_cls = jnp.sum(tile_end[None, :] <= jnp.minimum(j, n_used - 1)[:, None], axis=1, dtype=jnp.int32)
    valid_end = (tile_end - tiles_per_class) * MOE_ROWS + cnt
    of_class = tile_cls[:, None] == jnp.arange(N_CLASSES, dtype=jnp.int32)[None, :]

    def per_tile(table):
        return jnp.sum(jnp.where(of_class, table[None, :], 0), axis=1, dtype=jnp.int32)

    units = MOE_ROWS // ZERO_ROWS
    unit_end = (jnp.arange(n_tiles * units, dtype=jnp.int32) + 1) * ZERO_ROWS
    zero_fill = (unit_end.reshape(n_tiles, units) > per_tile(valid_end)[:, None]).astype(jnp.int32).reshape(-1)
    return (per_tile(jnp.asarray(exp_lo)), per_tile(jnp.asarray(exp_hi)), n_used.reshape(1).astype(jnp.int32),
            zero_fill)


def _row_copy(src, src_row, dst, dst_row, n, sem):
    return pltpu.make_async_copy(src.at[pl.ds(src_row, n)], dst.at[pl.ds(dst_row, n)], sem)


def _dispatch_kernel(pos_ref, zf_ref, h_ref, hs_ref, zero_ref, sem, *, td, n_tiles):
    i = pl.program_id(0)

    @pl.when(i == 0)
    def _():
        zero_ref[...] = jnp.zeros_like(zero_ref)

        def fill(j, n):
            @pl.when(zf_ref[j] > 0)
            def _():
                _row_copy(zero_ref, 0, hs_ref, pl.multiple_of(j * ZERO_ROWS, ZERO_ROWS), ZERO_ROWS, sem).start()
            return n + zf_ref[j]

        n_fill = lax.fori_loop(0, n_tiles * (MOE_ROWS // ZERO_ROWS), fill, jnp.int32(0))

        def drain(_, carry):
            _row_copy(zero_ref, 0, hs_ref, 0, ZERO_ROWS, sem).wait()
            return carry

        lax.fori_loop(0, n_fill, drain, 0)

    base = i * td

    for r in range(td):
        _row_copy(h_ref, r, hs_ref, pos_ref[base + r], 1, sem).start()
    _row_copy(h_ref, 0, hs_ref, 0, td, sem).wait()


def _dispatch(h2x, pos, zero_fill, n_tiles, td):
    t, w = h2x.shape
    return pl.pallas_call(
        functools.partial(_dispatch_kernel, td=td, n_tiles=n_tiles),
        grid_spec=pltpu.PrefetchScalarGridSpec(
            num_scalar_prefetch=2,
            grid=(t // td,),
            in_specs=[pl.BlockSpec((td, w), lambda i, pos, zf: (i, 0))],
            out_specs=pl.BlockSpec(memory_space=pl.ANY),
            scratch_shapes=[pltpu.VMEM((ZERO_ROWS, w), F32), pltpu.SemaphoreType.DMA(())]),
        out_shape=jax.ShapeDtypeStruct((n_tiles * MOE_ROWS, w), F32),
        compiler_params=_cparams(("arbitrary",)),
        name="dispatch",
    )(pos, zero_fill, h2x)


def _moe_kernel(elo_ref, ehi_ref, nu_ref, hs_ref, wg_lo, wu_lo, wd_lo, wg_hi, wu_hi, wd_hi, y_ref):
    j = pl.program_id(0)
    d = y_ref.shape[1]

    @pl.when(j < nu_ref[0])
    def _():
        h = hs_ref[:, 0:d].astype(BF16)
        ups = [(_dot(h, wg[0]), _dot(h, wu[0])) for wg, wu in ((wg_lo, wu_lo), (wg_hi, wu_hi))]
        hids = [((a * jax.nn.sigmoid(a)) * u * hs_ref[:, d + lane:d + lane + 1]).astype(BF16)
                for (a, u), lane in zip(ups, (ROUTE_W_LO, ROUTE_W_HI))]
        y_ref[...] = _dot(hids[0], wd_lo[0]) + _dot(hids[1], wd_hi[0])

    @pl.when(j >= nu_ref[0])
    def _():
        y_ref[...] = jnp.zeros_like(y_ref)


def _moe(hs, exp_lo, exp_hi, n_used, wg, wu, wd):
    n_tiles = hs.shape[0] // MOE_ROWS
    _, d, ff = wg.shape
    up = lambda sel: pl.BlockSpec((1, d, ff), lambda j, elo, ehi, nu: ((elo, ehi)[sel][j], 0, 0))
    down = lambda sel: pl.BlockSpec((1, ff, d), lambda j, elo, ehi, nu: ((elo, ehi)[sel][j], 0, 0))
    return pl.pallas_call(
        _moe_kernel,
        grid_spec=pltpu.PrefetchScalarGridSpec(
            num_scalar_prefetch=3,
            grid=(n_tiles,),
            in_specs=[pl.BlockSpec((MOE_ROWS, hs.shape[1]),
                                   lambda j, elo, ehi, nu: (jnp.minimum(j, nu[0] - 1), 0)),
                      up(0), up(0), down(0), up(1), up(1), down(1)],
            out_specs=pl.BlockSpec((MOE_ROWS, d), lambda j, elo, ehi, nu: (j, 0))),
        out_shape=jax.ShapeDtypeStruct((n_tiles * MOE_ROWS, d), F32),
        compiler_params=_cparams(("arbitrary",)),
        name="moe",
    )(exp_lo, exp_hi, n_used, hs, wg, wu, wd, wg, wu, wd)


def _final_kernel(pos_ref, x1_ref, mod_ref, gfin_ref, y_ref, o_ref, rows_ref, sem, *, tf):
    i, n = pl.program_id(0), pl.num_programs(0)

    def fetch(step, slot):
        for r in range(tf):
            _row_copy(y_ref, pos_ref[step * tf + r], rows_ref.at[slot], r, 1, sem.at[slot]).start()

    @pl.when(i == 0)
    def _():
        fetch(0, 0)

    @pl.when(i + 1 < n)
    def _():
        fetch(i + 1, (i + 1) % 2)

    slot = i % 2
    _row_copy(y_ref, 0, rows_ref.at[slot], 0, tf, sem.at[slot]).wait()
    x2 = x1_ref[...] + mod_ref[0, 5:6, :] * rows_ref[slot]
    o_ref[...] = _rms(x2, gfin_ref[...])


def _final(y, pos, x1, mod3, g_final, seq, tf):
    t, d = x1.shape
    per_b = seq // tf
    return pl.pallas_call(
        functools.partial(_final_kernel, tf=tf),
        grid_spec=pltpu.PrefetchScalarGridSpec(
            num_scalar_prefetch=1,
            grid=(t // tf,),
            in_specs=[pl.BlockSpec((tf, d), lambda i, pos: (i, 0)),
                      pl.BlockSpec((1,) + mod3.shape[1:], lambda i, pos: (i // per_b, 0, 0)),
                      pl.BlockSpec((1, d), lambda i, pos: (0, 0)),
                      pl.BlockSpec(memory_space=pl.ANY)],
            out_specs=pl.BlockSpec((tf, d), lambda i, pos: (i, 0)),
            scratch_shapes=[pltpu.VMEM((2, tf, d), F32), pltpu.SemaphoreType.DMA((2,))]),
        out_shape=jax.ShapeDtypeStruct((t, d), F32),
        compiler_params=_cparams(("arbitrary",)),
        name="final",
    )(pos, x1, mod3, g_final, y)


def _layout_weights(w_in, w_uq, w_uk, w_uv, heads, q_lora, kv_lora, d):
    hw = heads * HEAD_DIM
    cuts = np.cumsum([q_lora, kv_lora, ROPE_DIM, hw, hw, hw, d, d])
    w_qlat, w_kvlat, w_kr, w_qb, w_kb, w_vb, w_ga, w_gb = jnp.split(w_in, [int(v) for v in cuts[:-1]], axis=1)
    swap = np.concatenate([np.arange(ROPE_DIM // 2, ROPE_DIM), np.arange(ROPE_DIM // 2)])
    w_kr_sw = w_kr[:, swap]
    kr_blk = jnp.concatenate([w_kr, w_kr, w_kr_sw, w_kr_sw], axis=1)
    band_scale = HEAD_DIM ** -0.5 * LOG2_E
    wbig = jnp.concatenate([w_qlat, w_kvlat, kr_blk, w_qb * band_scale, w_kb, w_ga, w_gb], axis=1)

    def transposed_values(w, k):
        return jnp.pad(w.reshape(k, heads, HEAD_DIM).transpose(1, 2, 0),
                       ((0, 0), (0, VT_ROWS - HEAD_DIM), (0, 0))).reshape(heads * VT_ROWS, k)

    qk = HEAD_DIM + ROPE_DIM
    wq3 = w_uq.reshape(q_lora, heads, qk) * (qk ** -0.5 * LOG2_E)
    nope = wq3[:, :, :HEAD_DIM].reshape(q_lora, hw)
    rope = wq3[:, :, HEAD_DIM:]
    pad = jnp.zeros((q_lora, heads // 2, LANES - 2 * ROPE_DIM), w_uq.dtype)

    def lay(r):
        return jnp.concatenate([r.reshape(q_lora, heads // 2, 2 * ROPE_DIM), pad], axis=2).reshape(q_lora, hw)

    wq = jnp.concatenate([nope, lay(rope), lay(rope[:, :, swap])], axis=1)
    return (wbig.astype(BF16), wq.astype(BF16), w_uk.astype(BF16), transposed_values(w_uv, kv_lora).astype(BF16),
            transposed_values(w_vb, d).astype(BF16))


def kernel(x, c, positions, w_ada, b_ada, g_mix, w_in, g_q, w_uq, g_kv, w_uk, w_uv, rel_bias, w_oa, w_ob,
           w_out, g_ffn, w_rg, b_rg, w_re, b_re, w_gate, w_up, w_down, g_final):
    bsz, seq, d = x.shape
    depth = w_ada.shape[0]
    t = bsz * seq
    heads = rel_bias.shape[1]
    q_lora, kv_lora = g_q.shape[1], g_kv.shape[1]
    tq, tq_mla = TQ_BAND, TQ_MLA
    nblk = -(-LEFT_CHUNKS * CHUNK // tq) + 1
    assert seq % tq == 0 and seq % tq_mla == 0 and tq % CHUNK == 0 and heads % 2 == 0
    assert ROUTE_ROWS == EXPERTS_PER_GROUP and N_GROUPS <= EXPERTS_PER_GROUP

    x2 = x.reshape(t, d)
    pos2 = positions.reshape(bsz, 1, seq)
    out = x2
    for l in range(depth):
        mod3 = _ada(c, w_ada[l], b_ada[l]).reshape(bsz, -1, d)
        wbig, wq, wkv, wvt, wvbt = _layout_weights(w_in[l], w_uq[l], w_uk[l], w_uv[l], heads, q_lora, kv_lora, d)
        qa, ka, vt, qb, kb, vtb, ga, gb = _proj(
            x2, pos2, mod3, g_mix[l].reshape(1, d), wbig, g_q[l].reshape(1, -1), wq, g_kv[l].reshape(1, -1), wkv,
            wvt, wvbt, seq, tm=TM_PROJ)
        oa = _mla(qa, ka, vt, bsz, seq, tq_mla)
        ob = _band(qb, kb, vtb, _band_bias(rel_bias[l], tq, nblk), bsz, seq, tq, nblk,
                   tiles_per_step=BAND_TILES_PER_STEP)

        pad_g = EXPERTS_PER_GROUP - N_GROUPS
        n_route = (N_GROUPS + 1) * EXPERTS_PER_GROUP
        w_r = jnp.concatenate([w_rg[l], jnp.zeros((d, pad_g), F32), w_re[l],
                               jnp.zeros((d, LANES - n_route), F32)], axis=1).T
        b_r = jnp.concatenate([b_rg[l], jnp.zeros((pad_g,), F32), b_re[l],
                               jnp.zeros((LANES - n_route,), F32)]).reshape(LANES, 1)
        w_r_hi = w_r.astype(BF16)
        w_r_lo = (w_r - w_r_hi.astype(F32)).astype(BF16)
        x1, h2x, route_t = _merge(oa, ob, ga, gb, x2, mod3, w_oa[l].astype(BF16), w_ob[l].astype(BF16),
                                  w_out[l].astype(BF16), g_ffn[l].reshape(1, d),
                                  jnp.concatenate([w_r_hi, w_r_lo], axis=0), b_r, seq, tm=TM_MERGE)
        n_tiles = t // MOE_ROWS + N_CLASSES
        pos, counts = _plan(route_t, tc=TC_PLAN)
        pos = pos.reshape(t)
        exp_lo, exp_hi, n_used, zero_fill = _tile_tables(counts, n_tiles)
        hs = _dispatch(h2x, pos, zero_fill, n_tiles, td=TD_DISPATCH)
        y = _moe(hs, exp_lo, exp_hi, n_used, w_gate[l].astype(BF16), w_up[l].astype(BF16), w_down[l].astype(BF16))
        assert l == depth - 1, "multi-layer stacks need an un-normalised residual output"
        out = _final(y, pos, x1, mod3, g_final.reshape(1, d), seq, tf=TF_FINAL)
    return out.reshape(bsz, seq, d)
```

```python
import functools

import numpy as np
import jax
import jax.numpy as jnp
from jax import lax
from jax.experimental import pallas as pl
from jax.experimental.pallas import tpu as pltpu

F32 = jnp.float32
BF16 = jnp.bfloat16

NORM_EPS = 1e-6
NEG_INF = -1e30
ROPE_BASE = 10000.0
CHUNK = 64
LEFT_CHUNKS = 8
MAX_REL = 256
N_GROUPS = 4
EXPERTS_PER_GROUP = 8
PAIRS_PER_GROUP = EXPERTS_PER_GROUP * (EXPERTS_PER_GROUP - 1) // 2
N_CLASSES = N_GROUPS * PAIRS_PER_GROUP
MOE_ROWS = 256
ZERO_ROWS = 64
ROUTE_CLS, ROUTE_W_LO, ROUTE_W_HI = 0, 1, 2
ROUTE_ROWS = 8
LANES = 128
HEAD_DIM = 64
ROPE_DIM = 32
VT_ROWS = HEAD_DIM + 16
LOG2_E = 1.4426950408889634

VMEM_LIMIT = 56 * 1024 * 1024

TM_PROJ = 512
TM_MERGE = 512
TQ_MLA = 512
TQ_BAND = 256
BAND_TILES_PER_STEP = 4
TC_PLAN = 1024
TD_DISPATCH = 2048
TF_FINAL = 512


def _cparams(sem):
    return pltpu.CompilerParams(dimension_semantics=sem, vmem_limit_bytes=VMEM_LIMIT)


def _dot(a, b):
    return jnp.dot(a, b, preferred_element_type=F32)


def _dot_nt(a, b):
    return lax.dot_general(a, b, (((1,), (1,)), ((), ())), preferred_element_type=F32)


def _rms(x, g):
    return x * lax.rsqrt(jnp.mean(x * x, axis=-1, keepdims=True) + NORM_EPS) * g


def _ada_kernel(c_ref, w_ref, b_ref, o_ref):
    c = c_ref[...]
    o_ref[...] = _dot(c * jax.nn.sigmoid(c), w_ref[...]) + b_ref[...]


def _ada(c, w_ada, b_ada):
    bsz, d = c.shape
    n = w_ada.shape[1]
    return pl.pallas_call(
        _ada_kernel,
        grid=(n // d,),
        in_specs=[pl.BlockSpec((bsz, d), lambda j: (0, 0)),
                  pl.BlockSpec((d, d), lambda j: (0, j)),
                  pl.BlockSpec((1, d), lambda j: (0, j))],
        out_specs=pl.BlockSpec((bsz, d), lambda j: (0, j)),
        out_shape=jax.ShapeDtypeStruct((bsz, n), F32),
        compiler_params=_cparams(("arbitrary",)),
        name="ada",
    )(c, w_ada, b_ada.reshape(1, n))


def _proj_kernel(x_ref, pos_ref, mod_ref, gmix_ref, wbig_ref, gq_ref, wq_ref, gkv_ref, wkv_ref,
                 wvt_ref, wvbt_ref, one_ref, inv_ref, sgn_ref,
                 qa_ref, ka_ref, vt_ref, qb_ref, kb_ref, vtb_ref, ga_ref, gb_ref, *, q_lora, kv_lora):
    x = x_ref[...]
    h = _rms(x, gmix_ref[...]) * (1.0 + mod_ref[0, 1:2, :]) + mod_ref[0, 0:1, :]
    hb = h.astype(BF16)

    c0 = q_lora + kv_lora + LANES
    head = _dot(hb, wbig_ref[:, 0:c0])
    q_lat = head[:, 0:q_lora]
    kv_lat = head[:, q_lora:q_lora + kv_lora]
    kr_blk = head[:, q_lora + kv_lora:c0]
    hw = qb_ref.shape[1]
    d = x.shape[1]
    g0 = c0 + 2 * hw
    ga_ref[...] = jax.nn.sigmoid(_dot(hb, wbig_ref[:, g0:g0 + d])).astype(BF16)

    ang = inv_ref[...] * pos_ref[0].astype(F32)
    live = lax.broadcasted_iota(jnp.int32, (LANES, 1), 0) < 2 * ROPE_DIM
    cos_t = jnp.where(live, jnp.cos(ang), 0.0).T
    sin_t = jnp.where(live, jnp.sin(ang) * sgn_ref[...], 0.0).T

    qn = _rms(q_lat, gq_ref[...]).astype(BF16)
    q_all = _dot(qn, wq_ref[...])
    kvn = _rms(kv_lat, gkv_ref[...]).astype(BF16)
    k_nope = _dot(kvn, wkv_ref[...])
    vt_ref[0] = (_dot_nt(wvt_ref[...], kvn) + one_ref[...]).astype(BF16)
    gb_ref[...] = jax.nn.sigmoid(_dot(hb, wbig_ref[:, g0 + d:g0 + 2 * d])).astype(BF16)
    k_rot = (kr_blk * cos_t + pltpu.roll(kr_blk, LANES // 2, 1) * sin_t).astype(BF16)
    for p in range(hw // LANES):
        lo, hi = p * LANES, (p + 1) * LANES
        qa_ref[:, 2 * lo:2 * lo + LANES] = q_all[:, lo:hi].astype(BF16)
        qa_ref[:, 2 * lo + LANES:2 * hi] = (
            q_all[:, hw + lo:hw + hi] * cos_t + q_all[:, 2 * hw + lo:2 * hw + hi] * sin_t).astype(BF16)
        ka_ref[:, 2 * lo:2 * lo + LANES] = k_nope[:, lo:hi].astype(BF16)
        ka_ref[:, 2 * lo + LANES:2 * hi] = k_rot

    qb_ref[...] = _dot(hb, wbig_ref[:, c0:c0 + hw]).astype(BF16)
    kb_ref[...] = _dot(hb, wbig_ref[:, c0 + hw:c0 + 2 * hw]).astype(BF16)
    vtb_ref[0] = (_dot_nt(wvbt_ref[...], hb) + one_ref[...]).astype(BF16)


def _proj(x2, pos2, mod3, g_mix, wbig, g_q, wq, g_kv, wkv, wvt, wvbt, seq, tm):
    t, d = x2.shape
    q_lora, kv_lora = g_q.shape[1], g_kv.shape[1]
    hw = wkv.shape[1]
    per_b = seq // tm
    row = lambda i: (i, 0)
    full = lambda i: (0, 0)
    inv = ROPE_BASE ** (-(np.arange(LANES) % (ROPE_DIM // 2)).astype(np.float32) / (ROPE_DIM // 2))
    sgn = np.where((np.arange(LANES) % ROPE_DIM) < ROPE_DIM // 2, -1.0, 1.0).astype(np.float32)
    ones_col = (np.arange(wvt.shape[0]) % VT_ROWS == HEAD_DIM).astype(np.float32).reshape(-1, 1)
    v_t = jax.ShapeDtypeStruct((t // seq, wvt.shape[0], seq), BF16)
    outs = [jax.ShapeDtypeStruct((t, 2 * hw), BF16), jax.ShapeDtypeStruct((t, 2 * hw), BF16), v_t,
            jax.ShapeDtypeStruct((t, hw), BF16), jax.ShapeDtypeStruct((t, hw), BF16), v_t,
            jax.ShapeDtypeStruct((t, d), BF16), jax.ShapeDtypeStruct((t, d), BF16)]
    v_t_spec = pl.BlockSpec((1, wvt.shape[0], tm), lambda i: (i // per_b, 0, i % per_b))
    out_specs = [v_t_spec if o is v_t else pl.BlockSpec((tm, o.shape[1]), row) for o in outs]
    return pl.pallas_call(
        functools.partial(_proj_kernel, q_lora=q_lora, kv_lora=kv_lora),
        grid=(t // tm,),
        in_specs=[pl.BlockSpec((tm, d), row),
                  pl.BlockSpec((1, 1, tm), lambda i: (i // per_b, 0, i % per_b)),
                  pl.BlockSpec((1,) + mod3.shape[1:], lambda i: (i // per_b, 0, 0)),
                  pl.BlockSpec(g_mix.shape, full),
                  pl.BlockSpec(wbig.shape, full),
                  pl.BlockSpec(g_q.shape, full),
                  pl.BlockSpec(wq.shape, full),
                  pl.BlockSpec(g_kv.shape, full),
                  pl.BlockSpec(wkv.shape, full),
                  pl.BlockSpec(wvt.shape, full),
                  pl.BlockSpec(wvbt.shape, full),
                  pl.BlockSpec(ones_col.shape, full),
                  pl.BlockSpec((LANES, 1), full),
                  pl.BlockSpec((LANES, 1), full)],
        out_specs=out_specs,
        out_shape=outs,
        compiler_params=_cparams(("parallel",)),
        name="proj",
    )(x2, pos2, mod3, g_mix, wbig, g_q, wq, g_kv, wkv, wvt, wvbt, jnp.asarray(ones_col),
      jnp.asarray(inv).reshape(LANES, 1), jnp.asarray(sgn).reshape(LANES, 1))


def _mla_kernel(q_ref, k_ref, vt_ref, o_ref, sa, sb, *, tq):
    i, nq = pl.program_id(2), pl.num_programs(2)
    lane = lax.broadcasted_iota(jnp.int32, (1, 2 * LANES), 1)
    first = (lane < HEAD_DIM) | ((lane >= LANES) & (lane < LANES + ROPE_DIM))
    second = ((lane >= HEAD_DIM) & (lane < LANES)) | ((lane >= LANES + ROPE_DIM) & (lane < LANES + 2 * ROPE_DIM))

    def head_queries(tile):
        q = q_ref[pl.ds(pl.multiple_of(tile * tq, tq), tq), :]
        zero = jnp.zeros_like(q)
        return (jnp.where(first, q, zero), jnp.where(second, q, zero))

    qs = head_queries(i)

    def scores(queries, blk, buf):
        k = k_ref[pl.ds(pl.multiple_of(blk * tq, tq), tq), :]
        for h, qh in enumerate(queries):
            buf[h] = _dot_nt(k, qh)

    def consume(blk, buf, state, mask):
        start = pl.multiple_of(blk * tq, tq)
        ps, ms, alphas = [], [], []
        for h, (m, _) in enumerate(state):
            st = buf[h]
            if mask is not None:
                st = jnp.where(mask, st, NEG_INF)
            m_new = jnp.maximum(m, jnp.max(st, axis=0, keepdims=True))
            ms.append(m_new)
            alphas.append(jnp.exp2(m - m_new))
            ps.append(jnp.exp2(st - m_new).astype(BF16))
        new = []
        for h, (p, m_new, alpha, (_, acc)) in enumerate(zip(ps, ms, alphas, state)):
            vt = vt_ref[0, h * VT_ROWS:(h + 1) * VT_ROWS, pl.ds(start, tq)]
            new.append((m_new, alpha * acc + _dot(vt, p)))
        return tuple(new)

    causal = (lax.broadcasted_iota(jnp.int32, (tq, tq), 0) // CHUNK
              <= lax.broadcasted_iota(jnp.int32, (tq, tq), 1) // CHUNK)
    following = jnp.minimum(i + 1, nq - 1)

    def tile(sx, sy):
        @pl.when(i == 0)
        def _():
            scores(qs, 0, sx)

        def two_blocks(jj, state):
            scores(qs, 2 * jj + 1, sy)
            state = consume(2 * jj, sx, state, None)
            scores(qs, 2 * jj + 2, sx)
            return consume(2 * jj + 1, sy, state, None)

        def even_tail(state):
            scores(head_queries(following), 0, sy)
            return consume(i, sx, state, causal)

        def odd_tail(state):
            scores(qs, i, sy)
            state = consume(i - 1, sx, state, None)
            scores(head_queries(following), 0, sx)
            return consume(i, sy, state, causal)

        init = tuple((jnp.full((1, tq), NEG_INF, F32), jnp.zeros((VT_ROWS, tq), F32)) for _ in range(2))
        state = lax.fori_loop(0, i // 4, lambda jj, st: two_blocks(2 * jj + 1, two_blocks(2 * jj, st)), init)
        state = lax.fori_loop(2 * (i // 4), i // 2, two_blocks, state)
        (_, a0), (_, a1) = lax.cond(i % 2 == 1, odd_tail, even_tail, state)
        out_t = jnp.concatenate([a[0:HEAD_DIM] * (1.0 / a[HEAD_DIM:HEAD_DIM + 1]) for a in (a0, a1)], axis=0)
        o_ref[...] = out_t.T.astype(o_ref.dtype)

    swapped = ((i + 1) // 2) % 2

    @pl.when(swapped == 0)
    def _():
        tile(sa, sb)

    @pl.when(swapped == 1)
    def _():
        tile(sb, sa)


def _mla(qa, ka, vt, bsz, seq, tq):
    t = qa.shape[0]
    pairs = qa.shape[1] // (2 * LANES)
    nq = seq // tq
    return pl.pallas_call(
        functools.partial(_mla_kernel, tq=tq),
        grid=(bsz, pairs, nq),
        in_specs=[pl.BlockSpec((seq, 2 * LANES), lambda b, p, i: (b, p)),
                  pl.BlockSpec((seq, 2 * LANES), lambda b, p, i: (b, p)),
                  pl.BlockSpec((1, 2 * VT_ROWS, seq), lambda b, p, i: (b, p, 0))],
        out_specs=pl.BlockSpec((tq, LANES), lambda b, p, i: (b * nq + i, p)),
        out_shape=jax.ShapeDtypeStruct((t, pairs * LANES), BF16),
        scratch_shapes=[pltpu.VMEM((2, tq, tq), F32)] * 2,
        compiler_params=_cparams(("parallel", "parallel", "arbitrary")),
        name="mla",
    )(qa, ka, vt)


def _band_kernel(q_ref, k_ref, vt_ref, ring_ref, o_ref, bias_ref, *, tq, nblk):
    lane = lax.broadcasted_iota(jnp.int32, (1, LANES), 1)
    nk = nblk * tq

    @pl.when((pl.program_id(1) == 0) & (pl.program_id(2) == 0))
    def _():
        kc = lax.broadcasted_iota(jnp.int32, (nk, tq), 0) // CHUNK - ((nblk - 1) * tq // CHUNK - LEFT_CHUNKS)
        qc = lax.broadcasted_iota(jnp.int32, (nk, tq), 1) // CHUNK
        band = (kc >= qc) & (kc <= qc + LEFT_CHUNKS)
        for h in range(2):
            ring = jnp.broadcast_to(ring_ref[0, h:h + 1, :], (nk, ring_ref.shape[2]))
            rotated = pltpu.roll(ring, 0, 1, stride=1, stride_axis=0)
            bias_ref[h] = jnp.where(band, rotated[:, 0:tq], NEG_INF)

    def scores(n, edge):
        tile = first + n
        q = q_ref[n * tq:(n + 1) * tq, :]
        zero = jnp.zeros_like(q)
        qs = (jnp.where(lane < HEAD_DIM, q, zero), jnp.where(lane >= HEAD_DIM, q, zero))
        blks = [tile - (nblk - 1) + j for j in range(nblk)]
        starts = [pl.multiple_of((jnp.maximum(blk, 0) if edge else blk) * tq, tq) for blk in blks]
        ks = [k_ref[pl.ds(s, tq), :] for s in starts]
        sts = [[_dot_nt(ks[j], qh) + bias_ref[h, j * tq:(j + 1) * tq, :] for j in range(nblk)]
               for h, qh in enumerate(qs)]
        if edge:
            sts = [[jnp.where(blks[j] >= 0, st, NEG_INF) for j, st in enumerate(row)] for row in sts]
        return starts, sts

    def finish(starts, sts):
        outs = []
        for h in range(2):
            m = functools.reduce(jnp.maximum, [jnp.max(st, axis=0, keepdims=True) for st in sts[h]])
            acc = functools.reduce(jnp.add, [
                _dot(vt_ref[0, h * VT_ROWS:(h + 1) * VT_ROWS, pl.ds(starts[j], tq)], jnp.exp2(st - m).astype(BF16))
                for j, st in enumerate(sts[h])])
            outs.append(acc[0:HEAD_DIM] * (1.0 / acc[HEAD_DIM:HEAD_DIM + 1]))
        return jnp.concatenate(outs, axis=0).T

    tiles_per_step = q_ref.shape[0] // tq
    assert tiles_per_step >= nblk - 1
    first = pl.program_id(2) * tiles_per_step

    def tiles(edge):
        pending = scores(0, edge)
        for n in range(tiles_per_step):
            upcoming = scores(n + 1, edge) if n + 1 < tiles_per_step else None
            o_ref[n * tq:(n + 1) * tq, :] = finish(*pending).astype(o_ref.dtype)
            pending = upcoming

    for edge in (True, False):
        @pl.when((pl.program_id(2) == 0) == edge)
        def _():
            tiles(edge)


def _band_ring(rel_table, tq, nblk):
    ring = (nblk + 1) * tq
    x = np.arange(ring)
    query_minus_key = np.where(x < tq, x, x - ring) + (nblk - 1) * tq
    idx = np.clip(query_minus_key, -MAX_REL, MAX_REL) + MAX_REL
    heads = rel_table.shape[0]
    return (rel_table.astype(F32)[:, idx] * LOG2_E).reshape(heads // 2, 2, ring)


def _band(qb, kb, vtb, ring, bsz, seq, tq, nblk, tiles_per_step):
    t, hw = qb.shape
    pairs = hw // LANES
    rows = tq * tiles_per_step
    nq = seq // rows
    return pl.pallas_call(
        functools.partial(_band_kernel, tq=tq, nblk=nblk),
        grid=(pairs, bsz, nq),
        in_specs=[pl.BlockSpec((rows, LANES), lambda p, b, i: (b * nq + i, p)),
                  pl.BlockSpec((seq, LANES), lambda p, b, i: (b, p)),
                  pl.BlockSpec((1, 2 * VT_ROWS, seq), lambda p, b, i: (b, p, 0)),
                  pl.BlockSpec((1,) + ring.shape[1:], lambda p, b, i: (p, 0, 0))],
        out_specs=pl.BlockSpec((rows, LANES), lambda p, b, i: (b * nq + i, p)),
        out_shape=jax.ShapeDtypeStruct((t, hw), BF16),
        scratch_shapes=[pltpu.VMEM((2, nblk * tq, tq), F32)],
        compiler_params=_cparams(("arbitrary", "arbitrary", "arbitrary")),
        name="band",
    )(qb, kb, vtb, ring)


def _merge_kernel(oa_ref, ob_ref, ga_ref, gb_ref, x_ref, mod_ref, woa_ref, wob_ref, wout_ref, gffn_ref,
                  wr_ref, br_ref, x1_ref, h2x_ref, rt_ref, cnt_ref):
    merged = (ga_ref[...].astype(F32) * _dot(oa_ref[...], woa_ref[...])
              + gb_ref[...].astype(F32) * _dot(ob_ref[...], wob_ref[...]))
    mix = _dot(merged.astype(BF16), wout_ref[...])
    x1 = x_ref[...] + mod_ref[0, 2:3, :] * mix
    x1_ref[...] = x1
    h2 = _rms(x1, gffn_ref[...]) * (1.0 + mod_ref[0, 4:5, :]) + mod_ref[0, 3:4, :]

    h_hi = h2.astype(BF16)
    h_lo = (h2 - h_hi.astype(F32)).astype(BF16)
    r_hi = _dot_nt(wr_ref[...], h_hi)
    logits = (r_hi[0:LANES] + _dot_nt(wr_ref[0:LANES, :], h_lo) + r_hi[LANES:2 * LANES]) + br_ref[...]
    blk = EXPERTS_PER_GROUP
    sub = lax.broadcasted_iota(jnp.int32, (blk, logits.shape[1]), 0)
    far = jnp.int32(blk)
    is_g = sub < N_GROUPS
    gl = jnp.where(is_g, logits[0:blk], NEG_INF)
    g_max = jnp.max(gl, axis=0, keepdims=True)
    g_w = 1.0 / jnp.sum(jnp.where(is_g, jnp.exp(gl - g_max), 0.0), axis=0, keepdims=True)
    g_idx = jnp.min(jnp.where(gl == g_max, sub, far), axis=0, keepdims=True)
    el = logits[blk:2 * blk]
    for g in range(1, N_GROUPS):
        el = jnp.where(g_idx == g, logits[(g + 1) * blk:(g + 2) * blk], el)
    e1 = jnp.max(el, axis=0, keepdims=True)
    i1 = jnp.min(jnp.where(el == e1, sub, far), axis=0, keepdims=True)
    el2 = jnp.where(sub == i1, NEG_INF, el)
    e2 = jnp.max(el2, axis=0, keepdims=True)
    i2 = jnp.min(jnp.where(el2 == e2, sub, far), axis=0, keepdims=True)
    ratio = jnp.exp(e2 - e1)
    w1 = g_w / (1.0 + ratio)
    w2 = g_w * ratio / (1.0 + ratio)
    e_lo, e_hi = jnp.minimum(i1, i2), jnp.maximum(i1, i2)
    cls = (g_idx * PAIRS_PER_GROUP + jnp.right_shift(e_lo * (2 * EXPERTS_PER_GROUP - 1 - e_lo), 1)
           + (e_hi - e_lo - 1))
    first_is_lo = i1 < i2
    w_lo, w_hi = jnp.where(first_is_lo, w1, w2), jnp.where(first_is_lo, w2, w1)
    route_t = (jnp.where(sub == ROUTE_CLS, cls.astype(F32), 0.0) + jnp.where(sub == ROUTE_W_LO, w_lo, 0.0)
               + jnp.where(sub == ROUTE_W_HI, w_hi, 0.0))
    rt_ref[...] = route_t

    @pl.when(pl.program_id(0) == 0)
    def _():
        cnt_ref[...] = jnp.zeros_like(cnt_ref)

    cnt_ref[...] += jnp.sum(_class_onehot(cls.astype(F32)).astype(F32), axis=1, keepdims=True)
    d = h2.shape[1]
    h2x_ref[:, 0:d] = h2
    h2x_ref[:, d:d + LANES] = jnp.concatenate(
        [route_t, jnp.zeros((LANES - ROUTE_ROWS, route_t.shape[1]), F32)], axis=0).T


def _merge(oa, ob, ga, gb, x2, mod3, woa, wob, wout, g_ffn, wr, br, seq, tm):
    t, d = x2.shape
    hw = oa.shape[1]
    per_b = seq // tm
    row = lambda i: (i, 0)
    full = lambda i: (0, 0)
    return pl.pallas_call(
        _merge_kernel,
        grid=(t // tm,),
        in_specs=[pl.BlockSpec((tm, hw), row), pl.BlockSpec((tm, hw), row),
                  pl.BlockSpec((tm, d), row), pl.BlockSpec((tm, d), row),
                  pl.BlockSpec((tm, d), row),
                  pl.BlockSpec((1,) + mod3.shape[1:], lambda i: (i // per_b, 0, 0)),
                  pl.BlockSpec(woa.shape, full), pl.BlockSpec(wob.shape, full),
                  pl.BlockSpec(wout.shape, full), pl.BlockSpec(g_ffn.shape, full),
                  pl.BlockSpec(wr.shape, full), pl.BlockSpec(br.shape, full)],
        out_specs=[pl.BlockSpec((tm, d), row), pl.BlockSpec((tm, d + LANES), row),
                   pl.BlockSpec((ROUTE_ROWS, tm), lambda i: (0, i)),
                   pl.BlockSpec((LANES, 1), lambda i: (0, 0))],
        out_shape=[jax.ShapeDtypeStruct((t, d), F32), jax.ShapeDtypeStruct((t, d + LANES), F32),
                   jax.ShapeDtypeStruct((ROUTE_ROWS, t), F32), jax.ShapeDtypeStruct((LANES, 1), F32)],
        compiler_params=_cparams(("arbitrary",)),
        name="merge",
    )(oa, ob, ga, gb, x2, mod3, woa, wob, wout, g_ffn, wr, br)


def _class_onehot(cls_row):
    return lax.broadcasted_iota(jnp.int32, (LANES, cls_row.shape[1]), 0).astype(F32) == cls_row


def _plan_kernel(rt_ref, cnt_ref, pos_ref, carry_ref, off_ref, *, tc):
    i = pl.program_id(0)
    onehot = _class_onehot(rt_ref[ROUTE_CLS:ROUTE_CLS + 1, :])
    per_class = jnp.sum(onehot.astype(F32), axis=1, keepdims=True)

    @pl.when(i == 0)
    def _():
        padded = jnp.floor((cnt_ref[...] + (MOE_ROWS - 1)) * (1.0 / MOE_ROWS)) * MOE_ROWS
        hi = jnp.floor(padded * (1.0 / 256.0))
        digits = [jnp.broadcast_to(v, (LANES, LANES)).astype(BF16) for v in (hi, padded - 256.0 * hi)]
        below = (lax.broadcasted_iota(jnp.int32, (LANES, LANES), 1)
                 < lax.broadcasted_iota(jnp.int32, (LANES, LANES), 0)).astype(BF16)
        off = 256.0 * _dot(below, digits[0]) + _dot(below, digits[1])
        off_ref[...] = off[:, 0:1]
        carry_ref[...] = jnp.zeros_like(carry_ref)

    earlier = (lax.broadcasted_iota(jnp.int32, (tc, tc), 0)
               < lax.broadcasted_iota(jnp.int32, (tc, tc), 1)).astype(BF16)
    rank = _dot(onehot.astype(BF16), earlier)
    row = rank + (carry_ref[...] + off_ref[...])
    pos_ref[...] = jnp.sum(jnp.where(onehot, row, 0.0), axis=0, keepdims=True).astype(jnp.int32)
    carry_ref[...] += per_class


def _plan(route_t, counts, tc):
    t = route_t.shape[1]
    return pl.pallas_call(
        functools.partial(_plan_kernel, tc=tc),
        grid=(t // tc,),
        in_specs=[pl.BlockSpec((ROUTE_ROWS, tc), lambda i: (0, i)),
                  pl.BlockSpec((LANES, 1), lambda i: (0, 0))],
        out_specs=pl.BlockSpec((1, tc), lambda i: (0, i)),
        out_shape=jax.ShapeDtypeStruct((1, t), jnp.int32),
        scratch_shapes=[pltpu.VMEM((LANES, 1), F32), pltpu.VMEM((LANES, 1), F32)],
        compiler_params=_cparams(("arbitrary",)),
        name="plan",
    )(route_t, counts)


def _tile_tables(counts, n_tiles):
    pairs = [(lo, hi) for lo in range(EXPERTS_PER_GROUP) for hi in range(lo + 1, EXPERTS_PER_GROUP)]
    exp_lo = np.array([g * EXPERTS_PER_GROUP + lo for g in range(N_GROUPS) for lo, _ in pairs], np.int32)
    exp_hi = np.array([g * EXPERTS_PER_GROUP + hi for g in range(N_GROUPS) for _, hi in pairs], np.int32)
    cnt = counts.reshape(-1)[:N_CLASSES].astype(jnp.int32)
    tiles_per_class = (cnt + (MOE_ROWS - 1)) // MOE_ROWS
    tile_end = jnp.cumsum(tiles_per_class)
    n_used = tile_end[-1]
    j = jnp.arange(n_tiles, dtype=jnp.int32)
    tile_cls = jnp.sum(tile_end[None, :] <= jnp.minimum(j, n_used - 1)[:, None], axis=1, dtype=jnp.int32)
    valid_end = (tile_end - tiles_per_class) * MOE_ROWS + cnt
    of_class = tile_cls[:, None] == jnp.arange(N_CLASSES, dtype=jnp.int32)[None, :]

    def per_tile(table):
        return jnp.sum(jnp.where(of_class, table[None, :], 0), axis=1, dtype=jnp.int32)

    units = MOE_ROWS // ZERO_ROWS
    unit_end = (jnp.arange(n_tiles * units, dtype=jnp.int32) + 1) * ZERO_ROWS
    zero_fill = (unit_end.reshape(n_tiles, units) > per_tile(valid_end)[:, None]).astype(jnp.int32).reshape(-1)
    return (per_tile(jnp.asarray(exp_lo)), per_tile(jnp.asarray(exp_hi)), n_used.reshape(1).astype(jnp.int32),
            zero_fill)


def _row_copy(src, src_row, dst, dst_row, n, sem):
    return pltpu.make_async_copy(src.at[pl.ds(src_row, n)], dst.at[pl.ds(dst_row, n)], sem)


def _dispatch_kernel(pos_ref, zf_ref, h_ref, hs_ref, zero_ref, sem, *, td, n_tiles):
    i = pl.program_id(0)

    @pl.when(i == 0)
    def _():
        zero_ref[...] = jnp.zeros_like(zero_ref)

        def fill(j, n):
            @pl.when(zf_ref[j] > 0)
            def _():
                _row_copy(zero_ref, 0, hs_ref, pl.multiple_of(j * ZERO_ROWS, ZERO_ROWS), ZERO_ROWS, sem).start()
            return n + zf_ref[j]

        n_fill = lax.fori_loop(0, n_tiles * (MOE_ROWS // ZERO_ROWS), fill, jnp.int32(0))

        def drain(_, carry):
            _row_copy(zero_ref, 0, hs_ref, 0, ZERO_ROWS, sem).wait()
            return carry

        lax.fori_loop(0, n_fill, drain, 0)

    base = i * td

    for r in range(td):
        _row_copy(h_ref, r, hs_ref, pos_ref[base + r], 1, sem).start()
    _row_copy(h_ref, 0, hs_ref, 0, td, sem).wait()


def _dispatch(h2x, pos, zero_fill, n_tiles, td):
    t, w = h2x.shape
    return pl.pallas_call(
        functools.partial(_dispatch_kernel, td=td, n_tiles=n_tiles),
        grid_spec=pltpu.PrefetchScalarGridSpec(
            num_scalar_prefetch=2,
            grid=(t // td,),
            in_specs=[pl.BlockSpec((td, w), lambda i, pos, zf: (i, 0))],
            out_specs=pl.BlockSpec(memory_space=pl.ANY),
            scratch_shapes=[pltpu.VMEM((ZERO_ROWS, w), F32), pltpu.SemaphoreType.DMA(())]),
        out_shape=jax.ShapeDtypeStruct((n_tiles * MOE_ROWS, w), F32),
        compiler_params=_cparams(("arbitrary",)),
        name="dispatch",
    )(pos, zero_fill, h2x)


def _moe_kernel(elo_ref, ehi_ref, nu_ref, hs_ref, wg_lo, wu_lo, wd_lo, wg_hi, wu_hi, wd_hi, y_ref):
    j = pl.program_id(0)
    d = y_ref.shape[1]

    @pl.when(j < nu_ref[0])
    def _():
        h = hs_ref[:, 0:d].astype(BF16)
        ups = [(_dot(h, wg[0]), _dot(h, wu[0])) for wg, wu in ((wg_lo, wu_lo), (wg_hi, wu_hi))]
        hids = [((a * jax.nn.sigmoid(a)) * u * hs_ref[:, d + lane:d + lane + 1]).astype(BF16)
                for (a, u), lane in zip(ups, (ROUTE_W_LO, ROUTE_W_HI))]
        y_ref[...] = _dot(hids[0], wd_lo[0]) + _dot(hids[1], wd_hi[0])

    @pl.when(j >= nu_ref[0])
    def _():
        y_ref[...] = jnp.zeros_like(y_ref)


def _moe(hs, exp_lo, exp_hi, n_used, wg, wu, wd):
    n_tiles = hs.shape[0] // MOE_ROWS
    _, d, ff = wg.shape
    up = lambda sel: pl.BlockSpec((1, d, ff), lambda j, elo, ehi, nu: ((elo, ehi)[sel][j], 0, 0))
    down = lambda sel: pl.BlockSpec((1, ff, d), lambda j, elo, ehi, nu: ((elo, ehi)[sel][j], 0, 0))
    return pl.pallas_call(
        _moe_kernel,
        grid_spec=pltpu.PrefetchScalarGridSpec(
            num_scalar_prefetch=3,
            grid=(n_tiles,),
            in_specs=[pl.BlockSpec((MOE_ROWS, hs.shape[1]),
                                   lambda j, elo, ehi, nu: (jnp.minimum(j, nu[0] - 1), 0)),
                      up(0), up(0), down(0), up(1), up(1), down(1)],
            out_specs=pl.BlockSpec((MOE_ROWS, d), lambda j, elo, ehi, nu: (j, 0))),
        out_shape=jax.ShapeDtypeStruct((n_tiles * MOE_ROWS, d), F32),
        compiler_params=_cparams(("arbitrary",)),
        name="moe",
    )(exp_lo, exp_hi, n_used, hs, wg, wu, wd, wg, wu, wd)


def _final_kernel(pos_ref, x1_ref, mod_ref, gfin_ref, y_ref, o_ref, rows_ref, sem, *, tf):
    i, n = pl.program_id(0), pl.num_programs(0)

    def fetch(step, slot):
        for r in range(tf):
            _row_copy(y_ref, pos_ref[step * tf + r], rows_ref.at[slot], r, 1, sem.at[slot]).start()

    @pl.when(i == 0)
    def _():
        fetch(0, 0)

    @pl.when(i + 1 < n)
    def _():
        fetch(i + 1, (i + 1) % 2)

    slot = i % 2
    _row_copy(y_ref, 0, rows_ref.at[slot], 0, tf, sem.at[slot]).wait()
    x2 = x1_ref[...] + mod_ref[0, 5:6, :] * rows_ref[slot]
    o_ref[...] = _rms(x2, gfin_ref[...])


def _final(y, pos, x1, mod3, g_final, seq, tf):
    t, d = x1.shape
    per_b = seq // tf
    return pl.pallas_call(
        functools.partial(_final_kernel, tf=tf),
        grid_spec=pltpu.PrefetchScalarGridSpec(
            num_scalar_prefetch=1,
            grid=(t // tf,),
            in_specs=[pl.BlockSpec((tf, d), lambda i, pos: (i, 0)),
                      pl.BlockSpec((1,) + mod3.shape[1:], lambda i, pos: (i // per_b, 0, 0)),
                      pl.BlockSpec((1, d), lambda i, pos: (0, 0)),
                      pl.BlockSpec(memory_space=pl.ANY)],
            out_specs=pl.BlockSpec((tf, d), lambda i, pos: (i, 0)),
            scratch_shapes=[pltpu.VMEM((2, tf, d), F32), pltpu.SemaphoreType.DMA((2,))]),
        out_shape=jax.ShapeDtypeStruct((t, d), F32),
        compiler_params=_cparams(("arbitrary",)),
        name="final",
    )(pos, x1, mod3, g_final, y)


def _layout_weights(w_in, w_uq, w_uk, w_uv, heads, q_lora, kv_lora, d):
    hw = heads * HEAD_DIM
    cuts = np.cumsum([q_lora, kv_lora, ROPE_DIM, hw, hw, hw, d, d])
    w_qlat, w_kvlat, w_kr, w_qb, w_kb, w_vb, w_ga, w_gb = jnp.split(w_in, [int(v) for v in cuts[:-1]], axis=1)
    swap = np.concatenate([np.arange(ROPE_DIM // 2, ROPE_DIM), np.arange(ROPE_DIM // 2)])
    w_kr_sw = w_kr[:, swap]
    kr_blk = jnp.concatenate([w_kr, w_kr, w_kr_sw, w_kr_sw], axis=1)
    band_scale = HEAD_DIM ** -0.5 * LOG2_E
    wbig = jnp.concatenate([w_qlat, w_kvlat, kr_blk, w_qb * band_scale, w_kb, w_ga, w_gb], axis=1)

    def transposed_values(w, k):
        return jnp.pad(w.reshape(k, heads, HEAD_DIM).transpose(1, 2, 0),
                       ((0, 0), (0, VT_ROWS - HEAD_DIM), (0, 0))).reshape(heads * VT_ROWS, k)

    qk = HEAD_DIM + ROPE_DIM
    wq3 = w_uq.reshape(q_lora, heads, qk) * (qk ** -0.5 * LOG2_E)
    nope = wq3[:, :, :HEAD_DIM].reshape(q_lora, hw)
    rope = wq3[:, :, HEAD_DIM:]
    pad = jnp.zeros((q_lora, heads // 2, LANES - 2 * ROPE_DIM), w_uq.dtype)

    def lay(r):
        return jnp.concatenate([r.reshape(q_lora, heads // 2, 2 * ROPE_DIM), pad], axis=2).reshape(q_lora, hw)

    wq = jnp.concatenate([nope, lay(rope), lay(rope[:, :, swap])], axis=1)
    return (wbig.astype(BF16), wq.astype(BF16), w_uk.astype(BF16), transposed_values(w_uv, kv_lora).astype(BF16),
            transposed_values(w_vb, d).astype(BF16))


def kernel(x, c, positions, w_ada, b_ada, g_mix, w_in, g_q, w_uq, g_kv, w_uk, w_uv, rel_bias, w_oa, w_ob,
           w_out, g_ffn, w_rg, b_rg, w_re, b_re, w_gate, w_up, w_down, g_final):
    bsz, seq, d = x.shape
    depth = w_ada.shape[0]
    t = bsz * seq
    heads = rel_bias.shape[1]
    q_lora, kv_lora = g_q.shape[1], g_kv.shape[1]
    tq, tq_mla = TQ_BAND, TQ_MLA
    nblk = -(-LEFT_CHUNKS * CHUNK // tq) + 1
    assert seq % tq == 0 and seq % tq_mla == 0 and tq % CHUNK == 0 and heads % 2 == 0
    assert ROUTE_ROWS == EXPERTS_PER_GROUP and N_GROUPS <= EXPERTS_PER_GROUP

    x2 = x.reshape(t, d)
    pos2 = positions.reshape(bsz, 1, seq)
    out = x2
    for l in range(depth):
        mod3 = _ada(c, w_ada[l], b_ada[l]).reshape(bsz, -1, d)
        wbig, wq, wkv, wvt, wvbt = _layout_weights(w_in[l], w_uq[l], w_uk[l], w_uv[l], heads, q_lora, kv_lora, d)
        qa, ka, vt, qb, kb, vtb, ga, gb = _proj(
            x2, pos2, mod3, g_mix[l].reshape(1, d), wbig, g_q[l].reshape(1, -1), wq, g_kv[l].reshape(1, -1), wkv,
            wvt, wvbt, seq, tm=TM_PROJ)
        oa = _mla(qa, ka, vt, bsz, seq, tq_mla)
        ob = _band(qb, kb, vtb, _band_ring(rel_bias[l], tq, nblk), bsz, seq, tq, nblk,
                   tiles_per_step=BAND_TILES_PER_STEP)

        pad_g = EXPERTS_PER_GROUP - N_GROUPS
        n_route = (N_GROUPS + 1) * EXPERTS_PER_GROUP
        w_r = jnp.concatenate([w_rg[l], jnp.zeros((d, pad_g), F32), w_re[l],
                               jnp.zeros((d, LANES - n_route), F32)], axis=1).T
        b_r = jnp.concatenate([b_rg[l], jnp.zeros((pad_g,), F32), b_re[l],
                               jnp.zeros((LANES - n_route,), F32)]).reshape(LANES, 1)
        w_r_hi = w_r.astype(BF16)
        w_r_lo = (w_r - w_r_hi.astype(F32)).astype(BF16)
        x1, h2x, route_t, counts = _merge(oa, ob, ga, gb, x2, mod3, w_oa[l].astype(BF16), w_ob[l].astype(BF16),
                                          w_out[l].astype(BF16), g_ffn[l].reshape(1, d),
                                          jnp.concatenate([w_r_hi, w_r_lo], axis=0), b_r, seq, tm=TM_MERGE)
        n_tiles = t // MOE_ROWS + N_CLASSES
        pos = _plan(route_t, counts, tc=TC_PLAN).reshape(t)
        exp_lo, exp_hi, n_used, zero_fill = _tile_tables(counts, n_tiles)
        hs = _dispatch(h2x, pos, zero_fill, n_tiles, td=TD_DISPATCH)
        y = _moe(hs, exp_lo, exp_hi, n_used, w_gate[l].astype(BF16), w_up[l].astype(BF16), w_down[l].astype(BF16))
        assert l == depth - 1, "multi-layer stacks need an un-normalised residual output"
        out = _final(y, pos, x1, mod3, g_final.reshape(1, d), seq, tf=TF_FINAL)
    return out.reshape(bsz, seq, d)
```

```python
import functools

import numpy as np
import jax
import jax.numpy as jnp
from jax import lax
from jax.experimental import pallas as pl
from jax.experimental.pallas import tpu as pltpu

F32 = jnp.float32
BF16 = jnp.bfloat16

NORM_EPS = 1e-6
NEG_INF = -1e30
ROPE_BASE = 10000.0
CHUNK = 64
LEFT_CHUNKS = 8
MAX_REL = 256
N_GROUPS = 4
EXPERTS_PER_GROUP = 8
PAIRS_PER_GROUP = EXPERTS_PER_GROUP * (EXPERTS_PER_GROUP - 1) // 2
N_CLASSES = N_GROUPS * PAIRS_PER_GROUP
MOE_ROWS = 256
ZERO_ROWS = 64
ROUTE_CLS, ROUTE_W_LO, ROUTE_W_HI = 0, 1, 2
ROUTE_ROWS = 8
LANES = 128
HEAD_DIM = 64
ROPE_DIM = 32
VT_ROWS = HEAD_DIM + 16
LOG2_E = 1.4426950408889634

VMEM_LIMIT = 56 * 1024 * 1024

TM_PROJ = 512
TM_MERGE = 512
TQ_MLA = 512
TQ_BAND = 256
BAND_TILES_PER_STEP = 4
TC_PLAN = 1024
TD_DISPATCH = 2048
TF_FINAL = 512


def _cparams(sem):
    return pltpu.CompilerParams(dimension_semantics=sem, vmem_limit_bytes=VMEM_LIMIT)


def _dot(a, b):
    return jnp.dot(a, b, preferred_element_type=F32)


def _dot_nt(a, b):
    return lax.dot_general(a, b, (((1,), (1,)), ((), ())), preferred_element_type=F32)


def _rms(x, g):
    return x * lax.rsqrt(jnp.mean(x * x, axis=-1, keepdims=True) + NORM_EPS) * g


def _ada_kernel(c_ref, w_ref, b_ref, o_ref):
    c = c_ref[...]
    o_ref[...] = _dot(c * jax.nn.sigmoid(c), w_ref[...]) + b_ref[...]


def _ada(c, w_ada, b_ada):
    bsz, d = c.shape
    n = w_ada.shape[1]
    return pl.pallas_call(
        _ada_kernel,
        grid=(n // d,),
        in_specs=[pl.BlockSpec((bsz, d), lambda j: (0, 0)),
                  pl.BlockSpec((d, d), lambda j: (0, j)),
                  pl.BlockSpec((1, d), lambda j: (0, j))],
        out_specs=pl.BlockSpec((bsz, d), lambda j: (0, j)),
        out_shape=jax.ShapeDtypeStruct((bsz, n), F32),
        compiler_params=_cparams(("arbitrary",)),
        name="ada",
    )(c, w_ada, b_ada.reshape(1, n))


def _proj_kernel(x_ref, pos_ref, mod_ref, gmix_ref, wbig_ref, gq_ref, wq_ref, gkv_ref, wkv_ref,
                 wvt_ref, wvbt_ref, one_ref, inv_ref, sgn_ref,
                 qa_ref, ka_ref, vt_ref, qb_ref, kb_ref, vtb_ref, ga_ref, gb_ref, *, q_lora, kv_lora):
    x = x_ref[...]
    h = _rms(x, gmix_ref[...]) * (1.0 + mod_ref[0, 1:2, :]) + mod_ref[0, 0:1, :]
    hb = h.astype(BF16)

    c0 = q_lora + kv_lora + LANES
    head = _dot(hb, wbig_ref[:, 0:c0])
    q_lat = head[:, 0:q_lora]
    kv_lat = head[:, q_lora:q_lora + kv_lora]
    kr_blk = head[:, q_lora + kv_lora:c0]
    hw = qb_ref.shape[1]
    d = x.shape[1]
    g0 = c0 + 2 * hw
    ga_ref[...] = jax.nn.sigmoid(_dot(hb, wbig_ref[:, g0:g0 + d])).astype(BF16)

    ang = inv_ref[...] * pos_ref[0].astype(F32)
    live = lax.broadcasted_iota(jnp.int32, (LANES, 1), 0) < 2 * ROPE_DIM
    cos_t = jnp.where(live, jnp.cos(ang), 0.0).T
    sin_t = jnp.where(live, jnp.sin(ang) * sgn_ref[...], 0.0).T

    qn = _rms(q_lat, gq_ref[...]).astype(BF16)
    q_all = _dot(qn, wq_ref[...])
    kvn = _rms(kv_lat, gkv_ref[...]).astype(BF16)
    k_nope = _dot(kvn, wkv_ref[...])
    vt_ref[0] = (_dot_nt(wvt_ref[...], kvn) + one_ref[...]).astype(BF16)
    gb_ref[...] = jax.nn.sigmoid(_dot(hb, wbig_ref[:, g0 + d:g0 + 2 * d])).astype(BF16)
    k_rot = (kr_blk * cos_t + pltpu.roll(kr_blk, LANES // 2, 1) * sin_t).astype(BF16)
    for p in range(hw // LANES):
        lo, hi = p * LANES, (p + 1) * LANES
        qa_ref[:, 2 * lo:2 * lo + LANES] = q_all[:, lo:hi].astype(BF16)
        qa_ref[:, 2 * lo + LANES:2 * hi] = (
            q_all[:, hw + lo:hw + hi] * cos_t + q_all[:, 2 * hw + lo:2 * hw + hi] * sin_t).astype(BF16)
        ka_ref[:, 2 * lo:2 * lo + LANES] = k_nope[:, lo:hi].astype(BF16)
        ka_ref[:, 2 * lo + LANES:2 * hi] = k_rot

    qb_ref[...] = _dot(hb, wbig_ref[:, c0:c0 + hw]).astype(BF16)
    kb_ref[...] = _dot(hb, wbig_ref[:, c0 + hw:c0 + 2 * hw]).astype(BF16)
    vtb_ref[0] = (_dot_nt(wvbt_ref[...], hb) + one_ref[...]).astype(BF16)


def _proj(x2, pos2, mod3, g_mix, wbig, g_q, wq, g_kv, wkv, wvt, wvbt, seq, tm):
    t, d = x2.shape
    q_lora, kv_lora = g_q.shape[1], g_kv.shape[1]
    hw = wkv.shape[1]
    per_b = seq // tm
    row = lambda i: (i, 0)
    full = lambda i: (0, 0)
    inv = ROPE_BASE ** (-(np.arange(LANES) % (ROPE_DIM // 2)).astype(np.float32) / (ROPE_DIM // 2))
    sgn = np.where((np.arange(LANES) % ROPE_DIM) < ROPE_DIM // 2, -1.0, 1.0).astype(np.float32)
    ones_col = (np.arange(wvt.shape[0]) % VT_ROWS == HEAD_DIM).astype(np.float32).reshape(-1, 1)
    v_t = jax.ShapeDtypeStruct((t // seq, wvt.shape[0], seq), BF16)
    outs = [jax.ShapeDtypeStruct((t, 2 * hw), BF16), jax.ShapeDtypeStruct((t, 2 * hw), BF16), v_t,
            jax.ShapeDtypeStruct((t, hw), BF16), jax.ShapeDtypeStruct((t, hw), BF16), v_t,
            jax.ShapeDtypeStruct((t, d), BF16), jax.ShapeDtypeStruct((t, d), BF16)]
    v_t_spec = pl.BlockSpec((1, wvt.shape[0], tm), lambda i: (i // per_b, 0, i % per_b))
    out_specs = [v_t_spec if o is v_t else pl.BlockSpec((tm, o.shape[1]), row) for o in outs]
    return pl.pallas_call(
        functools.partial(_proj_kernel, q_lora=q_lora, kv_lora=kv_lora),
        grid=(t // tm,),
        in_specs=[pl.BlockSpec((tm, d), row),
                  pl.BlockSpec((1, 1, tm), lambda i: (i // per_b, 0, i % per_b)),
                  pl.BlockSpec((1,) + mod3.shape[1:], lambda i: (i // per_b, 0, 0)),
                  pl.BlockSpec(g_mix.shape, full),
                  pl.BlockSpec(wbig.shape, full),
                  pl.BlockSpec(g_q.shape, full),
                  pl.BlockSpec(wq.shape, full),
                  pl.BlockSpec(g_kv.shape, full),
                  pl.BlockSpec(wkv.shape, full),
                  pl.BlockSpec(wvt.shape, full),
                  pl.BlockSpec(wvbt.shape, full),
                  pl.BlockSpec(ones_col.shape, full),
                  pl.BlockSpec((LANES, 1), full),
                  pl.BlockSpec((LANES, 1), full)],
        out_specs=out_specs,
        out_shape=outs,
        compiler_params=_cparams(("parallel",)),
        name="proj",
    )(x2, pos2, mod3, g_mix, wbig, g_q, wq, g_kv, wkv, wvt, wvbt, jnp.asarray(ones_col),
      jnp.asarray(inv).reshape(LANES, 1), jnp.asarray(sgn).reshape(LANES, 1))


def _mla_kernel(q_ref, k_ref, vt_ref, o_ref, sa, sb, *, tq):
    i, nq = pl.program_id(2), pl.num_programs(2)
    lane = lax.broadcasted_iota(jnp.int32, (1, 2 * LANES), 1)
    first = (lane < HEAD_DIM) | ((lane >= LANES) & (lane < LANES + ROPE_DIM))
    second = ((lane >= HEAD_DIM) & (lane < LANES)) | ((lane >= LANES + ROPE_DIM) & (lane < LANES + 2 * ROPE_DIM))

    def head_queries(tile):
        q = q_ref[pl.ds(pl.multiple_of(tile * tq, tq), tq), :]
        zero = jnp.zeros_like(q)
        return (jnp.where(first, q, zero), jnp.where(second, q, zero))

    qs = head_queries(i)

    def scores(queries, blk, buf):
        k = k_ref[pl.ds(pl.multiple_of(blk * tq, tq), tq), :]
        for h, qh in enumerate(queries):
            buf[h] = _dot_nt(k, qh)

    def consume(blk, buf, state, mask):
        start = pl.multiple_of(blk * tq, tq)
        ps, ms, alphas = [], [], []
        for h, (m, _) in enumerate(state):
            st = buf[h]
            if mask is not None:
                st = jnp.where(mask, st, NEG_INF)
            m_new = jnp.maximum(m, jnp.max(st, axis=0, keepdims=True))
            ms.append(m_new)
            alphas.append(jnp.exp2(m - m_new))
            ps.append(jnp.exp2(st - m_new).astype(BF16))
        new = []
        for h, (p, m_new, alpha, (_, acc)) in enumerate(zip(ps, ms, alphas, state)):
            vt = vt_ref[0, h * VT_ROWS:(h + 1) * VT_ROWS, pl.ds(start, tq)]
            new.append((m_new, alpha * acc + _dot(vt, p)))
        return tuple(new)

    causal = (lax.broadcasted_iota(jnp.int32, (tq, tq), 0) // CHUNK
              <= lax.broadcasted_iota(jnp.int32, (tq, tq), 1) // CHUNK)
    following = jnp.minimum(i + 1, nq - 1)

    def tile(sx, sy):
        @pl.when(i == 0)
        def _():
            scores(qs, 0, sx)

        def two_blocks(jj, state):
            scores(qs, 2 * jj + 1, sy)
            state = consume(2 * jj, sx, state, None)
            scores(qs, 2 * jj + 2, sx)
            return consume(2 * jj + 1, sy, state, None)

        def even_tail(state):
            scores(head_queries(following), 0, sy)
            return consume(i, sx, state, causal)

        def odd_tail(state):
            scores(qs, i, sy)
            state = consume(i - 1, sx, state, None)
            scores(head_queries(following), 0, sx)
            return consume(i, sy, state, causal)

        init = tuple((jnp.full((1, tq), NEG_INF, F32), jnp.zeros((VT_ROWS, tq), F32)) for _ in range(2))
        state = lax.fori_loop(0, i // 4, lambda jj, st: two_blocks(2 * jj + 1, two_blocks(2 * jj, st)), init)
        state = lax.fori_loop(2 * (i // 4), i // 2, two_blocks, state)
        (_, a0), (_, a1) = lax.cond(i % 2 == 1, odd_tail, even_tail, state)
        out_t = jnp.concatenate([a[0:HEAD_DIM] * (1.0 / a[HEAD_DIM:HEAD_DIM + 1]) for a in (a0, a1)], axis=0)
        o_ref[...] = out_t.T.astype(o_ref.dtype)

    swapped = ((i + 1) // 2) % 2

    @pl.when(swapped == 0)
    def _():
        tile(sa, sb)

    @pl.when(swapped == 1)
    def _():
        tile(sb, sa)


def _mla(qa, ka, vt, bsz, seq, tq):
    t = qa.shape[0]
    pairs = qa.shape[1] // (2 * LANES)
    nq = seq // tq
    return pl.pallas_call(
        functools.partial(_mla_kernel, tq=tq),
        grid=(bsz, pairs, nq),
        in_specs=[pl.BlockSpec((seq, 2 * LANES), lambda b, p, i: (b, p)),
                  pl.BlockSpec((seq, 2 * LANES), lambda b, p, i: (b, p)),
                  pl.BlockSpec((1, 2 * VT_ROWS, seq), lambda b, p, i: (b, p, 0))],
        out_specs=pl.BlockSpec((tq, LANES), lambda b, p, i: (b * nq + i, p)),
        out_shape=jax.ShapeDtypeStruct((t, pairs * LANES), BF16),
        scratch_shapes=[pltpu.VMEM((2, tq, tq), F32)] * 2,
        compiler_params=_cparams(("parallel", "parallel", "arbitrary")),
        name="mla",
    )(qa, ka, vt)


def _band_kernel(q_ref, k_ref, vt_ref, ring_ref, o_ref, bias_ref, *, tq, nblk):
    lane = lax.broadcasted_iota(jnp.int32, (1, LANES), 1)
    nk = nblk * tq

    @pl.when((pl.program_id(1) == 0) & (pl.program_id(2) == 0))
    def _():
        kc = lax.broadcasted_iota(jnp.int32, (nk, tq), 0) // CHUNK - ((nblk - 1) * tq // CHUNK - LEFT_CHUNKS)
        qc = lax.broadcasted_iota(jnp.int32, (nk, tq), 1) // CHUNK
        band = (kc >= qc) & (kc <= qc + LEFT_CHUNKS)
        for h in range(2):
            ring = jnp.broadcast_to(ring_ref[0, h:h + 1, :], (nk, ring_ref.shape[2]))
            rotated = pltpu.roll(ring, 0, 1, stride=1, stride_axis=0)
            bias_ref[h] = jnp.where(band, rotated[:, 0:tq], NEG_INF)

    def scores(n, edge):
        tile = first + n
        q = q_ref[n * tq:(n + 1) * tq, :]
        zero = jnp.zeros_like(q)
        qs = (jnp.where(lane < HEAD_DIM, q, zero), jnp.where(lane >= HEAD_DIM, q, zero))
        blks = [tile - (nblk - 1) + j for j in range(nblk)]
        starts = [pl.multiple_of((jnp.maximum(blk, 0) if edge else blk) * tq, tq) for blk in blks]
        ks = [k_ref[pl.ds(s, tq), :] for s in starts]
        sts = [[_dot_nt(ks[j], qh) + bias_ref[h, j * tq:(j + 1) * tq, :] for j in range(nblk)]
               for h, qh in enumerate(qs)]
        if edge:
            sts = [[jnp.where(blks[j] >= 0, st, NEG_INF) for j, st in enumerate(row)] for row in sts]
        return starts, sts

    def finish(starts, sts):
        outs = []
        for h in range(2):
            m = functools.reduce(jnp.maximum, [jnp.max(st, axis=0, keepdims=True) for st in sts[h]])
            acc = functools.reduce(jnp.add, [
                _dot(vt_ref[0, h * VT_ROWS:(h + 1) * VT_ROWS, pl.ds(starts[j], tq)], jnp.exp2(st - m).astype(BF16))
                for j, st in enumerate(sts[h])])
            outs.append(acc[0:HEAD_DIM] * (1.0 / acc[HEAD_DIM:HEAD_DIM + 1]))
        return jnp.concatenate(outs, axis=0).T

    tiles_per_step = q_ref.shape[0] // tq
    assert tiles_per_step >= nblk - 1
    first = pl.program_id(2) * tiles_per_step

    def tiles(edge):
        pending = scores(0, edge)
        for n in range(tiles_per_step):
            upcoming = scores(n + 1, edge) if n + 1 < tiles_per_step else None
            o_ref[n * tq:(n + 1) * tq, :] = finish(*pending).astype(o_ref.dtype)
            pending = upcoming

    for edge in (True, False):
        @pl.when((pl.program_id(2) == 0) == edge)
        def _():
            tiles(edge)


def _band_ring(rel_table, tq, nblk):
    ring = (nblk + 1) * tq
    x = np.arange(ring)
    query_minus_key = np.where(x < tq, x, x - ring) + (nblk - 1) * tq
    idx = np.clip(query_minus_key, -MAX_REL, MAX_REL) + MAX_REL
    heads = rel_table.shape[0]
    return (rel_table.astype(F32)[:, idx] * LOG2_E).reshape(heads // 2, 2, ring)


def _band(qb, kb, vtb, ring, bsz, seq, tq, nblk, tiles_per_step):
    t, hw = qb.shape
    pairs = hw // LANES
    rows = tq * tiles_per_step
    nq = seq // rows
    return pl.pallas_call(
        functools.partial(_band_kernel, tq=tq, nblk=nblk),
        grid=(pairs, bsz, nq),
        in_specs=[pl.BlockSpec((rows, LANES), lambda p, b, i: (b * nq + i, p)),
                  pl.BlockSpec((seq, LANES), lambda p, b, i: (b, p)),
                  pl.BlockSpec((1, 2 * VT_ROWS, seq), lambda p, b, i: (b, p, 0)),
                  pl.BlockSpec((1,) + ring.shape[1:], lambda p, b, i: (p, 0, 0))],
        out_specs=pl.BlockSpec((rows, LANES), lambda p, b, i: (b * nq + i, p)),
        out_shape=jax.ShapeDtypeStruct((t, hw), BF16),
        scratch_shapes=[pltpu.VMEM((2, nblk * tq, tq), F32)],
        compiler_params=_cparams(("arbitrary", "arbitrary", "arbitrary")),
        name="band",
    )(qb, kb, vtb, ring)


def _merge_kernel(oa_ref, ob_ref, ga_ref, gb_ref, x_ref, mod_ref, woa_ref, wob_ref, wout_ref, gffn_ref,
                  wr_ref, br_ref, x1_ref, h2x_ref, rt_ref, cnt_ref):
    merged = (ga_ref[...].astype(F32) * _dot(oa_ref[...], woa_ref[...])
              + gb_ref[...].astype(F32) * _dot(ob_ref[...], wob_ref[...]))
    mix = _dot(merged.astype(BF16), wout_ref[...])
    x1 = x_ref[...] + mod_ref[0, 2:3, :] * mix
    x1_ref[...] = x1
    h2 = _rms(x1, gffn_ref[...]) * (1.0 + mod_ref[0, 4:5, :]) + mod_ref[0, 3:4, :]

    h_hi = h2.astype(BF16)
    h_lo = (h2 - h_hi.astype(F32)).astype(BF16)
    r_hi = _dot_nt(wr_ref[...], h_hi)
    logits = (r_hi[0:LANES] + _dot_nt(wr_ref[0:LANES, :], h_lo) + r_hi[LANES:2 * LANES]) + br_ref[...]
    blk = EXPERTS_PER_GROUP
    sub = lax.broadcasted_iota(jnp.int32, (blk, logits.shape[1]), 0)
    far = jnp.int32(blk)
    is_g = sub < N_GROUPS
    gl = jnp.where(is_g, logits[0:blk], NEG_INF)
    g_max = jnp.max(gl, axis=0, keepdims=True)
    g_w = 1.0 / jnp.sum(jnp.where(is_g, jnp.exp(gl - g_max), 0.0), axis=0, keepdims=True)
    g_idx = jnp.min(jnp.where(gl == g_max, sub, far), axis=0, keepdims=True)
    el = logits[blk:2 * blk]
    for g in range(1, N_GROUPS):
        el = jnp.where(g_idx == g, logits[(g + 1) * blk:(g + 2) * blk], el)
    e1 = jnp.max(el, axis=0, keepdims=True)
    i1 = jnp.min(jnp.where(el == e1, sub, far), axis=0, keepdims=True)
    el2 = jnp.where(sub == i1, NEG_INF, el)
    e2 = jnp.max(el2, axis=0, keepdims=True)
    i2 = jnp.min(jnp.where(el2 == e2, sub, far), axis=0, keepdims=True)
    ratio = jnp.exp(e2 - e1)
    w1 = g_w / (1.0 + ratio)
    w2 = g_w * ratio / (1.0 + ratio)
    e_lo, e_hi = jnp.minimum(i1, i2), jnp.maximum(i1, i2)
    cls = (g_idx * PAIRS_PER_GROUP + jnp.right_shift(e_lo * (2 * EXPERTS_PER_GROUP - 1 - e_lo), 1)
           + (e_hi - e_lo - 1))
    first_is_lo = i1 < i2
    w_lo, w_hi = jnp.where(first_is_lo, w1, w2), jnp.where(first_is_lo, w2, w1)
    route_t = (jnp.where(sub == ROUTE_CLS, cls.astype(F32), 0.0) + jnp.where(sub == ROUTE_W_LO, w_lo, 0.0)
               + jnp.where(sub == ROUTE_W_HI, w_hi, 0.0))
    rt_ref[...] = route_t

    @pl.when(pl.program_id(0) == 0)
    def _():
        cnt_ref[...] = jnp.zeros_like(cnt_ref)

    cnt_ref[...] += jnp.sum(_class_onehot(cls.astype(F32)).astype(F32), axis=1, keepdims=True)
    d = h2.shape[1]
    h2x_ref[:, 0:d] = h2
    h2x_ref[:, d:d + LANES] = jnp.concatenate(
        [route_t, jnp.zeros((LANES - ROUTE_ROWS, route_t.shape[1]), F32)], axis=0).T


def _merge(oa, ob, ga, gb, x2, mod3, woa, wob, wout, g_ffn, wr, br, seq, tm):
    t, d = x2.shape
    hw = oa.shape[1]
    per_b = seq // tm
    row = lambda i: (i, 0)
    full = lambda i: (0, 0)
    return pl.pallas_call(
        _merge_kernel,
        grid=(t // tm,),
        in_specs=[pl.BlockSpec((tm, hw), row), pl.BlockSpec((tm, hw), row),
                  pl.BlockSpec((tm, d), row), pl.BlockSpec((tm, d), row),
                  pl.BlockSpec((tm, d), row),
                  pl.BlockSpec((1,) + mod3.shape[1:], lambda i: (i // per_b, 0, 0)),
                  pl.BlockSpec(woa.shape, full), pl.BlockSpec(wob.shape, full),
                  pl.BlockSpec(wout.shape, full), pl.BlockSpec(g_ffn.shape, full),
                  pl.BlockSpec(wr.shape, full), pl.BlockSpec(br.shape, full)],
        out_specs=[pl.BlockSpec((tm, d), row), pl.BlockSpec((tm, d + LANES), row),
                   pl.BlockSpec((ROUTE_ROWS, tm), lambda i: (0, i)),
                   pl.BlockSpec((LANES, 1), lambda i: (0, 0))],
        out_shape=[jax.ShapeDtypeStruct((t, d), F32), jax.ShapeDtypeStruct((t, d + LANES), F32),
                   jax.ShapeDtypeStruct((ROUTE_ROWS, t), F32), jax.ShapeDtypeStruct((LANES, 1), F32)],
        compiler_params=_cparams(("arbitrary",)),
        name="merge",
    )(oa, ob, ga, gb, x2, mod3, woa, wob, wout, g_ffn, wr, br)


def _class_onehot(cls_row):
    return lax.broadcasted_iota(jnp.int32, (LANES, cls_row.shape[1]), 0).astype(F32) == cls_row


def _plan_kernel(rt_ref, cnt_ref, pos_ref, carry_ref, off_ref, *, tc):
    i = pl.program_id(0)
    onehot = _class_onehot(rt_ref[ROUTE_CLS:ROUTE_CLS + 1, :])
    per_class = jnp.sum(onehot.astype(F32), axis=1, keepdims=True)

    @pl.when(i == 0)
    def _():
        padded = jnp.floor((cnt_ref[...] + (MOE_ROWS - 1)) * (1.0 / MOE_ROWS)) * MOE_ROWS
        hi = jnp.floor(padded * (1.0 / 256.0))
        digits = [jnp.broadcast_to(v, (LANES, LANES)).astype(BF16) for v in (hi, padded - 256.0 * hi)]
        below = (lax.broadcasted_iota(jnp.int32, (LANES, LANES), 1)
                 < lax.broadcasted_iota(jnp.int32, (LANES, LANES), 0)).astype(BF16)
        off = 256.0 * _dot(below, digits[0]) + _dot(below, digits[1])
        off_ref[...] = off[:, 0:1]
        carry_ref[...] = jnp.zeros_like(carry_ref)

    earlier = (lax.broadcasted_iota(jnp.int32, (tc, tc), 0)
               < lax.broadcasted_iota(jnp.int32, (tc, tc), 1)).astype(BF16)
    rank = _dot(onehot.astype(BF16), earlier)
    row = rank + (carry_ref[...] + off_ref[...])
    pos_ref[...] = jnp.sum(jnp.where(onehot, row, 0.0), axis=0, keepdims=True).astype(jnp.int32)
    carry_ref[...] += per_class


def _plan(route_t, counts, tc):
    t = route_t.shape[1]
    return pl.pallas_call(
        functools.partial(_plan_kernel, tc=tc),
        grid=(t // tc,),
        in_specs=[pl.BlockSpec((ROUTE_ROWS, tc), lambda i: (0, i)),
                  pl.BlockSpec((LANES, 1), lambda i: (0, 0))],
        out_specs=pl.BlockSpec((1, tc), lambda i: (0, i)),
        out_shape=jax.ShapeDtypeStruct((1, t), jnp.int32),
        scratch_shapes=[pltpu.VMEM((LANES, 1), F32), pltpu.VMEM((LANES, 1), F32)],
        compiler_params=_cparams(("arbitrary",)),
        name="plan",
    )(route_t, counts)


def _tile_tables(counts, n_tiles):
    pairs = [(lo, hi) for lo in range(EXPERTS_PER_GROUP) for hi in range(lo + 1, EXPERTS_PER_GROUP)]
    exp_lo = np.array([g * EXPERTS_PER_GROUP + lo for g in range(N_GROUPS) for lo, _ in pairs], np.int32)
    exp_hi = np.array([g * EXPERTS_PER_GROUP + hi for g in range(N_GROUPS) for _, hi in pairs], np.int32)
    cnt = counts.reshape(-1)[:N_CLASSES].astype(jnp.int32)
    tiles_per_class = (cnt + (MOE_ROWS - 1)) // MOE_ROWS
    tile_end = jnp.cumsum(tiles_per_class)
    n_used = tile_end[-1]
    j = jnp.arange(n_tiles, dtype=jnp.int32)
    tile_cls = jnp.sum(tile_end[None, :] <= jnp.minimum(j, n_used - 1)[:, None], axis=1, dtype=jnp.int32)
    valid_end = (tile_end - tiles_per_class) * MOE_ROWS + cnt
    of_class = tile_cls[:, None] == jnp.arange(N_CLASSES, dtype=jnp.int32)[None, :]

    def per_tile(table):
        return jnp.sum(jnp.where(of_class, table[None, :], 0), axis=1, dtype=jnp.int32)

    units = MOE_ROWS // ZERO_ROWS
    unit_end = (jnp.arange(n_tiles * units, dtype=jnp.int32) + 1) * ZERO_ROWS
    zero_fill = (unit_end.reshape(n_tiles, units) > per_tile(valid_end)[:, None]).astype(jnp.int32).reshape(-1)
    return (per_tile(jnp.asarray(exp_lo)), per_tile(jnp.asarray(exp_hi)), n_used.reshape(1).astype(jnp.int32),
            zero_fill)


def _row_copy(src, src_row, dst, dst_row, n, sem):
    return pltpu.make_async_copy(src.at[pl.ds(src_row, n)], dst.at[pl.ds(dst_row, n)], sem)


def _dispatch_kernel(pos_ref, zf_ref, h_ref, *refs, td, n_tiles, n_weights):
    w_in, hs_ref, w_out = refs[:n_weights], refs[n_weights], refs[n_weights + 1:2 * n_weights + 1]
    zero_ref, sem = refs[2 * n_weights + 1:]
    i = pl.program_id(0)

    @pl.when(i == 0)
    def _():
        zero_ref[...] = jnp.zeros_like(zero_ref)

        def fill(j, n):
            @pl.when(zf_ref[j] > 0)
            def _():
                _row_copy(zero_ref, 0, hs_ref, pl.multiple_of(j * ZERO_ROWS, ZERO_ROWS), ZERO_ROWS, sem).start()
            return n + zf_ref[j]

        n_fill = lax.fori_loop(0, n_tiles * (MOE_ROWS // ZERO_ROWS), fill, jnp.int32(0))

        def drain(_, carry):
            _row_copy(zero_ref, 0, hs_ref, 0, ZERO_ROWS, sem).wait()
            return carry

        lax.fori_loop(0, n_fill, drain, 0)

    base = i * td

    for r in range(td):
        _row_copy(h_ref, r, hs_ref, pos_ref[base + r], 1, sem).start()
    for src, dst in zip(w_in, w_out):
        dst[...] = src[...].astype(dst.dtype)
    _row_copy(h_ref, 0, hs_ref, 0, td, sem).wait()


def _dispatch(h2x, pos, zero_fill, n_tiles, td, weights):
    t, w = h2x.shape
    steps = t // td
    flat = [a.reshape(-1, a.shape[-1]) for a in weights]
    assert all(a.shape[0] % (steps * 16) == 0 for a in flat)
    w_specs = [pl.BlockSpec((a.shape[0] // steps, a.shape[1]), lambda i, pos, zf: (i, 0)) for a in flat]
    outs = pl.pallas_call(
        functools.partial(_dispatch_kernel, td=td, n_tiles=n_tiles, n_weights=len(flat)),
        grid_spec=pltpu.PrefetchScalarGridSpec(
            num_scalar_prefetch=2,
            grid=(steps,),
            in_specs=[pl.BlockSpec((td, w), lambda i, pos, zf: (i, 0))] + w_specs,
            out_specs=[pl.BlockSpec(memory_space=pl.ANY)] + w_specs,
            scratch_shapes=[pltpu.VMEM((ZERO_ROWS, w), F32), pltpu.SemaphoreType.DMA(())]),
        out_shape=[jax.ShapeDtypeStruct((n_tiles * MOE_ROWS, w), F32)]
        + [jax.ShapeDtypeStruct(a.shape, BF16) for a in flat],
        compiler_params=_cparams(("arbitrary",)),
        name="dispatch",
    )(pos, zero_fill, h2x, *flat)
    return outs[0], [o.reshape(a.shape) for o, a in zip(outs[1:], weights)]


def _moe_kernel(elo_ref, ehi_ref, nu_ref, hs_ref, wg_lo, wu_lo, wd_lo, wg_hi, wu_hi, wd_hi, y_ref):
    j = pl.program_id(0)
    d = y_ref.shape[1]

    @pl.when(j < nu_ref[0])
    def _():
        h = hs_ref[:, 0:d].astype(BF16)
        ups = [(_dot(h, wg[0]), _dot(h, wu[0])) for wg, wu in ((wg_lo, wu_lo), (wg_hi, wu_hi))]
        hids = [((a * jax.nn.sigmoid(a)) * u * hs_ref[:, d + lane:d + lane + 1]).astype(BF16)
                for (a, u), lane in zip(ups, (ROUTE_W_LO, ROUTE_W_HI))]
        y_ref[...] = _dot(hids[0], wd_lo[0]) + _dot(hids[1], wd_hi[0])

    @pl.when(j >= nu_ref[0])
    def _():
        y_ref[...] = jnp.zeros_like(y_ref)


def _moe(hs, exp_lo, exp_hi, n_used, wg, wu, wd):
    n_tiles = hs.shape[0] // MOE_ROWS
    _, d, ff = wg.shape
    up = lambda sel: pl.BlockSpec((1, d, ff), lambda j, elo, ehi, nu: ((elo, ehi)[sel][j], 0, 0))
    down = lambda sel: pl.BlockSpec((1, ff, d), lambda j, elo, ehi, nu: ((elo, ehi)[sel][j], 0, 0))
    return pl.pallas_call(
        _moe_kernel,
        grid_spec=pltpu.PrefetchScalarGridSpec(
            num_scalar_prefetch=3,
            grid=(n_tiles,),
            in_specs=[pl.BlockSpec((MOE_ROWS, hs.shape[1]),
                                   lambda j, elo, ehi, nu: (jnp.minimum(j, nu[0] - 1), 0)),
                      up(0), up(0), down(0), up(1), up(1), down(1)],
            out_specs=pl.BlockSpec((MOE_ROWS, d), lambda j, elo, ehi, nu: (j, 0))),
        out_shape=jax.ShapeDtypeStruct((n_tiles * MOE_ROWS, d), F32),
        compiler_params=_cparams(("arbitrary",)),
        name="moe",
    )(exp_lo, exp_hi, n_used, hs, wg, wu, wd, wg, wu, wd)


def _final_kernel(pos_ref, x1_ref, mod_ref, gfin_ref, y_ref, o_ref, rows_ref, sem, *, tf):
    i, n = pl.program_id(0), pl.num_programs(0)

    def fetch(step, slot):
        for r in range(tf):
            _row_copy(y_ref, pos_ref[step * tf + r], rows_ref.at[slot], r, 1, sem.at[slot]).start()

    @pl.when(i == 0)
    def _():
        fetch(0, 0)

    @pl.when(i + 1 < n)
    def _():
        fetch(i + 1, (i + 1) % 2)

    slot = i % 2
    _row_copy(y_ref, 0, rows_ref.at[slot], 0, tf, sem.at[slot]).wait()
    x2 = x1_ref[...] + mod_ref[0, 5:6, :] * rows_ref[slot]
    o_ref[...] = _rms(x2, gfin_ref[...])


def _final(y, pos, x1, mod3, g_final, seq, tf):
    t, d = x1.shape
    per_b = seq // tf
    return pl.pallas_call(
        functools.partial(_final_kernel, tf=tf),
        grid_spec=pltpu.PrefetchScalarGridSpec(
            num_scalar_prefetch=1,
            grid=(t // tf,),
            in_specs=[pl.BlockSpec((tf, d), lambda i, pos: (i, 0)),
                      pl.BlockSpec((1,) + mod3.shape[1:], lambda i, pos: (i // per_b, 0, 0)),
                      pl.BlockSpec((1, d), lambda i, pos: (0, 0)),
                      pl.BlockSpec(memory_space=pl.ANY)],
            out_specs=pl.BlockSpec((tf, d), lambda i, pos: (i, 0)),
            scratch_shapes=[pltpu.VMEM((2, tf, d), F32), pltpu.SemaphoreType.DMA((2,))]),
        out_shape=jax.ShapeDtypeStruct((t, d), F32),
        compiler_params=_cparams(("arbitrary",)),
        name="final",
    )(pos, x1, mod3, g_final, y)


def _layout_weights(w_in, w_uq, w_uk, w_uv, heads, q_lora, kv_lora, d):
    hw = heads * HEAD_DIM
    cuts = np.cumsum([q_lora, kv_lora, ROPE_DIM, hw, hw, hw, d, d])
    w_qlat, w_kvlat, w_kr, w_qb, w_kb, w_vb, w_ga, w_gb = jnp.split(w_in, [int(v) for v in cuts[:-1]], axis=1)
    swap = np.concatenate([np.arange(ROPE_DIM // 2, ROPE_DIM), np.arange(ROPE_DIM // 2)])
    w_kr_sw = w_kr[:, swap]
    kr_blk = jnp.concatenate([w_kr, w_kr, w_kr_sw, w_kr_sw], axis=1)
    band_scale = HEAD_DIM ** -0.5 * LOG2_E
    wbig = jnp.concatenate([w_qlat, w_kvlat, kr_blk, w_qb * band_scale, w_kb, w_ga, w_gb], axis=1)

    def transposed_values(w, k):
        return jnp.pad(w.reshape(k, heads, HEAD_DIM).transpose(1, 2, 0),
                       ((0, 0), (0, VT_ROWS - HEAD_DIM), (0, 0))).reshape(heads * VT_ROWS, k)

    qk = HEAD_DIM + ROPE_DIM
    wq3 = w_uq.reshape(q_lora, heads, qk) * (qk ** -0.5 * LOG2_E)
    nope = wq3[:, :, :HEAD_DIM].reshape(q_lora, hw)
    rope = wq3[:, :, HEAD_DIM:]
    pad = jnp.zeros((q_lora, heads // 2, LANES - 2 * ROPE_DIM), w_uq.dtype)

    def lay(r):
        return jnp.concatenate([r.reshape(q_lora, heads // 2, 2 * ROPE_DIM), pad], axis=2).reshape(q_lora, hw)

    wq = jnp.concatenate([nope, lay(rope), lay(rope[:, :, swap])], axis=1)
    return (wbig.astype(BF16), wq.astype(BF16), w_uk.astype(BF16), transposed_values(w_uv, kv_lora).astype(BF16),
            transposed_values(w_vb, d).astype(BF16))


def kernel(x, c, positions, w_ada, b_ada, g_mix, w_in, g_q, w_uq, g_kv, w_uk, w_uv, rel_bias, w_oa, w_ob,
           w_out, g_ffn, w_rg, b_rg, w_re, b_re, w_gate, w_up, w_down, g_final):
    bsz, seq, d = x.shape
    depth = w_ada.shape[0]
    t = bsz * seq
    heads = rel_bias.shape[1]
    q_lora, kv_lora = g_q.shape[1], g_kv.shape[1]
    tq, tq_mla = TQ_BAND, TQ_MLA
    nblk = -(-LEFT_CHUNKS * CHUNK // tq) + 1
    assert seq % tq == 0 and seq % tq_mla == 0 and tq % CHUNK == 0 and heads % 2 == 0
    assert ROUTE_ROWS == EXPERTS_PER_GROUP and N_GROUPS <= EXPERTS_PER_GROUP

    x2 = x.reshape(t, d)
    pos2 = positions.reshape(bsz, 1, seq)
    out = x2
    for l in range(depth):
        mod3 = _ada(c, w_ada[l], b_ada[l]).reshape(bsz, -1, d)
        wbig, wq, wkv, wvt, wvbt = _layout_weights(w_in[l], w_uq[l], w_uk[l], w_uv[l], heads, q_lora, kv_lora, d)
        qa, ka, vt, qb, kb, vtb, ga, gb = _proj(
            x2, pos2, mod3, g_mix[l].reshape(1, d), wbig, g_q[l].reshape(1, -1), wq, g_kv[l].reshape(1, -1), wkv,
            wvt, wvbt, seq, tm=TM_PROJ)
        oa = _mla(qa, ka, vt, bsz, seq, tq_mla)
        ob = _band(qb, kb, vtb, _band_ring(rel_bias[l], tq, nblk), bsz, seq, tq, nblk,
                   tiles_per_step=BAND_TILES_PER_STEP)

        pad_g = EXPERTS_PER_GROUP - N_GROUPS
        n_route = (N_GROUPS + 1) * EXPERTS_PER_GROUP
        w_r = jnp.concatenate([w_rg[l], jnp.zeros((d, pad_g), F32), w_re[l],
                               jnp.zeros((d, LANES - n_route), F32)], axis=1).T
        b_r = jnp.concatenate([b_rg[l], jnp.zeros((pad_g,), F32), b_re[l],
                               jnp.zeros((LANES - n_route,), F32)]).reshape(LANES, 1)
        w_r_hi = w_r.astype(BF16)
        w_r_lo = (w_r - w_r_hi.astype(F32)).astype(BF16)
        x1, h2x, route_t, counts = _merge(oa, ob, ga, gb, x2, mod3, w_oa[l].astype(BF16), w_ob[l].astype(BF16),
                                          w_out[l].astype(BF16), g_ffn[l].reshape(1, d),
                                          jnp.concatenate([w_r_hi, w_r_lo], axis=0), b_r, seq, tm=TM_MERGE)
        n_tiles = t // MOE_ROWS + N_CLASSES
        pos = _plan(route_t, counts, tc=TC_PLAN).reshape(t)
        exp_lo, exp_hi, n_used, zero_fill = _tile_tables(counts, n_tiles)
        hs, expert_weights = _dispatch(h2x, pos, zero_fill, n_tiles, TD_DISPATCH, (w_gate[l], w_up[l], w_down[l]))
        y = _moe(hs, exp_lo, exp_hi, n_used, *expert_weights)
        assert l == depth - 1, "multi-layer stacks need an un-normalised residual output"
        out = _final(y, pos, x1, mod3, g_final.reshape(1, d), seq, tf=TF_FINAL)
    return out.reshape(bsz, seq, d)
```

```python
import functools

import numpy as np
import jax
import jax.numpy as jnp
from jax import lax
from jax.experimental import pallas as pl
from jax.experimental.pallas import tpu as pltpu

F32 = jnp.float32
BF16 = jnp.bfloat16

NORM_EPS = 1e-6
NEG_INF = -1e30
ROPE_BASE = 10000.0
CHUNK = 64
LEFT_CHUNKS = 8
MAX_REL = 256
N_GROUPS = 4
EXPERTS_PER_GROUP = 8
PAIRS_PER_GROUP = EXPERTS_PER_GROUP * (EXPERTS_PER_GROUP - 1) // 2
N_CLASSES = N_GROUPS * PAIRS_PER_GROUP
MOE_ROWS = 256
ZERO_ROWS = 64
ROUTE_CLS, ROUTE_W_LO, ROUTE_W_HI = 0, 1, 2
ROUTE_ROWS = 8
LANES = 128
HEAD_DIM = 64
ROPE_DIM = 32
VT_ROWS = HEAD_DIM + 16
LOG2_E = 1.4426950408889634

VMEM_LIMIT = 56 * 1024 * 1024

TM_PROJ = 512
TM_MERGE = 512
TQ_MLA = 512
TQ_BAND = 256
BAND_TILES_PER_STEP = 4
TC_PLAN = 1024
TD_DISPATCH = 2048
TF_FINAL = 512


def _cparams(sem):
    return pltpu.CompilerParams(dimension_semantics=sem, vmem_limit_bytes=VMEM_LIMIT)


def _dot(a, b):
    return jnp.dot(a, b, preferred_element_type=F32)


def _dot_nt(a, b):
    return lax.dot_general(a, b, (((1,), (1,)), ((), ())), preferred_element_type=F32)


def _rms(x, g):
    return x * lax.rsqrt(jnp.mean(x * x, axis=-1, keepdims=True) + NORM_EPS) * g


def _ada_kernel(c_ref, w_ref, b_ref, o_ref):
    c = c_ref[...]
    o_ref[...] = _dot(c * jax.nn.sigmoid(c), w_ref[...]) + b_ref[...]


def _ada(c, w_ada, b_ada):
    bsz, d = c.shape
    n = w_ada.shape[1]
    return pl.pallas_call(
        _ada_kernel,
        grid=(n // d,),
        in_specs=[pl.BlockSpec((bsz, d), lambda j: (0, 0)),
                  pl.BlockSpec((d, d), lambda j: (0, j)),
                  pl.BlockSpec((1, d), lambda j: (0, j))],
        out_specs=pl.BlockSpec((bsz, d), lambda j: (0, j)),
        out_shape=jax.ShapeDtypeStruct((bsz, n), F32),
        compiler_params=_cparams(("arbitrary",)),
        name="ada",
    )(c, w_ada, b_ada.reshape(1, n))


def _proj_kernel(x_ref, pos_ref, mod_ref, gmix_ref, wbig_ref, gq_ref, wq_ref, gkv_ref, wkv_ref,
                 wvt_ref, wvbt_ref, one_ref, inv_ref, sgn_ref,
                 qa_ref, ka_ref, vt_ref, qb_ref, kb_ref, vtb_ref, ga_ref, gb_ref, *, q_lora, kv_lora):
    x = x_ref[...]
    h = _rms(x, gmix_ref[...]) * (1.0 + mod_ref[0, 1:2, :]) + mod_ref[0, 0:1, :]
    hb = h.astype(BF16)

    c0 = q_lora + kv_lora + LANES
    head = _dot(hb, wbig_ref[:, 0:c0])
    q_lat = head[:, 0:q_lora]
    kv_lat = head[:, q_lora:q_lora + kv_lora]
    kr_blk = head[:, q_lora + kv_lora:c0]
    hw = qb_ref.shape[1]
    d = x.shape[1]
    g0 = c0 + 2 * hw
    ga_ref[...] = jax.nn.sigmoid(_dot(hb, wbig_ref[:, g0:g0 + d])).astype(BF16)

    ang = inv_ref[...] * pos_ref[0].astype(F32)
    live = lax.broadcasted_iota(jnp.int32, (LANES, 1), 0) < 2 * ROPE_DIM
    cos_t = jnp.where(live, jnp.cos(ang), 0.0).T
    sin_t = jnp.where(live, jnp.sin(ang) * sgn_ref[...], 0.0).T

    qn = _rms(q_lat, gq_ref[...]).astype(BF16)
    q_all = _dot(qn, wq_ref[...])
    kvn = _rms(kv_lat, gkv_ref[...]).astype(BF16)
    k_nope = _dot(kvn, wkv_ref[...])
    vt_ref[0] = (_dot_nt(wvt_ref[...], kvn) + one_ref[...]).astype(BF16)
    gb_ref[...] = jax.nn.sigmoid(_dot(hb, wbig_ref[:, g0 + d:g0 + 2 * d])).astype(BF16)
    k_rot = (kr_blk * cos_t + pltpu.roll(kr_blk, LANES // 2, 1) * sin_t).astype(BF16)
    for p in range(hw // LANES):
        lo, hi = p * LANES, (p + 1) * LANES
        qa_ref[:, 2 * lo:2 * lo + LANES] = q_all[:, lo:hi].astype(BF16)
        qa_ref[:, 2 * lo + LANES:2 * hi] = (
            q_all[:, hw + lo:hw + hi] * cos_t + q_all[:, 2 * hw + lo:2 * hw + hi] * sin_t).astype(BF16)
        ka_ref[:, 2 * lo:2 * lo + LANES] = k_nope[:, lo:hi].astype(BF16)
        ka_ref[:, 2 * lo + LANES:2 * hi] = k_rot

    qb_ref[...] = _dot(hb, wbig_ref[:, c0:c0 + hw]).astype(BF16)
    kb_ref[...] = _dot(hb, wbig_ref[:, c0 + hw:c0 + 2 * hw]).astype(BF16)
    vtb_ref[0] = (_dot_nt(wvbt_ref[...], hb) + one_ref[...]).astype(BF16)


def _proj(x2, pos2, mod3, g_mix, wbig, g_q, wq, g_kv, wkv, wvt, wvbt, seq, tm):
    t, d = x2.shape
    q_lora, kv_lora = g_q.shape[1], g_kv.shape[1]
    hw = wkv.shape[1]
    per_b = seq // tm
    row = lambda i: (i, 0)
    full = lambda i: (0, 0)
    inv = ROPE_BASE ** (-(np.arange(LANES) % (ROPE_DIM // 2)).astype(np.float32) / (ROPE_DIM // 2))
    sgn = np.where((np.arange(LANES) % ROPE_DIM) < ROPE_DIM // 2, -1.0, 1.0).astype(np.float32)
    ones_col = (np.arange(wvt.shape[0]) % VT_ROWS == HEAD_DIM).astype(np.float32).reshape(-1, 1)
    v_t = jax.ShapeDtypeStruct((t // seq, wvt.shape[0], seq), BF16)
    outs = [jax.ShapeDtypeStruct((t, 2 * hw), BF16), jax.ShapeDtypeStruct((t, 2 * hw), BF16), v_t,
            jax.ShapeDtypeStruct((t, hw), BF16), jax.ShapeDtypeStruct((t, hw), BF16), v_t,
            jax.ShapeDtypeStruct((t, d), BF16), jax.ShapeDtypeStruct((t, d), BF16)]
    v_t_spec = pl.BlockSpec((1, wvt.shape[0], tm), lambda i: (i // per_b, 0, i % per_b))
    out_specs = [v_t_spec if o is v_t else pl.BlockSpec((tm, o.shape[1]), row) for o in outs]
    return pl.pallas_call(
        functools.partial(_proj_kernel, q_lora=q_lora, kv_lora=kv_lora),
        grid=(t // tm,),
        in_specs=[pl.BlockSpec((tm, d), row),
                  pl.BlockSpec((1, 1, tm), lambda i: (i // per_b, 0, i % per_b)),
                  pl.BlockSpec((1,) + mod3.shape[1:], lambda i: (i // per_b, 0, 0)),
                  pl.BlockSpec(g_mix.shape, full),
                  pl.BlockSpec(wbig.shape, full),
                  pl.BlockSpec(g_q.shape, full),
                  pl.BlockSpec(wq.shape, full),
                  pl.BlockSpec(g_kv.shape, full),
                  pl.BlockSpec(wkv.shape, full),
                  pl.BlockSpec(wvt.shape, full),
                  pl.BlockSpec(wvbt.shape, full),
                  pl.BlockSpec(ones_col.shape, full),
                  pl.BlockSpec((LANES, 1), full),
                  pl.BlockSpec((LANES, 1), full)],
        out_specs=out_specs,
        out_shape=outs,
        compiler_params=_cparams(("parallel",)),
        name="proj",
    )(x2, pos2, mod3, g_mix, wbig, g_q, wq, g_kv, wkv, wvt, wvbt, jnp.asarray(ones_col),
      jnp.asarray(inv).reshape(LANES, 1), jnp.asarray(sgn).reshape(LANES, 1))


def _mla_kernel(q_ref, k_ref, vt_ref, o_ref, sa, sb, *, tq):
    i, nq = pl.program_id(2), pl.num_programs(2)
    lane = lax.broadcasted_iota(jnp.int32, (1, 2 * LANES), 1)
    first = (lane < HEAD_DIM) | ((lane >= LANES) & (lane < LANES + ROPE_DIM))
    second = ((lane >= HEAD_DIM) & (lane < LANES)) | ((lane >= LANES + ROPE_DIM) & (lane < LANES + 2 * ROPE_DIM))

    def head_queries(tile):
        q = q_ref[pl.ds(pl.multiple_of(tile * tq, tq), tq), :]
        zero = jnp.zeros_like(q)
        return (jnp.where(first, q, zero), jnp.where(second, q, zero))

    qs = head_queries(i)

    def scores(queries, blk, buf):
        k = k_ref[pl.ds(pl.multiple_of(blk * tq, tq), tq), :]
        for h, qh in enumerate(queries):
            buf[h] = _dot_nt(k, qh)

    def consume(blk, buf, state, mask):
        start = pl.multiple_of(blk * tq, tq)
        ps, ms, alphas = [], [], []
        for h, (m, _) in enumerate(state):
            st = buf[h]
            if mask is not None:
                st = jnp.where(mask, st, NEG_INF)
            m_new = jnp.maximum(m, jnp.max(st, axis=0, keepdims=True))
            ms.append(m_new)
            alphas.append(jnp.exp2(m - m_new))
            ps.append(jnp.exp2(st - m_new).astype(BF16))
        new = []
        for h, (p, m_new, alpha, (_, acc)) in enumerate(zip(ps, ms, alphas, state)):
            vt = vt_ref[0, h * VT_ROWS:(h + 1) * VT_ROWS, pl.ds(start, tq)]
            new.append((m_new, alpha * acc + _dot(vt, p)))
        return tuple(new)

    causal = (lax.broadcasted_iota(jnp.int32, (tq, tq), 0) // CHUNK
              <= lax.broadcasted_iota(jnp.int32, (tq, tq), 1) // CHUNK)
    following = jnp.minimum(i + 1, nq - 1)

    def tile(sx, sy):
        @pl.when(i == 0)
        def _():
            scores(qs, 0, sx)

        def two_blocks(jj, state):
            scores(qs, 2 * jj + 1, sy)
            state = consume(2 * jj, sx, state, None)
            scores(qs, 2 * jj + 2, sx)
            return consume(2 * jj + 1, sy, state, None)

        def even_tail(state):
            scores(head_queries(following), 0, sy)
            return consume(i, sx, state, causal)

        def odd_tail(state):
            scores(qs, i, sy)
            state = consume(i - 1, sx, state, None)
            scores(head_queries(following), 0, sx)
            return consume(i, sy, state, causal)

        init = tuple((jnp.full((1, tq), NEG_INF, F32), jnp.zeros((VT_ROWS, tq), F32)) for _ in range(2))
        state = lax.fori_loop(0, i // 4, lambda jj, st: two_blocks(2 * jj + 1, two_blocks(2 * jj, st)), init)
        state = lax.fori_loop(2 * (i // 4), i // 2, two_blocks, state)
        (_, a0), (_, a1) = lax.cond(i % 2 == 1, odd_tail, even_tail, state)
        out_t = jnp.concatenate([a[0:HEAD_DIM] * (1.0 / a[HEAD_DIM:HEAD_DIM + 1]) for a in (a0, a1)], axis=0)
        o_ref[...] = out_t.T.astype(o_ref.dtype)

    swapped = ((i + 1) // 2) % 2

    @pl.when(swapped == 0)
    def _():
        tile(sa, sb)

    @pl.when(swapped == 1)
    def _():
        tile(sb, sa)


def _mla(qa, ka, vt, bsz, seq, tq):
    t = qa.shape[0]
    pairs = qa.shape[1] // (2 * LANES)
    nq = seq // tq
    return pl.pallas_call(
        functools.partial(_mla_kernel, tq=tq),
        grid=(bsz, pairs, nq),
        in_specs=[pl.BlockSpec((seq, 2 * LANES), lambda b, p, i: (b, p)),
                  pl.BlockSpec((seq, 2 * LANES), lambda b, p, i: (b, p)),
                  pl.BlockSpec((1, 2 * VT_ROWS, seq), lambda b, p, i: (b, p, 0))],
        out_specs=pl.BlockSpec((tq, LANES), lambda b, p, i: (b * nq + i, p)),
        out_shape=jax.ShapeDtypeStruct((t, pairs * LANES), BF16),
        scratch_shapes=[pltpu.VMEM((2, tq, tq), F32)] * 2,
        compiler_params=_cparams(("parallel", "parallel", "arbitrary")),
        name="mla",
    )(qa, ka, vt)


def _band_kernel(q_ref, k_ref, vt_ref, ring_ref, o_ref, bias_ref, *, tq, nblk):
    lane = lax.broadcasted_iota(jnp.int32, (1, LANES), 1)
    nk = nblk * tq

    @pl.when((pl.program_id(1) == 0) & (pl.program_id(2) == 0))
    def _():
        kc = lax.broadcasted_iota(jnp.int32, (nk, tq), 0) // CHUNK - ((nblk - 1) * tq // CHUNK - LEFT_CHUNKS)
        qc = lax.broadcasted_iota(jnp.int32, (nk, tq), 1) // CHUNK
        band = (kc >= qc) & (kc <= qc + LEFT_CHUNKS)
        for h in range(2):
            ring = jnp.broadcast_to(ring_ref[0, h:h + 1, :], (nk, ring_ref.shape[2]))
            rotated = pltpu.roll(ring, 0, 1, stride=1, stride_axis=0)
            bias_ref[h] = jnp.where(band, rotated[:, 0:tq], NEG_INF)

    def scores(n, edge):
        tile = first + n
        q = q_ref[n * tq:(n + 1) * tq, :]
        zero = jnp.zeros_like(q)
        qs = (jnp.where(lane < HEAD_DIM, q, zero), jnp.where(lane >= HEAD_DIM, q, zero))
        blks = [tile - (nblk - 1) + j for j in range(nblk)]
        starts = [pl.multiple_of((jnp.maximum(blk, 0) if edge else blk) * tq, tq) for blk in blks]
        ks = [k_ref[pl.ds(s, tq), :] for s in starts]
        sts = [[_dot_nt(ks[j], qh) + bias_ref[h, j * tq:(j + 1) * tq, :] for j in range(nblk)]
               for h, qh in enumerate(qs)]
        if edge:
            sts = [[jnp.where(blks[j] >= 0, st, NEG_INF) for j, st in enumerate(row)] for row in sts]
        return starts, sts

    def finish(starts, sts):
        outs = []
        for h in range(2):
            m = functools.reduce(jnp.maximum, [jnp.max(st, axis=0, keepdims=True) for st in sts[h]])
            acc = functools.reduce(jnp.add, [
                _dot(vt_ref[0, h * VT_ROWS:(h + 1) * VT_ROWS, pl.ds(starts[j], tq)], jnp.exp2(st - m).astype(BF16))
                for j, st in enumerate(sts[h])])
            outs.append(acc[0:HEAD_DIM] * (1.0 / acc[HEAD_DIM:HEAD_DIM + 1]))
        return jnp.concatenate(outs, axis=0).T

    tiles_per_step = q_ref.shape[0] // tq
    assert tiles_per_step >= nblk - 1
    first = pl.program_id(2) * tiles_per_step

    def tiles(edge):
        pending = scores(0, edge)
        for n in range(tiles_per_step):
            upcoming = scores(n + 1, edge) if n + 1 < tiles_per_step else None
            o_ref[n * tq:(n + 1) * tq, :] = finish(*pending).astype(o_ref.dtype)
            pending = upcoming

    for edge in (True, False):
        @pl.when((pl.program_id(2) == 0) == edge)
        def _():
            tiles(edge)


def _band_ring(rel_table, tq, nblk):
    ring = (nblk + 1) * tq
    x = np.arange(ring)
    query_minus_key = np.where(x < tq, x, x - ring) + (nblk - 1) * tq
    idx = np.clip(query_minus_key, -MAX_REL, MAX_REL) + MAX_REL
    heads = rel_table.shape[0]
    return (rel_table.astype(F32)[:, idx] * LOG2_E).reshape(heads // 2, 2, ring)


def _band(qb, kb, vtb, ring, bsz, seq, tq, nblk, tiles_per_step):
    t, hw = qb.shape
    pairs = hw // LANES
    rows = tq * tiles_per_step
    nq = seq // rows
    return pl.pallas_call(
        functools.partial(_band_kernel, tq=tq, nblk=nblk),
        grid=(pairs, bsz, nq),
        in_specs=[pl.BlockSpec((rows, LANES), lambda p, b, i: (b * nq + i, p)),
                  pl.BlockSpec((seq, LANES), lambda p, b, i: (b, p)),
                  pl.BlockSpec((1, 2 * VT_ROWS, seq), lambda p, b, i: (b, p, 0)),
                  pl.BlockSpec((1,) + ring.shape[1:], lambda p, b, i: (p, 0, 0))],
        out_specs=pl.BlockSpec((rows, LANES), lambda p, b, i: (b * nq + i, p)),
        out_shape=jax.ShapeDtypeStruct((t, hw), BF16),
        scratch_shapes=[pltpu.VMEM((2, nblk * tq, tq), F32)],
        compiler_params=_cparams(("arbitrary", "arbitrary", "arbitrary")),
        name="band",
    )(qb, kb, vtb, ring)


def _merge_kernel(oa_ref, ob_ref, ga_ref, gb_ref, x_ref, mod_ref, woa_ref, wob_ref, wout_ref, gffn_ref,
                  wr_ref, br_ref, x1_ref, h2x_ref, rt_ref, cnt_ref):
    @pl.when(pl.program_id(0) == 0)
    def _():
        cnt_ref[...] = jnp.zeros_like(cnt_ref)

    merged = (ga_ref[...].astype(F32) * _dot(oa_ref[...], woa_ref[...])
              + gb_ref[...].astype(F32) * _dot(ob_ref[...], wob_ref[...]))
    mix = _dot(merged.astype(BF16), wout_ref[...])
    x1 = x_ref[...] + mod_ref[0, 2:3, :] * mix
    x1_ref[...] = x1
    h2 = _rms(x1, gffn_ref[...]) * (1.0 + mod_ref[0, 4:5, :]) + mod_ref[0, 3:4, :]

    h_hi = h2.astype(BF16)
    h_lo = (h2 - h_hi.astype(F32)).astype(BF16)
    r_hi = _dot_nt(wr_ref[...], h_hi)
    logits = (r_hi[0:LANES] + _dot_nt(wr_ref[0:LANES, :], h_lo) + r_hi[LANES:2 * LANES]) + br_ref[...]
    blk = EXPERTS_PER_GROUP
    sub = lax.broadcasted_iota(jnp.int32, (blk, logits.shape[1]), 0)
    far = jnp.int32(blk)
    is_g = sub < N_GROUPS
    gl = jnp.where(is_g, logits[0:blk], NEG_INF)
    g_max = jnp.max(gl, axis=0, keepdims=True)
    g_w = 1.0 / jnp.sum(jnp.where(is_g, jnp.exp(gl - g_max), 0.0), axis=0, keepdims=True)
    g_idx = jnp.min(jnp.where(gl == g_max, sub, far), axis=0, keepdims=True)
    el = logits[blk:2 * blk]
    for g in range(1, N_GROUPS):
        el = jnp.where(g_idx == g, logits[(g + 1) * blk:(g + 2) * blk], el)
    e1 = jnp.max(el, axis=0, keepdims=True)
    i1 = jnp.min(jnp.where(el == e1, sub, far), axis=0, keepdims=True)
    el2 = jnp.where(sub == i1, NEG_INF, el)
    e2 = jnp.max(el2, axis=0, keepdims=True)
    i2 = jnp.min(jnp.where(el2 == e2, sub, far), axis=0, keepdims=True)
    ratio = jnp.exp(e2 - e1)
    w1 = g_w / (1.0 + ratio)
    w2 = g_w * ratio / (1.0 + ratio)
    e_lo, e_hi = jnp.minimum(i1, i2), jnp.maximum(i1, i2)
    cls = (g_idx * PAIRS_PER_GROUP + jnp.right_shift(e_lo * (2 * EXPERTS_PER_GROUP - 1 - e_lo), 1)
           + (e_hi - e_lo - 1))
    first_is_lo = i1 < i2
    w_lo, w_hi = jnp.where(first_is_lo, w1, w2), jnp.where(first_is_lo, w2, w1)
    route_t = (jnp.where(sub == ROUTE_CLS, cls.astype(F32), 0.0) + jnp.where(sub == ROUTE_W_LO, w_lo, 0.0)
               + jnp.where(sub == ROUTE_W_HI, w_hi, 0.0))
    rt_ref[...] = route_t
    member = _class_onehot(cls.astype(F32)).astype(BF16)
    cnt_ref[...] += _dot(member, jnp.ones((member.shape[1], LANES), BF16))[:, 0:1]
    d = h2.shape[1]
    h2x_ref[:, 0:d] = h2
    h2x_ref[:, d:d + LANES] = jnp.concatenate(
        [route_t, jnp.zeros((LANES - ROUTE_ROWS, route_t.shape[1]), F32)], axis=0).T


def _merge(oa, ob, ga, gb, x2, mod3, woa, wob, wout, g_ffn, wr, br, seq, tm):
    t, d = x2.shape
    hw = oa.shape[1]
    per_b = seq // tm
    row = lambda i: (i, 0)
    full = lambda i: (0, 0)
    return pl.pallas_call(
        _merge_kernel,
        grid=(t // tm,),
        in_specs=[pl.BlockSpec((tm, hw), row), pl.BlockSpec((tm, hw), row),
                  pl.BlockSpec((tm, d), row), pl.BlockSpec((tm, d), row),
                  pl.BlockSpec((tm, d), row),
                  pl.BlockSpec((1,) + mod3.shape[1:], lambda i: (i // per_b, 0, 0)),
                  pl.BlockSpec(woa.shape, full), pl.BlockSpec(wob.shape, full),
                  pl.BlockSpec(wout.shape, full), pl.BlockSpec(g_ffn.shape, full),
                  pl.BlockSpec(wr.shape, full), pl.BlockSpec(br.shape, full)],
        out_specs=[pl.BlockSpec((tm, d), row), pl.BlockSpec((tm, d + LANES), row),
                   pl.BlockSpec((ROUTE_ROWS, tm), lambda i: (0, i)),
                   pl.BlockSpec((LANES, 1), lambda i: (0, 0))],
        out_shape=[jax.ShapeDtypeStruct((t, d), F32), jax.ShapeDtypeStruct((t, d + LANES), F32),
                   jax.ShapeDtypeStruct((ROUTE_ROWS, t), F32), jax.ShapeDtypeStruct((LANES, 1), F32)],
        compiler_params=_cparams(("arbitrary",)),
        name="merge",
    )(oa, ob, ga, gb, x2, mod3, woa, wob, wout, g_ffn, wr, br)


def _class_onehot(cls_row):
    return lax.broadcasted_iota(jnp.int32, (LANES, cls_row.shape[1]), 0).astype(F32) == cls_row


def _plan_kernel(rt_ref, cnt_ref, pos_ref, carry_ref, off_ref, *, tc):
    i = pl.program_id(0)
    onehot = _class_onehot(rt_ref[ROUTE_CLS:ROUTE_CLS + 1, :])
    per_class = jnp.sum(onehot.astype(F32), axis=1, keepdims=True)

    @pl.when(i == 0)
    def _():
        padded = jnp.floor((cnt_ref[...] + (MOE_ROWS - 1)) * (1.0 / MOE_ROWS)) * MOE_ROWS
        hi = jnp.floor(padded * (1.0 / 256.0))
        digits = [jnp.broadcast_to(v, (LANES, LANES)).astype(BF16) for v in (hi, padded - 256.0 * hi)]
        below = (lax.broadcasted_iota(jnp.int32, (LANES, LANES), 1)
                 < lax.broadcasted_iota(jnp.int32, (LANES, LANES), 0)).astype(BF16)
        off = 256.0 * _dot(below, digits[0]) + _dot(below, digits[1])
        off_ref[...] = off[:, 0:1]
        carry_ref[...] = jnp.zeros_like(carry_ref)

    earlier = (lax.broadcasted_iota(jnp.int32, (tc, tc), 0)
               < lax.broadcasted_iota(jnp.int32, (tc, tc), 1)).astype(BF16)
    rank = _dot(onehot.astype(BF16), earlier)
    row = rank + (carry_ref[...] + off_ref[...])
    pos_ref[...] = jnp.sum(jnp.where(onehot, row, 0.0), axis=0, keepdims=True).astype(jnp.int32)
    carry_ref[...] += per_class


def _plan(route_t, counts, tc):
    t = route_t.shape[1]
    return pl.pallas_call(
        functools.partial(_plan_kernel, tc=tc),
        grid=(t // tc,),
        in_specs=[pl.BlockSpec((ROUTE_ROWS, tc), lambda i: (0, i)),
                  pl.BlockSpec((LANES, 1), lambda i: (0, 0))],
        out_specs=pl.BlockSpec((1, tc), lambda i: (0, i)),
        out_shape=jax.ShapeDtypeStruct((1, t), jnp.int32),
        scratch_shapes=[pltpu.VMEM((LANES, 1), F32), pltpu.VMEM((LANES, 1), F32)],
        compiler_params=_cparams(("arbitrary",)),
        name="plan",
    )(route_t, counts)


def _tile_tables(counts, n_tiles):
    pairs = [(lo, hi) for lo in range(EXPERTS_PER_GROUP) for hi in range(lo + 1, EXPERTS_PER_GROUP)]
    exp_lo = np.array([g * EXPERTS_PER_GROUP + lo for g in range(N_GROUPS) for lo, _ in pairs], np.int32)
    exp_hi = np.array([g * EXPERTS_PER_GROUP + hi for g in range(N_GROUPS) for _, hi in pairs], np.int32)
    cnt = counts.reshape(-1)[:N_CLASSES].astype(jnp.int32)
    tiles_per_class = (cnt + (MOE_ROWS - 1)) // MOE_ROWS
    tile_end = jnp.cumsum(tiles_per_class)
    n_used = tile_end[-1]
    j = jnp.arange(n_tiles, dtype=jnp.int32)
    tile_cls = jnp.sum(tile_end[None, :] <= jnp.minimum(j, n_used - 1)[:, None], axis=1, dtype=jnp.int32)
    valid_end = (tile_end - tiles_per_class) * MOE_ROWS + cnt
    of_class = tile_cls[:, None] == jnp.arange(N_CLASSES, dtype=jnp.int32)[None, :]

    def per_tile(table):
        return jnp.sum(jnp.where(of_class, table[None, :], 0), axis=1, dtype=jnp.int32)

    units = MOE_ROWS // ZERO_ROWS
    unit_end = (jnp.arange(n_tiles * units, dtype=jnp.int32) + 1) * ZERO_ROWS
    zero_fill = (unit_end.reshape(n_tiles, units) > per_tile(valid_end)[:, None]).astype(jnp.int32).reshape(-1)
    return (per_tile(jnp.asarray(exp_lo)), per_tile(jnp.asarray(exp_hi)), n_used.reshape(1).astype(jnp.int32),
            zero_fill)


def _row_copy(src, src_row, dst, dst_row, n, sem):
    return pltpu.make_async_copy(src.at[pl.ds(src_row, n)], dst.at[pl.ds(dst_row, n)], sem)


def _dispatch_kernel(pos_ref, zf_ref, h_ref, *refs, td, n_tiles, n_weights):
    w_in, hs_ref, w_out = refs[:n_weights], refs[n_weights], refs[n_weights + 1:2 * n_weights + 1]
    zero_ref, sem = refs[2 * n_weights + 1:]
    i = pl.program_id(0)

    @pl.when(i == 0)
    def _():
        zero_ref[...] = jnp.zeros_like(zero_ref)

        def fill(j, n):
            @pl.when(zf_ref[j] > 0)
            def _():
                _row_copy(zero_ref, 0, hs_ref, pl.multiple_of(j * ZERO_ROWS, ZERO_ROWS), ZERO_ROWS, sem).start()
            return n + zf_ref[j]

        n_fill = lax.fori_loop(0, n_tiles * (MOE_ROWS // ZERO_ROWS), fill, jnp.int32(0))

        def drain(_, carry):
            _row_copy(zero_ref, 0, hs_ref, 0, ZERO_ROWS, sem).wait()
            return carry

        lax.fori_loop(0, n_fill, drain, 0)

    base = i * td

    for r in range(td):
        _row_copy(h_ref, r, hs_ref, pos_ref[base + r], 1, sem).start()
    for src, dst in zip(w_in, w_out):
        dst[...] = src[...].astype(dst.dtype)
    _row_copy(h_ref, 0, hs_ref, 0, td, sem).wait()


def _dispatch(h2x, pos, zero_fill, n_tiles, td, weights):
    t, w = h2x.shape
    steps = t // td
    flat = [a.reshape(-1, a.shape[-1]) for a in weights]
    assert all(a.shape[0] % (steps * 16) == 0 for a in flat)
    w_specs = [pl.BlockSpec((a.shape[0] // steps, a.shape[1]), lambda i, pos, zf: (i, 0)) for a in flat]
    outs = pl.pallas_call(
        functools.partial(_dispatch_kernel, td=td, n_tiles=n_tiles, n_weights=len(flat)),
        grid_spec=pltpu.PrefetchScalarGridSpec(
            num_scalar_prefetch=2,
            grid=(steps,),
            in_specs=[pl.BlockSpec((td, w), lambda i, pos, zf: (i, 0))] + w_specs,
            out_specs=[pl.BlockSpec(memory_space=pl.ANY)] + w_specs,
            scratch_shapes=[pltpu.VMEM((ZERO_ROWS, w), F32), pltpu.SemaphoreType.DMA(())]),
        out_shape=[jax.ShapeDtypeStruct((n_tiles * MOE_ROWS, w), F32)]
        + [jax.ShapeDtypeStruct(a.shape, BF16) for a in flat],
        compiler_params=_cparams(("arbitrary",)),
        name="dispatch",
    )(pos, zero_fill, h2x, *flat)
    return outs[0], [o.reshape(a.shape) for o, a in zip(outs[1:], weights)]


def _moe_kernel(elo_ref, ehi_ref, nu_ref, hs_ref, wg_lo, wu_lo, wd_lo, wg_hi, wu_hi, wd_hi, y_ref):
    j = pl.program_id(0)
    d = y_ref.shape[1]

    @pl.when(j < nu_ref[0])
    def _():
        h = hs_ref[:, 0:d].astype(BF16)
        ups = [(_dot(h, wg[0]), _dot(h, wu[0])) for wg, wu in ((wg_lo, wu_lo), (wg_hi, wu_hi))]
        hids = [((a * jax.nn.sigmoid(a)) * u * hs_ref[:, d + lane:d + lane + 1]).astype(BF16)
                for (a, u), lane in zip(ups, (ROUTE_W_LO, ROUTE_W_HI))]
        y_ref[...] = _dot(hids[0], wd_lo[0]) + _dot(hids[1], wd_hi[0])

    @pl.when(j >= nu_ref[0])
    def _():
        y_ref[...] = jnp.zeros_like(y_ref)


def _moe(hs, exp_lo, exp_hi, n_used, wg, wu, wd):
    n_tiles = hs.shape[0] // MOE_ROWS
    _, d, ff = wg.shape
    up = lambda sel: pl.BlockSpec((1, d, ff), lambda j, elo, ehi, nu: ((elo, ehi)[sel][j], 0, 0))
    down = lambda sel: pl.BlockSpec((1, ff, d), lambda j, elo, ehi, nu: ((elo, ehi)[sel][j], 0, 0))
    return pl.pallas_call(
        _moe_kernel,
        grid_spec=pltpu.PrefetchScalarGridSpec(
            num_scalar_prefetch=3,
            grid=(n_tiles,),
            in_specs=[pl.BlockSpec((MOE_ROWS, hs.shape[1]),
                                   lambda j, elo, ehi, nu: (jnp.minimum(j, nu[0] - 1), 0)),
                      up(0), up(0), down(0), up(1), up(1), down(1)],
            out_specs=pl.BlockSpec((MOE_ROWS, d), lambda j, elo, ehi, nu: (j, 0))),
        out_shape=jax.ShapeDtypeStruct((n_tiles * MOE_ROWS, d), F32),
        compiler_params=_cparams(("arbitrary",)),
        name="moe",
    )(exp_lo, exp_hi, n_used, hs, wg, wu, wd, wg, wu, wd)


def _final_kernel(pos_ref, x1_ref, mod_ref, gfin_ref, y_ref, o_ref, rows_a, rows_b, sem, *, tf):
    i, n = pl.program_id(0), pl.num_programs(0)

    def fetch(step, rows, slot):
        for r in range(tf):
            _row_copy(y_ref, pos_ref[step * tf + r], rows, r, 1, sem.at[slot]).start()

    def wait(rows, slot):
        _row_copy(y_ref, 0, rows, 0, tf, sem.at[slot]).wait()

    def step(cur, cur_slot, nxt, nxt_slot):
        @pl.when(i == 0)
        def _():
            fetch(0, cur, cur_slot)

        wait(cur, cur_slot)
        fetch(jnp.minimum(i + 1, n - 1), nxt, nxt_slot)
        x2 = x1_ref[...] + mod_ref[0, 5:6, :] * cur[...]
        o_ref[...] = _rms(x2, gfin_ref[...])

        @pl.when(i == n - 1)
        def _():
            wait(nxt, nxt_slot)

    @pl.when(i % 2 == 0)
    def _():
        step(rows_a, 0, rows_b, 1)

    @pl.when(i % 2 == 1)
    def _():
        step(rows_b, 1, rows_a, 0)


def _final(y, pos, x1, mod3, g_final, seq, tf):
    t, d = x1.shape
    per_b = seq // tf
    return pl.pallas_call(
        functools.partial(_final_kernel, tf=tf),
        grid_spec=pltpu.PrefetchScalarGridSpec(
            num_scalar_prefetch=1,
            grid=(t // tf,),
            in_specs=[pl.BlockSpec((tf, d), lambda i, pos: (i, 0)),
                      pl.BlockSpec((1,) + mod3.shape[1:], lambda i, pos: (i // per_b, 0, 0)),
                      pl.BlockSpec((1, d), lambda i, pos: (0, 0)),
                      pl.BlockSpec(memory_space=pl.ANY)],
            out_specs=pl.BlockSpec((tf, d), lambda i, pos: (i, 0)),
            scratch_shapes=[pltpu.VMEM((tf, d), F32), pltpu.VMEM((tf, d), F32), pltpu.SemaphoreType.DMA((2,))]),
        out_shape=jax.ShapeDtypeStruct((t, d), F32),
        compiler_params=_cparams(("arbitrary",)),
        name="final",
    )(pos, x1, mod3, g_final, y)


def _layout_weights(w_in, w_uq, w_uk, w_uv, heads, q_lora, kv_lora, d):
    hw = heads * HEAD_DIM
    cuts = np.cumsum([q_lora, kv_lora, ROPE_DIM, hw, hw, hw, d, d])
    w_qlat, w_kvlat, w_kr, w_qb, w_kb, w_vb, w_ga, w_gb = jnp.split(w_in, [int(v) for v in cuts[:-1]], axis=1)
    swap = np.concatenate([np.arange(ROPE_DIM // 2, ROPE_DIM), np.arange(ROPE_DIM // 2)])
    w_kr_sw = w_kr[:, swap]
    kr_blk = jnp.concatenate([w_kr, w_kr, w_kr_sw, w_kr_sw], axis=1)
    band_scale = HEAD_DIM ** -0.5 * LOG2_E
    wbig = jnp.concatenate([w_qlat, w_kvlat, kr_blk, w_qb * band_scale, w_kb, w_ga, w_gb], axis=1)

    def transposed_values(w, k):
        return jnp.pad(w.reshape(k, heads, HEAD_DIM).transpose(1, 2, 0),
                       ((0, 0), (0, VT_ROWS - HEAD_DIM), (0, 0))).reshape(heads * VT_ROWS, k)

    qk = HEAD_DIM + ROPE_DIM
    wq3 = w_uq.reshape(q_lora, heads, qk) * (qk ** -0.5 * LOG2_E)
    nope = wq3[:, :, :HEAD_DIM].reshape(q_lora, hw)
    rope = wq3[:, :, HEAD_DIM:]
    pad = jnp.zeros((q_lora, heads // 2, LANES - 2 * ROPE_DIM), w_uq.dtype)

    def lay(r):
        return jnp.concatenate([r.reshape(q_lora, heads // 2, 2 * ROPE_DIM), pad], axis=2).reshape(q_lora, hw)

    wq = jnp.concatenate([nope, lay(rope), lay(rope[:, :, swap])], axis=1)
    return (wbig.astype(BF16), wq.astype(BF16), w_uk.astype(BF16), transposed_values(w_uv, kv_lora).astype(BF16),
            transposed_values(w_vb, d).astype(BF16))


def kernel(x, c, positions, w_ada, b_ada, g_mix, w_in, g_q, w_uq, g_kv, w_uk, w_uv, rel_bias, w_oa, w_ob,
           w_out, g_ffn, w_rg, b_rg, w_re, b_re, w_gate, w_up, w_down, g_final):
    bsz, seq, d = x.shape
    depth = w_ada.shape[0]
    t = bsz * seq
    heads = rel_bias.shape[1]
    q_lora, kv_lora = g_q.shape[1], g_kv.shape[1]
    tq, tq_mla = TQ_BAND, TQ_MLA
    nblk = -(-LEFT_CHUNKS * CHUNK // tq) + 1
    assert seq % tq == 0 and seq % tq_mla == 0 and tq % CHUNK == 0 and heads % 2 == 0
    assert ROUTE_ROWS == EXPERTS_PER_GROUP and N_GROUPS <= EXPERTS_PER_GROUP

    x2 = x.reshape(t, d)
    pos2 = positions.reshape(bsz, 1, seq)
    out = x2
    for l in range(depth):
        mod3 = _ada(c, w_ada[l], b_ada[l]).reshape(bsz, -1, d)
        wbig, wq, wkv, wvt, wvbt = _layout_weights(w_in[l], w_uq[l], w_uk[l], w_uv[l], heads, q_lora, kv_lora, d)
        qa, ka, vt, qb, kb, vtb, ga, gb = _proj(
            x2, pos2, mod3, g_mix[l].reshape(1, d), wbig, g_q[l].reshape(1, -1), wq, g_kv[l].reshape(1, -1), wkv,
            wvt, wvbt, seq, tm=TM_PROJ)
        oa = _mla(qa, ka, vt, bsz, seq, tq_mla)
        ob = _band(qb, kb, vtb, _band_ring(rel_bias[l], tq, nblk), bsz, seq, tq, nblk,
                   tiles_per_step=BAND_TILES_PER_STEP)

        pad_g = EXPERTS_PER_GROUP - N_GROUPS
        n_route = (N_GROUPS + 1) * EXPERTS_PER_GROUP
        w_r = jnp.concatenate([w_rg[l], jnp.zeros((d, pad_g), F32), w_re[l],
                               jnp.zeros((d, LANES - n_route), F32)], axis=1).T
        b_r = jnp.concatenate([b_rg[l], jnp.zeros((pad_g,), F32), b_re[l],
                               jnp.zeros((LANES - n_route,), F32)]).reshape(LANES, 1)
        w_r_hi = w_r.astype(BF16)
        w_r_lo = (w_r - w_r_hi.astype(F32)).astype(BF16)
        x1, h2x, route_t, counts = _merge(oa, ob, ga, gb, x2, mod3, w_oa[l].astype(BF16), w_ob[l].astype(BF16),
                                          w_out[l].astype(BF16), g_ffn[l].reshape(1, d),
                                          jnp.concatenate([w_r_hi, w_r_lo], axis=0), b_r, seq, tm=TM_MERGE)
        n_tiles = t // MOE_ROWS + N_CLASSES
        pos = _plan(route_t, counts, tc=TC_PLAN).reshape(t)
        exp_lo, exp_hi, n_used, zero_fill = _tile_tables(counts, n_tiles)
        hs, expert_weights = _dispatch(h2x, pos, zero_fill, n_tiles, TD_DISPATCH, (w_gate[l], w_up[l], w_down[l]))
        y = _moe(hs, exp_lo, exp_hi, n_used, *expert_weights)
        assert l == depth - 1, "multi-layer stacks need an un-normalised residual output"
        out = _final(y, pos, x1, mod3, g_final.reshape(1, d), seq, tf=TF_FINAL)
    return out.reshape(bsz, seq, d)
```

```python
import functools

import numpy as np
import jax
import jax.numpy as jnp
from jax import lax
from jax.experimental import pallas as pl
from jax.experimental.pallas import tpu as pltpu

F32 = jnp.float32
BF16 = jnp.bfloat16

NORM_EPS = 1e-6
NEG_INF = -1e30
ROPE_BASE = 10000.0
CHUNK = 64
LEFT_CHUNKS = 8
MAX_REL = 256
N_GROUPS = 4
EXPERTS_PER_GROUP = 8
PAIRS_PER_GROUP = EXPERTS_PER_GROUP * (EXPERTS_PER_GROUP - 1) // 2
N_CLASSES = N_GROUPS * PAIRS_PER_GROUP
MOE_ROWS = 256
ZERO_ROWS = 64
ROUTE_CLS, ROUTE_W_LO, ROUTE_W_HI = 0, 1, 2
ROUTE_ROWS = 8
LANES = 128
HEAD_DIM = 64
ROPE_DIM = 32
VT_ROWS = HEAD_DIM + 16
LOG2_E = 1.4426950408889634

VMEM_LIMIT = 56 * 1024 * 1024

TM_PROJ = 512
TM_MERGE = 512
TQ_MLA = 512
TQ_BAND = 256
BAND_TILES_PER_STEP = 8
TC_PLAN = 1024
TD_DISPATCH = 2048
TF_FINAL = 512


def _cparams(sem):
    return pltpu.CompilerParams(dimension_semantics=sem, vmem_limit_bytes=VMEM_LIMIT)


def _dot(a, b):
    return jnp.dot(a, b, preferred_element_type=F32)


def _dot_nt(a, b):
    return lax.dot_general(a, b, (((1,), (1,)), ((), ())), preferred_element_type=F32)


def _rms(x, g):
    return x * lax.rsqrt(jnp.mean(x * x, axis=-1, keepdims=True) + NORM_EPS) * g


def _ada_kernel(c_ref, w_ref, b_ref, o_ref):
    c = c_ref[...]
    o_ref[...] = _dot(c * jax.nn.sigmoid(c), w_ref[...]) + b_ref[...]


def _ada(c, w_ada, b_ada):
    bsz, d = c.shape
    n = w_ada.shape[1]
    return pl.pallas_call(
        _ada_kernel,
        grid=(n // d,),
        in_specs=[pl.BlockSpec((bsz, d), lambda j: (0, 0)),
                  pl.BlockSpec((d, d), lambda j: (0, j)),
                  pl.BlockSpec((1, d), lambda j: (0, j))],
        out_specs=pl.BlockSpec((bsz, d), lambda j: (0, j)),
        out_shape=jax.ShapeDtypeStruct((bsz, n), F32),
        compiler_params=_cparams(("arbitrary",)),
        name="ada",
    )(c, w_ada, b_ada.reshape(1, n))


def _proj_kernel(x_ref, pos_ref, mod_ref, gmix_ref, wbig_ref, gq_ref, wq_ref, gkv_ref, wkv_ref,
                 wvt_ref, wvbt_ref, one_ref, inv_ref, sgn_ref,
                 qa_ref, ka_ref, vt_ref, qb_ref, kb_ref, vtb_ref, ga_ref, gb_ref, *, q_lora, kv_lora):
    x = x_ref[...]
    h = _rms(x, gmix_ref[...]) * (1.0 + mod_ref[0, 1:2, :]) + mod_ref[0, 0:1, :]
    hb = h.astype(BF16)

    c0 = q_lora + kv_lora + LANES
    head = _dot(hb, wbig_ref[:, 0:c0])
    q_lat = head[:, 0:q_lora]
    kv_lat = head[:, q_lora:q_lora + kv_lora]
    kr_blk = head[:, q_lora + kv_lora:c0]
    hw = qb_ref.shape[1]
    d = x.shape[1]
    g0 = c0 + 2 * hw
    ga_ref[...] = jax.nn.sigmoid(_dot(hb, wbig_ref[:, g0:g0 + d])).astype(BF16)

    ang = inv_ref[...] * pos_ref[0].astype(F32)
    live = lax.broadcasted_iota(jnp.int32, (LANES, 1), 0) < 2 * ROPE_DIM
    cos_t = jnp.where(live, jnp.cos(ang), 0.0).T
    sin_t = jnp.where(live, jnp.sin(ang) * sgn_ref[...], 0.0).T

    qn = _rms(q_lat, gq_ref[...]).astype(BF16)
    q_all = _dot(qn, wq_ref[...])
    kvn = _rms(kv_lat, gkv_ref[...]).astype(BF16)
    k_nope = _dot(kvn, wkv_ref[...])
    vt_ref[0] = (_dot_nt(wvt_ref[...], kvn) + one_ref[...]).astype(BF16)
    gb_ref[...] = jax.nn.sigmoid(_dot(hb, wbig_ref[:, g0 + d:g0 + 2 * d])).astype(BF16)
    k_rot = (kr_blk * cos_t + pltpu.roll(kr_blk, LANES // 2, 1) * sin_t).astype(BF16)
    for p in range(hw // LANES):
        lo, hi = p * LANES, (p + 1) * LANES
        qa_ref[:, 2 * lo:2 * lo + LANES] = q_all[:, lo:hi].astype(BF16)
        qa_ref[:, 2 * lo + LANES:2 * hi] = (
            q_all[:, hw + lo:hw + hi] * cos_t + q_all[:, 2 * hw + lo:2 * hw + hi] * sin_t).astype(BF16)
        ka_ref[:, 2 * lo:2 * lo + LANES] = k_nope[:, lo:hi].astype(BF16)
        ka_ref[:, 2 * lo + LANES:2 * hi] = k_rot

    qb_ref[...] = _dot(hb, wbig_ref[:, c0:c0 + hw]).astype(BF16)
    kb_ref[...] = _dot(hb, wbig_ref[:, c0 + hw:c0 + 2 * hw]).astype(BF16)
    vtb_ref[0] = (_dot_nt(wvbt_ref[...], hb) + one_ref[...]).astype(BF16)


def _proj(x2, pos2, mod3, g_mix, wbig, g_q, wq, g_kv, wkv, wvt, wvbt, seq, tm):
    t, d = x2.shape
    q_lora, kv_lora = g_q.shape[1], g_kv.shape[1]
    hw = wkv.shape[1]
    per_b = seq // tm
    row = lambda i: (i, 0)
    full = lambda i: (0, 0)
    inv = ROPE_BASE ** (-(np.arange(LANES) % (ROPE_DIM // 2)).astype(np.float32) / (ROPE_DIM // 2))
    sgn = np.where((np.arange(LANES) % ROPE_DIM) < ROPE_DIM // 2, -1.0, 1.0).astype(np.float32)
    ones_col = (np.arange(wvt.shape[0]) % VT_ROWS == HEAD_DIM).astype(np.float32).reshape(-1, 1)
    v_t = jax.ShapeDtypeStruct((t // seq, wvt.shape[0], seq), BF16)
    outs = [jax.ShapeDtypeStruct((t, 2 * hw), BF16), jax.ShapeDtypeStruct((t, 2 * hw), BF16), v_t,
            jax.ShapeDtypeStruct((t, hw), BF16), jax.ShapeDtypeStruct((t, hw), BF16), v_t,
            jax.ShapeDtypeStruct((t, d), BF16), jax.ShapeDtypeStruct((t, d), BF16)]
    v_t_spec = pl.BlockSpec((1, wvt.shape[0], tm), lambda i: (i // per_b, 0, i % per_b))
    out_specs = [v_t_spec if o is v_t else pl.BlockSpec((tm, o.shape[1]), row) for o in outs]
    return pl.pallas_call(
        functools.partial(_proj_kernel, q_lora=q_lora, kv_lora=kv_lora),
        grid=(t // tm,),
        in_specs=[pl.BlockSpec((tm, d), row),
                  pl.BlockSpec((1, 1, tm), lambda i: (i // per_b, 0, i % per_b)),
                  pl.BlockSpec((1,) + mod3.shape[1:], lambda i: (i // per_b, 0, 0)),
                  pl.BlockSpec(g_mix.shape, full),
                  pl.BlockSpec(wbig.shape, full),
                  pl.BlockSpec(g_q.shape, full),
                  pl.BlockSpec(wq.shape, full),
                  pl.BlockSpec(g_kv.shape, full),
                  pl.BlockSpec(wkv.shape, full),
                  pl.BlockSpec(wvt.shape, full),
                  pl.BlockSpec(wvbt.shape, full),
                  pl.BlockSpec(ones_col.shape, full),
                  pl.BlockSpec((LANES, 1), full),
                  pl.BlockSpec((LANES, 1), full)],
        out_specs=out_specs,
        out_shape=outs,
        compiler_params=_cparams(("parallel",)),
        name="proj",
    )(x2, pos2, mod3, g_mix, wbig, g_q, wq, g_kv, wkv, wvt, wvbt, jnp.asarray(ones_col),
      jnp.asarray(inv).reshape(LANES, 1), jnp.asarray(sgn).reshape(LANES, 1))


def _mla_kernel(q_ref, k_ref, vt_ref, o_ref, sa, sb, *, tq):
    i, nq = pl.program_id(2), pl.num_programs(2)
    lane = lax.broadcasted_iota(jnp.int32, (1, 2 * LANES), 1)
    first = (lane < HEAD_DIM) | ((lane >= LANES) & (lane < LANES + ROPE_DIM))
    second = ((lane >= HEAD_DIM) & (lane < LANES)) | ((lane >= LANES + ROPE_DIM) & (lane < LANES + 2 * ROPE_DIM))

    def head_queries(tile):
        q = q_ref[pl.ds(pl.multiple_of(tile * tq, tq), tq), :]
        zero = jnp.zeros_like(q)
        return (jnp.where(first, q, zero), jnp.where(second, q, zero))

    qs = head_queries(i)

    def scores(queries, blk, buf):
        k = k_ref[pl.ds(pl.multiple_of(blk * tq, tq), tq), :]
        for h, qh in enumerate(queries):
            buf[h] = _dot_nt(k, qh)

    def consume(blk, buf, state, mask):
        start = pl.multiple_of(blk * tq, tq)
        ps, ms, alphas = [], [], []
        for h, (m, _) in enumerate(state):
            st = buf[h]
            if mask is not None:
                st = jnp.where(mask, st, NEG_INF)
            m_new = jnp.maximum(m, jnp.max(st, axis=0, keepdims=True))
            ms.append(m_new)
            alphas.append(jnp.exp2(m - m_new))
            ps.append(jnp.exp2(st - m_new).astype(BF16))
        new = []
        for h, (p, m_new, alpha, (_, acc)) in enumerate(zip(ps, ms, alphas, state)):
            vt = vt_ref[0, h * VT_ROWS:(h + 1) * VT_ROWS, pl.ds(start, tq)]
            new.append((m_new, alpha * acc + _dot(vt, p)))
        return tuple(new)

    causal = (lax.broadcasted_iota(jnp.int32, (tq, tq), 0) // CHUNK
              <= lax.broadcasted_iota(jnp.int32, (tq, tq), 1) // CHUNK)
    following = jnp.minimum(i + 1, nq - 1)

    def tile(sx, sy):
        @pl.when(i == 0)
        def _():
            scores(qs, 0, sx)

        def two_blocks(jj, state):
            scores(qs, 2 * jj + 1, sy)
            state = consume(2 * jj, sx, state, None)
            scores(qs, 2 * jj + 2, sx)
            return consume(2 * jj + 1, sy, state, None)

        def even_tail(state):
            scores(head_queries(following), 0, sy)
            return consume(i, sx, state, causal)

        def odd_tail(state):
            scores(qs, i, sy)
            state = consume(i - 1, sx, state, None)
            scores(head_queries(following), 0, sx)
            return consume(i, sy, state, causal)

        init = tuple((jnp.full((1, tq), NEG_INF, F32), jnp.zeros((VT_ROWS, tq), F32)) for _ in range(2))
        state = lax.fori_loop(0, i // 4, lambda jj, st: two_blocks(2 * jj + 1, two_blocks(2 * jj, st)), init)
        state = lax.fori_loop(2 * (i // 4), i // 2, two_blocks, state)
        (_, a0), (_, a1) = lax.cond(i % 2 == 1, odd_tail, even_tail, state)
        out_t = jnp.concatenate([a[0:HEAD_DIM] * (1.0 / a[HEAD_DIM:HEAD_DIM + 1]) for a in (a0, a1)], axis=0)
        o_ref[...] = out_t.T.astype(o_ref.dtype)

    swapped = ((i + 1) // 2) % 2

    @pl.when(swapped == 0)
    def _():
        tile(sa, sb)

    @pl.when(swapped == 1)
    def _():
        tile(sb, sa)


def _mla(qa, ka, vt, bsz, seq, tq):
    t = qa.shape[0]
    pairs = qa.shape[1] // (2 * LANES)
    nq = seq // tq
    return pl.pallas_call(
        functools.partial(_mla_kernel, tq=tq),
        grid=(bsz, pairs, nq),
        in_specs=[pl.BlockSpec((seq, 2 * LANES), lambda b, p, i: (b, p)),
                  pl.BlockSpec((seq, 2 * LANES), lambda b, p, i: (b, p)),
                  pl.BlockSpec((1, 2 * VT_ROWS, seq), lambda b, p, i: (b, p, 0))],
        out_specs=pl.BlockSpec((tq, LANES), lambda b, p, i: (b * nq + i, p)),
        out_shape=jax.ShapeDtypeStruct((t, pairs * LANES), BF16),
        scratch_shapes=[pltpu.VMEM((2, tq, tq), F32)] * 2,
        compiler_params=_cparams(("parallel", "parallel", "arbitrary")),
        name="mla",
    )(qa, ka, vt)


def _band_kernel(q_ref, k_ref, vt_ref, ring_ref, o_ref, bias_ref, *, tq, nblk):
    lane = lax.broadcasted_iota(jnp.int32, (1, LANES), 1)
    nk = nblk * tq

    @pl.when((pl.program_id(1) == 0) & (pl.program_id(2) == 0))
    def _():
        kc = lax.broadcasted_iota(jnp.int32, (nk, tq), 0) // CHUNK - ((nblk - 1) * tq // CHUNK - LEFT_CHUNKS)
        qc = lax.broadcasted_iota(jnp.int32, (nk, tq), 1) // CHUNK
        band = (kc >= qc) & (kc <= qc + LEFT_CHUNKS)
        for h in range(2):
            ring = jnp.broadcast_to(ring_ref[0, h:h + 1, :], (nk, ring_ref.shape[2]))
            rotated = pltpu.roll(ring, 0, 1, stride=1, stride_axis=0)
            bias_ref[h] = jnp.where(band, rotated[:, 0:tq], NEG_INF)

    def scores(n, edge):
        tile = first + n
        q = q_ref[n * tq:(n + 1) * tq, :]
        zero = jnp.zeros_like(q)
        qs = (jnp.where(lane < HEAD_DIM, q, zero), jnp.where(lane >= HEAD_DIM, q, zero))
        blks = [tile - (nblk - 1) + j for j in range(nblk)]
        starts = [pl.multiple_of((jnp.maximum(blk, 0) if edge else blk) * tq, tq) for blk in blks]
        ks = [k_ref[pl.ds(s, tq), :] for s in starts]
        sts = [[_dot_nt(ks[j], qh) + bias_ref[h, j * tq:(j + 1) * tq, :] for j in range(nblk)]
               for h, qh in enumerate(qs)]
        if edge:
            sts = [[jnp.where(blks[j] >= 0, st, NEG_INF) for j, st in enumerate(row)] for row in sts]
        return starts, sts

    def finish(starts, sts):
        outs = []
        for h in range(2):
            m = functools.reduce(jnp.maximum, [jnp.max(st, axis=0, keepdims=True) for st in sts[h]])
            acc = functools.reduce(jnp.add, [
                _dot(vt_ref[0, h * VT_ROWS:(h + 1) * VT_ROWS, pl.ds(starts[j], tq)], jnp.exp2(st - m).astype(BF16))
                for j, st in enumerate(sts[h])])
            outs.append(acc[0:HEAD_DIM] * (1.0 / acc[HEAD_DIM:HEAD_DIM + 1]))
        return jnp.concatenate(outs, axis=0).T

    tiles_per_step = q_ref.shape[0] // tq
    assert tiles_per_step >= nblk - 1
    first = pl.program_id(2) * tiles_per_step

    def tiles(edge):
        pending = scores(0, edge)
        for n in range(tiles_per_step):
            upcoming = scores(n + 1, edge) if n + 1 < tiles_per_step else None
            o_ref[n * tq:(n + 1) * tq, :] = finish(*pending).astype(o_ref.dtype)
            pending = upcoming

    for edge in (True, False):
        @pl.when((pl.program_id(2) == 0) == edge)
        def _():
            tiles(edge)


def _band_ring(rel_table, tq, nblk):
    ring = (nblk + 1) * tq
    x = np.arange(ring)
    query_minus_key = np.where(x < tq, x, x - ring) + (nblk - 1) * tq
    idx = np.clip(query_minus_key, -MAX_REL, MAX_REL) + MAX_REL
    heads = rel_table.shape[0]
    return (rel_table.astype(F32)[:, idx] * LOG2_E).reshape(heads // 2, 2, ring)


def _band(qb, kb, vtb, ring, bsz, seq, tq, nblk, tiles_per_step):
    t, hw = qb.shape
    pairs = hw // LANES
    rows = tq * tiles_per_step
    nq = seq // rows
    return pl.pallas_call(
        functools.partial(_band_kernel, tq=tq, nblk=nblk),
        grid=(pairs, bsz, nq),
        in_specs=[pl.BlockSpec((rows, LANES), lambda p, b, i: (b * nq + i, p)),
                  pl.BlockSpec((seq, LANES), lambda p, b, i: (b, p)),
                  pl.BlockSpec((1, 2 * VT_ROWS, seq), lambda p, b, i: (b, p, 0)),
                  pl.BlockSpec((1,) + ring.shape[1:], lambda p, b, i: (p, 0, 0))],
        out_specs=pl.BlockSpec((rows, LANES), lambda p, b, i: (b * nq + i, p)),
        out_shape=jax.ShapeDtypeStruct((t, hw), BF16),
        scratch_shapes=[pltpu.VMEM((2, nblk * tq, tq), F32)],
        compiler_params=_cparams(("arbitrary", "arbitrary", "arbitrary")),
        name="band",
    )(qb, kb, vtb, ring)


def _merge_kernel(oa_ref, ob_ref, ga_ref, gb_ref, x_ref, mod_ref, woa_ref, wob_ref, wout_ref, gffn_ref,
                  wr_ref, br_ref, x1_ref, h2x_ref, rt_ref, cnt_ref):
    @pl.when(pl.program_id(0) == 0)
    def _():
        cnt_ref[...] = jnp.zeros_like(cnt_ref)

    merged = (ga_ref[...].astype(F32) * _dot(oa_ref[...], woa_ref[...])
              + gb_ref[...].astype(F32) * _dot(ob_ref[...], wob_ref[...]))
    mix = _dot(merged.astype(BF16), wout_ref[...])
    x1 = x_ref[...] + mod_ref[0, 2:3, :] * mix
    x1_ref[...] = x1
    h2 = _rms(x1, gffn_ref[...]) * (1.0 + mod_ref[0, 4:5, :]) + mod_ref[0, 3:4, :]

    h_hi = h2.astype(BF16)
    h_lo = (h2 - h_hi.astype(F32)).astype(BF16)
    r_hi = _dot_nt(wr_ref[...], h_hi)
    logits = (r_hi[0:LANES] + _dot_nt(wr_ref[0:LANES, :], h_lo) + r_hi[LANES:2 * LANES]) + br_ref[...]
    blk = EXPERTS_PER_GROUP
    sub = lax.broadcasted_iota(jnp.int32, (blk, logits.shape[1]), 0)
    far = jnp.int32(blk)
    is_g = sub < N_GROUPS
    gl = jnp.where(is_g, logits[0:blk], NEG_INF)
    g_max = jnp.max(gl, axis=0, keepdims=True)
    g_w = 1.0 / jnp.sum(jnp.where(is_g, jnp.exp(gl - g_max), 0.0), axis=0, keepdims=True)
    g_idx = jnp.min(jnp.where(gl == g_max, sub, far), axis=0, keepdims=True)
    el = logits[blk:2 * blk]
    for g in range(1, N_GROUPS):
        el = jnp.where(g_idx == g, logits[(g + 1) * blk:(g + 2) * blk], el)
    e1 = jnp.max(el, axis=0, keepdims=True)
    i1 = jnp.min(jnp.where(el == e1, sub, far), axis=0, keepdims=True)
    el2 = jnp.where(sub == i1, NEG_INF, el)
    e2 = jnp.max(el2, axis=0, keepdims=True)
    i2 = jnp.min(jnp.where(el2 == e2, sub, far), axis=0, keepdims=True)
    ratio = jnp.exp(e2 - e1)
    w1 = g_w / (1.0 + ratio)
    w2 = g_w * ratio / (1.0 + ratio)
    e_lo, e_hi = jnp.minimum(i1, i2), jnp.maximum(i1, i2)
    cls = (g_idx * PAIRS_PER_GROUP + jnp.right_shift(e_lo * (2 * EXPERTS_PER_GROUP - 1 - e_lo), 1)
           + (e_hi - e_lo - 1))
    first_is_lo = i1 < i2
    w_lo, w_hi = jnp.where(first_is_lo, w1, w2), jnp.where(first_is_lo, w2, w1)
    route_t = (jnp.where(sub == ROUTE_CLS, cls.astype(F32), 0.0) + jnp.where(sub == ROUTE_W_LO, w_lo, 0.0)
               + jnp.where(sub == ROUTE_W_HI, w_hi, 0.0))
    rt_ref[...] = route_t
    member = _class_onehot(cls.astype(F32)).astype(BF16)
    cnt_ref[...] += _dot(member, jnp.ones((member.shape[1], LANES), BF16))[:, 0:1]
    d = h2.shape[1]
    h2x_ref[:, 0:d] = h2
    h2x_ref[:, d:d + LANES] = jnp.concatenate(
        [route_t, jnp.zeros((LANES - ROUTE_ROWS, route_t.shape[1]), F32)], axis=0).T


def _merge(oa, ob, ga, gb, x2, mod3, woa, wob, wout, g_ffn, wr, br, seq, tm):
    t, d = x2.shape
    hw = oa.shape[1]
    per_b = seq // tm
    row = lambda i: (i, 0)
    full = lambda i: (0, 0)
    return pl.pallas_call(
        _merge_kernel,
        grid=(t // tm,),
        in_specs=[pl.BlockSpec((tm, hw), row), pl.BlockSpec((tm, hw), row),
                  pl.BlockSpec((tm, d), row), pl.BlockSpec((tm, d), row),
                  pl.BlockSpec((tm, d), row),
                  pl.BlockSpec((1,) + mod3.shape[1:], lambda i: (i // per_b, 0, 0)),
                  pl.BlockSpec(woa.shape, full), pl.BlockSpec(wob.shape, full),
                  pl.BlockSpec(wout.shape, full), pl.BlockSpec(g_ffn.shape, full),
                  pl.BlockSpec(wr.shape, full), pl.BlockSpec(br.shape, full)],
        out_specs=[pl.BlockSpec((tm, d), row), pl.BlockSpec((tm, d + LANES), row),
                   pl.BlockSpec((ROUTE_ROWS, tm), lambda i: (0, i)),
                   pl.BlockSpec((LANES, 1), lambda i: (0, 0))],
        out_shape=[jax.ShapeDtypeStruct((t, d), F32), jax.ShapeDtypeStruct((t, d + LANES), F32),
                   jax.ShapeDtypeStruct((ROUTE_ROWS, t), F32), jax.ShapeDtypeStruct((LANES, 1), F32)],
        compiler_params=_cparams(("arbitrary",)),
        name="merge",
    )(oa, ob, ga, gb, x2, mod3, woa, wob, wout, g_ffn, wr, br)


def _class_onehot(cls_row):
    return lax.broadcasted_iota(jnp.int32, (LANES, cls_row.shape[1]), 0).astype(F32) == cls_row


def _plan_kernel(rt_ref, cnt_ref, pos_ref, carry_ref, off_ref, *, tc):
    i = pl.program_id(0)
    onehot = _class_onehot(rt_ref[ROUTE_CLS:ROUTE_CLS + 1, :])
    per_class = jnp.sum(onehot.astype(F32), axis=1, keepdims=True)

    @pl.when(i == 0)
    def _():
        padded = jnp.floor((cnt_ref[...] + (MOE_ROWS - 1)) * (1.0 / MOE_ROWS)) * MOE_ROWS
        hi = jnp.floor(padded * (1.0 / 256.0))
        digits = [jnp.broadcast_to(v, (LANES, LANES)).astype(BF16) for v in (hi, padded - 256.0 * hi)]
        below = (lax.broadcasted_iota(jnp.int32, (LANES, LANES), 1)
                 < lax.broadcasted_iota(jnp.int32, (LANES, LANES), 0)).astype(BF16)
        off = 256.0 * _dot(below, digits[0]) + _dot(below, digits[1])
        off_ref[...] = off[:, 0:1]
        carry_ref[...] = jnp.zeros_like(carry_ref)

    earlier = (lax.broadcasted_iota(jnp.int32, (tc, tc), 0)
               < lax.broadcasted_iota(jnp.int32, (tc, tc), 1)).astype(BF16)
    rank = _dot(onehot.astype(BF16), earlier)
    row = rank + (carry_ref[...] + off_ref[...])
    pos_ref[...] = jnp.sum(jnp.where(onehot, row, 0.0), axis=0, keepdims=True).astype(jnp.int32)
    carry_ref[...] += per_class


def _plan(route_t, counts, tc):
    t = route_t.shape[1]
    return pl.pallas_call(
        functools.partial(_plan_kernel, tc=tc),
        grid=(t // tc,),
        in_specs=[pl.BlockSpec((ROUTE_ROWS, tc), lambda i: (0, i)),
                  pl.BlockSpec((LANES, 1), lambda i: (0, 0))],
        out_specs=pl.BlockSpec((1, tc), lambda i: (0, i)),
        out_shape=jax.ShapeDtypeStruct((1, t), jnp.int32),
        scratch_shapes=[pltpu.VMEM((LANES, 1), F32), pltpu.VMEM((LANES, 1), F32)],
        compiler_params=_cparams(("arbitrary",)),
        name="plan",
    )(route_t, counts)


def _tile_tables(counts, n_tiles):
    pairs = [(lo, hi) for lo in range(EXPERTS_PER_GROUP) for hi in range(lo + 1, EXPERTS_PER_GROUP)]
    exp_lo = np.array([g * EXPERTS_PER_GROUP + lo for g in range(N_GROUPS) for lo, _ in pairs], np.int32)
    exp_hi = np.array([g * EXPERTS_PER_GROUP + hi for g in range(N_GROUPS) for _, hi in pairs], np.int32)
    cnt = counts.reshape(-1)[:N_CLASSES].astype(jnp.int32)
    tiles_per_class = (cnt + (MOE_ROWS - 1)) // MOE_ROWS
    tile_end = jnp.cumsum(tiles_per_class)
    n_used = tile_end[-1]
    j = jnp.arange(n_tiles, dtype=jnp.int32)
    tile_cls = jnp.sum(tile_end[None, :] <= jnp.minimum(j, n_used - 1)[:, None], axis=1, dtype=jnp.int32)
    valid_end = (tile_end - tiles_per_class) * MOE_ROWS + cnt
    of_class = tile_cls[:, None] == jnp.arange(N_CLASSES, dtype=jnp.int32)[None, :]

    def per_tile(table):
        return jnp.sum(jnp.where(of_class, table[None, :], 0), axis=1, dtype=jnp.int32)

    units = MOE_ROWS // ZERO_ROWS
    unit_end = (jnp.arange(n_tiles * units, dtype=jnp.int32) + 1) * ZERO_ROWS
    zero_fill = (unit_end.reshape(n_tiles, units) > per_tile(valid_end)[:, None]).astype(jnp.int32).reshape(-1)
    return (per_tile(jnp.asarray(exp_lo)), per_tile(jnp.asarray(exp_hi)), n_used.reshape(1).astype(jnp.int32),
            zero_fill)


def _row_copy(src, src_row, dst, dst_row, n, sem):
    return pltpu.make_async_copy(src.at[pl.ds(src_row, n)], dst.at[pl.ds(dst_row, n)], sem)


def _dispatch_kernel(pos_ref, zf_ref, h_ref, *refs, td, n_tiles, n_weights):
    w_in, hs_ref, w_out = refs[:n_weights], refs[n_weights], refs[n_weights + 1:2 * n_weights + 1]
    zero_ref, sem = refs[2 * n_weights + 1:]
    i = pl.program_id(0)

    @pl.when(i == 0)
    def _():
        zero_ref[...] = jnp.zeros_like(zero_ref)

        def fill(j, n):
            @pl.when(zf_ref[j] > 0)
            def _():
                _row_copy(zero_ref, 0, hs_ref, pl.multiple_of(j * ZERO_ROWS, ZERO_ROWS), ZERO_ROWS, sem).start()
            return n + zf_ref[j]

        n_fill = lax.fori_loop(0, n_tiles * (MOE_ROWS // ZERO_ROWS), fill, jnp.int32(0))

        def drain(_, carry):
            _row_copy(zero_ref, 0, hs_ref, 0, ZERO_ROWS, sem).wait()
            return carry

        lax.fori_loop(0, n_fill, drain, 0)

    base = i * td

    for r in range(td):
        _row_copy(h_ref, r, hs_ref, pos_ref[base + r], 1, sem).start()
    for src, dst in zip(w_in, w_out):
        dst[...] = src[...].astype(dst.dtype)
    _row_copy(h_ref, 0, hs_ref, 0, td, sem).wait()


def _dispatch(h2x, pos, zero_fill, n_tiles, td, weights):
    t, w = h2x.shape
    steps = t // td
    flat = [a.reshape(-1, a.shape[-1]) for a in weights]
    assert all(a.shape[0] % (steps * 16) == 0 for a in flat)
    w_specs = [pl.BlockSpec((a.shape[0] // steps, a.shape[1]), lambda i, pos, zf: (i, 0)) for a in flat]
    outs = pl.pallas_call(
        functools.partial(_dispatch_kernel, td=td, n_tiles=n_tiles, n_weights=len(flat)),
        grid_spec=pltpu.PrefetchScalarGridSpec(
            num_scalar_prefetch=2,
            grid=(steps,),
            in_specs=[pl.BlockSpec((td, w), lambda i, pos, zf: (i, 0))] + w_specs,
            out_specs=[pl.BlockSpec(memory_space=pl.ANY)] + w_specs,
            scratch_shapes=[pltpu.VMEM((ZERO_ROWS, w), F32), pltpu.SemaphoreType.DMA(())]),
        out_shape=[jax.ShapeDtypeStruct((n_tiles * MOE_ROWS, w), F32)]
        + [jax.ShapeDtypeStruct(a.shape, BF16) for a in flat],
        compiler_params=_cparams(("arbitrary",)),
        name="dispatch",
    )(pos, zero_fill, h2x, *flat)
    return outs[0], [o.reshape(a.shape) for o, a in zip(outs[1:], weights)]


def _moe_kernel(elo_ref, ehi_ref, nu_ref, hs_ref, wg_lo, wu_lo, wd_lo, wg_hi, wu_hi, wd_hi, y_ref):
    j = pl.program_id(0)
    d = y_ref.shape[1]

    @pl.when(j < nu_ref[0])
    def _():
        h = hs_ref[:, 0:d].astype(BF16)
        ups = [(_dot(h, wg[0]), _dot(h, wu[0])) for wg, wu in ((wg_lo, wu_lo), (wg_hi, wu_hi))]
        hids = [((a * jax.nn.sigmoid(a)) * u * hs_ref[:, d + lane:d + lane + 1]).astype(BF16)
                for (a, u), lane in zip(ups, (ROUTE_W_LO, ROUTE_W_HI))]
        y_ref[...] = _dot(hids[0], wd_lo[0]) + _dot(hids[1], wd_hi[0])

    @pl.when(j >= nu_ref[0])
    def _():
        y_ref[...] = jnp.zeros_like(y_ref)


def _moe(hs, exp_lo, exp_hi, n_used, wg, wu, wd):
    n_tiles = hs.shape[0] // MOE_ROWS
    _, d, ff = wg.shape
    up = lambda sel: pl.BlockSpec((1, d, ff), lambda j, elo, ehi, nu: ((elo, ehi)[sel][j], 0, 0))
    down = lambda sel: pl.BlockSpec((1, ff, d), lambda j, elo, ehi, nu: ((elo, ehi)[sel][j], 0, 0))
    return pl.pallas_call(
        _moe_kernel,
        grid_spec=pltpu.PrefetchScalarGridSpec(
            num_scalar_prefetch=3,
            grid=(n_tiles,),
            in_specs=[pl.BlockSpec((MOE_ROWS, hs.shape[1]),
                                   lambda j, elo, ehi, nu: (jnp.minimum(j, nu[0] - 1), 0)),
                      up(0), up(0), down(0), up(1), up(1), down(1)],
            out_specs=pl.BlockSpec((MOE_ROWS, d), lambda j, elo, ehi, nu: (j, 0))),
        out_shape=jax.ShapeDtypeStruct((n_tiles * MOE_ROWS, d), F32),
        compiler_params=_cparams(("arbitrary",)),
        name="moe",
    )(exp_lo, exp_hi, n_used, hs, wg, wu, wd, wg, wu, wd)


def _final_kernel(pos_ref, x1_ref, mod_ref, gfin_ref, y_ref, o_ref, rows_ref, sem, *, tf):
    i, n = pl.program_id(0), pl.num_programs(0)

    def fetch(step, slot):
        for r in range(tf):
            _row_copy(y_ref, pos_ref[step * tf + r], rows_ref.at[slot], r, 1, sem.at[slot]).start()

    @pl.when(i == 0)
    def _():
        fetch(0, 0)

    @pl.when(i + 1 < n)
    def _():
        fetch(i + 1, (i + 1) % 2)

    slot = i % 2
    _row_copy(y_ref, 0, rows_ref.at[slot], 0, tf, sem.at[slot]).wait()
    x2 = x1_ref[...] + mod_ref[0, 5:6, :] * rows_ref[slot]
    o_ref[...] = _rms(x2, gfin_ref[...])


def _final(y, pos, x1, mod3, g_final, seq, tf):
    t, d = x1.shape
    per_b = seq // tf
    return pl.pallas_call(
        functools.partial(_final_kernel, tf=tf),
        grid_spec=pltpu.PrefetchScalarGridSpec(
            num_scalar_prefetch=1,
            grid=(t // tf,),
            in_specs=[pl.BlockSpec((tf, d), lambda i, pos: (i, 0)),
                      pl.BlockSpec((1,) + mod3.shape[1:], lambda i, pos: (i // per_b, 0, 0)),
                      pl.BlockSpec((1, d), lambda i, pos: (0, 0)),
                      pl.BlockSpec(memory_space=pl.ANY)],
            out_specs=pl.BlockSpec((tf, d), lambda i, pos: (i, 0)),
            scratch_shapes=[pltpu.VMEM((2, tf, d), F32), pltpu.SemaphoreType.DMA((2,))]),
        out_shape=jax.ShapeDtypeStruct((t, d), F32),
        compiler_params=_cparams(("arbitrary",)),
        name="final",
    )(pos, x1, mod3, g_final, y)


def _layout_weights(w_in, w_uq, w_uk, w_uv, heads, q_lora, kv_lora, d):
    hw = heads * HEAD_DIM
    cuts = np.cumsum([q_lora, kv_lora, ROPE_DIM, hw, hw, hw, d, d])
    w_qlat, w_kvlat, w_kr, w_qb, w_kb, w_vb, w_ga, w_gb = jnp.split(w_in, [int(v) for v in cuts[:-1]], axis=1)
    swap = np.concatenate([np.arange(ROPE_DIM // 2, ROPE_DIM), np.arange(ROPE_DIM // 2)])
    w_kr_sw = w_kr[:, swap]
    kr_blk = jnp.concatenate([w_kr, w_kr, w_kr_sw, w_kr_sw], axis=1)
    band_scale = HEAD_DIM ** -0.5 * LOG2_E
    wbig = jnp.concatenate([w_qlat, w_kvlat, kr_blk, w_qb * band_scale, w_kb, w_ga, w_gb], axis=1)

    def transposed_values(w, k):
        return jnp.pad(w.reshape(k, heads, HEAD_DIM).transpose(1, 2, 0),
                       ((0, 0), (0, VT_ROWS - HEAD_DIM), (0, 0))).reshape(heads * VT_ROWS, k)

    qk = HEAD_DIM + ROPE_DIM
    wq3 = w_uq.reshape(q_lora, heads, qk) * (qk ** -0.5 * LOG2_E)
    nope = wq3[:, :, :HEAD_DIM].reshape(q_lora, hw)
    rope = wq3[:, :, HEAD_DIM:]
    pad = jnp.zeros((q_lora, heads // 2, LANES - 2 * ROPE_DIM), w_uq.dtype)

    def lay(r):
        return jnp.concatenate([r.reshape(q_lora, heads // 2, 2 * ROPE_DIM), pad], axis=2).reshape(q_lora, hw)

    wq = jnp.concatenate([nope, lay(rope), lay(rope[:, :, swap])], axis=1)
    return (wbig.astype(BF16), wq.astype(BF16), w_uk.astype(BF16), transposed_values(w_uv, kv_lora).astype(BF16),
            transposed_values(w_vb, d).astype(BF16))


def kernel(x, c, positions, w_ada, b_ada, g_mix, w_in, g_q, w_uq, g_kv, w_uk, w_uv, rel_bias, w_oa, w_ob,
           w_out, g_ffn, w_rg, b_rg, w_re, b_re, w_gate, w_up, w_down, g_final):
    bsz, seq, d = x.shape
    depth = w_ada.shape[0]
    t = bsz * seq
    heads = rel_bias.shape[1]
    q_lora, kv_lora = g_q.shape[1], g_kv.shape[1]
    tq, tq_mla = TQ_BAND, TQ_MLA
    nblk = -(-LEFT_CHUNKS * CHUNK // tq) + 1
    assert seq % tq == 0 and seq % tq_mla == 0 and tq % CHUNK == 0 and heads % 2 == 0
    assert ROUTE_ROWS == EXPERTS_PER_GROUP and N_GROUPS <= EXPERTS_PER_GROUP

    x2 = x.reshape(t, d)
    pos2 = positions.reshape(bsz, 1, seq)
    out = x2
    for l in range(depth):
        mod3 = _ada(c, w_ada[l], b_ada[l]).reshape(bsz, -1, d)
        wbig, wq, wkv, wvt, wvbt = _layout_weights(w_in[l], w_uq[l], w_uk[l], w_uv[l], heads, q_lora, kv_lora, d)
        qa, ka, vt, qb, kb, vtb, ga, gb = _proj(
            x2, pos2, mod3, g_mix[l].reshape(1, d), wbig, g_q[l].reshape(1, -1), wq, g_kv[l].reshape(1, -1), wkv,
            wvt, wvbt, seq, tm=TM_PROJ)
        oa = _mla(qa, ka, vt, bsz, seq, tq_mla)
        ob = _band(qb, kb, vtb, _band_ring(rel_bias[l], tq, nblk), bsz, seq, tq, nblk,
                   tiles_per_step=BAND_TILES_PER_STEP)

        pad_g = EXPERTS_PER_GROUP - N_GROUPS
        n_route = (N_GROUPS + 1) * EXPERTS_PER_GROUP
        w_r = jnp.concatenate([w_rg[l], jnp.zeros((d, pad_g), F32), w_re[l],
                               jnp.zeros((d, LANES - n_route), F32)], axis=1).T
        b_r = jnp.concatenate([b_rg[l], jnp.zeros((pad_g,), F32), b_re[l],
                               jnp.zeros((LANES - n_route,), F32)]).reshape(LANES, 1)
        w_r_hi = w_r.astype(BF16)
        w_r_lo = (w_r - w_r_hi.astype(F32)).astype(BF16)
        x1, h2x, route_t, counts = _merge(oa, ob, ga, gb, x2, mod3, w_oa[l].astype(BF16), w_ob[l].astype(BF16),
                                          w_out[l].astype(BF16), g_ffn[l].reshape(1, d),
                                          jnp.concatenate([w_r_hi, w_r_lo], axis=0), b_r, seq, tm=TM_MERGE)
        n_tiles = t // MOE_ROWS + N_CLASSES
        pos = _plan(route_t, counts, tc=TC_PLAN).reshape(t)
        exp_lo, exp_hi, n_used, zero_fill = _tile_tables(counts, n_tiles)
        hs, expert_weights = _dispatch(h2x, pos, zero_fill, n_tiles, TD_DISPATCH, (w_gate[l], w_up[l], w_down[l]))
        y = _moe(hs, exp_lo, exp_hi, n_used, *expert_weights)
        assert l == depth - 1, "multi-layer stacks need an un-normalised residual output"
        out = _final(y, pos, x1, mod3, g_final.reshape(1, d), seq, tf=TF_FINAL)
    return out.reshape(bsz, seq, d)
```

```python
import functools

import numpy as np
import jax
import jax.numpy as jnp
from jax import lax
from jax.experimental import pallas as pl
from jax.experimental.pallas import tpu as pltpu

F32 = jnp.float32
BF16 = jnp.bfloat16

NORM_EPS = 1e-6
NEG_INF = -1e30
ROPE_BASE = 10000.0
CHUNK = 64
LEFT_CHUNKS = 8
MAX_REL = 256
N_GROUPS = 4
EXPERTS_PER_GROUP = 8
PAIRS_PER_GROUP = EXPERTS_PER_GROUP * (EXPERTS_PER_GROUP - 1) // 2
N_CLASSES = N_GROUPS * PAIRS_PER_GROUP
MOE_ROWS = 256
ZERO_ROWS = 64
ROUTE_CLS, ROUTE_W_LO, ROUTE_W_HI = 0, 1, 2
ROUTE_ROWS = 8
LANES = 128
HEAD_DIM = 64
ROPE_DIM = 32
VT_ROWS = HEAD_DIM + 16
LOG2_E = 1.4426950408889634

VMEM_LIMIT = 56 * 1024 * 1024

TM_PROJ = 512
TM_MERGE = 512
TQ_MLA = 512
TQ_BAND = 256
BAND_TILES_PER_STEP = 8
TC_PLAN = 1024
TD_DISPATCH = 2048
TF_FINAL = 512


def _cparams(sem):
    return pltpu.CompilerParams(dimension_semantics=sem, vmem_limit_bytes=VMEM_LIMIT)


def _dot(a, b):
    return jnp.dot(a, b, preferred_element_type=F32)


def _dot_nt(a, b):
    return lax.dot_general(a, b, (((1,), (1,)), ((), ())), preferred_element_type=F32)


def _rms(x, g):
    return x * lax.rsqrt(jnp.mean(x * x, axis=-1, keepdims=True) + NORM_EPS) * g


def _ada_kernel(c_ref, w_ref, b_ref, o_ref):
    c = c_ref[...]
    o_ref[...] = _dot(c * jax.nn.sigmoid(c), w_ref[...]) + b_ref[...]


def _ada(c, w_ada, b_ada):
    bsz, d = c.shape
    n = w_ada.shape[1]
    return pl.pallas_call(
        _ada_kernel,
        grid=(n // d,),
        in_specs=[pl.BlockSpec((bsz, d), lambda j: (0, 0)),
                  pl.BlockSpec((d, d), lambda j: (0, j)),
                  pl.BlockSpec((1, d), lambda j: (0, j))],
        out_specs=pl.BlockSpec((bsz, d), lambda j: (0, j)),
        out_shape=jax.ShapeDtypeStruct((bsz, n), F32),
        compiler_params=_cparams(("arbitrary",)),
        name="ada",
    )(c, w_ada, b_ada.reshape(1, n))


def _proj_kernel(x_ref, pos_ref, mod_ref, gmix_ref, wbig_ref, gq_ref, wq_ref, gkv_ref, wkv_ref,
                 wvt_ref, wvbt_ref, one_ref, inv_ref, sgn_ref,
                 qa_ref, ka_ref, vt_ref, qb_ref, kb_ref, vtb_ref, ga_ref, gb_ref, *, q_lora, kv_lora):
    x = x_ref[...]
    h = _rms(x, gmix_ref[...]) * (1.0 + mod_ref[0, 1:2, :]) + mod_ref[0, 0:1, :]
    hb = h.astype(BF16)

    c0 = q_lora + kv_lora + LANES
    head = _dot(hb, wbig_ref[:, 0:c0])
    q_lat = head[:, 0:q_lora]
    kv_lat = head[:, q_lora:q_lora + kv_lora]
    kr_blk = head[:, q_lora + kv_lora:c0]
    hw = qb_ref.shape[1]
    d = x.shape[1]
    g0 = c0 + 2 * hw
    ga_ref[...] = jax.nn.sigmoid(_dot(hb, wbig_ref[:, g0:g0 + d])).astype(BF16)

    ang = inv_ref[...] * pos_ref[0].astype(F32)
    live = lax.broadcasted_iota(jnp.int32, (LANES, 1), 0) < 2 * ROPE_DIM
    cos_t = jnp.where(live, jnp.cos(ang), 0.0).T
    sin_t = jnp.where(live, jnp.sin(ang) * sgn_ref[...], 0.0).T

    qn = _rms(q_lat, gq_ref[...]).astype(BF16)
    q_all = _dot(qn, wq_ref[...])
    kvn = _rms(kv_lat, gkv_ref[...]).astype(BF16)
    k_nope = _dot(kvn, wkv_ref[...])
    vt_ref[0] = (_dot_nt(wvt_ref[...], kvn) + one_ref[...]).astype(BF16)
    gb_ref[...] = jax.nn.sigmoid(_dot(hb, wbig_ref[:, g0 + d:g0 + 2 * d])).astype(BF16)
    k_rot = (kr_blk * cos_t + pltpu.roll(kr_blk, LANES // 2, 1) * sin_t).astype(BF16)
    for p in range(hw // LANES):
        lo, hi = p * LANES, (p + 1) * LANES
        qa_ref[:, 2 * lo:2 * lo + LANES] = q_all[:, lo:hi].astype(BF16)
        qa_ref[:, 2 * lo + LANES:2 * hi] = (
            q_all[:, hw + lo:hw + hi] * cos_t + q_all[:, 2 * hw + lo:2 * hw + hi] * sin_t).astype(BF16)
        ka_ref[:, 2 * lo:2 * lo + LANES] = k_nope[:, lo:hi].astype(BF16)
        ka_ref[:, 2 * lo + LANES:2 * hi] = k_rot

    qb_ref[...] = _dot(hb, wbig_ref[:, c0:c0 + hw]).astype(BF16)
    kb_ref[...] = _dot(hb, wbig_ref[:, c0 + hw:c0 + 2 * hw]).astype(BF16)
    vtb_ref[0] = (_dot_nt(wvbt_ref[...], hb) + one_ref[...]).astype(BF16)


def _proj(x2, pos2, mod3, g_mix, wbig, g_q, wq, g_kv, wkv, wvt, wvbt, seq, tm):
    t, d = x2.shape
    q_lora, kv_lora = g_q.shape[1], g_kv.shape[1]
    hw = wkv.shape[1]
    per_b = seq // tm
    row = lambda i: (i, 0)
    full = lambda i: (0, 0)
    inv = ROPE_BASE ** (-(np.arange(LANES) % (ROPE_DIM // 2)).astype(np.float32) / (ROPE_DIM // 2))
    sgn = np.where((np.arange(LANES) % ROPE_DIM) < ROPE_DIM // 2, -1.0, 1.0).astype(np.float32)
    ones_col = (np.arange(wvt.shape[0]) % VT_ROWS == HEAD_DIM).astype(np.float32).reshape(-1, 1)
    v_t = jax.ShapeDtypeStruct((t // seq, wvt.shape[0], seq), BF16)
    outs = [jax.ShapeDtypeStruct((t, 2 * hw), BF16), jax.ShapeDtypeStruct((t, 2 * hw), BF16), v_t,
            jax.ShapeDtypeStruct((t, hw), BF16), jax.ShapeDtypeStruct((t, hw), BF16), v_t,
            jax.ShapeDtypeStruct((t, d), BF16), jax.ShapeDtypeStruct((t, d), BF16)]
    v_t_spec = pl.BlockSpec((1, wvt.shape[0], tm), lambda i: (i // per_b, 0, i % per_b))
    out_specs = [v_t_spec if o is v_t else pl.BlockSpec((tm, o.shape[1]), row) for o in outs]
    return pl.pallas_call(
        functools.partial(_proj_kernel, q_lora=q_lora, kv_lora=kv_lora),
        grid=(t // tm,),
        in_specs=[pl.BlockSpec((tm, d), row),
                  pl.BlockSpec((1, 1, tm), lambda i: (i // per_b, 0, i % per_b)),
                  pl.BlockSpec((1,) + mod3.shape[1:], lambda i: (i // per_b, 0, 0)),
                  pl.BlockSpec(g_mix.shape, full),
                  pl.BlockSpec(wbig.shape, full),
                  pl.BlockSpec(g_q.shape, full),
                  pl.BlockSpec(wq.shape, full),
                  pl.BlockSpec(g_kv.shape, full),
                  pl.BlockSpec(wkv.shape, full),
                  pl.BlockSpec(wvt.shape, full),
                  pl.BlockSpec(wvbt.shape, full),
                  pl.BlockSpec(ones_col.shape, full),
                  pl.BlockSpec((LANES, 1), full),
                  pl.BlockSpec((LANES, 1), full)],
        out_specs=out_specs,
        out_shape=outs,
        compiler_params=_cparams(("parallel",)),
        name="proj",
    )(x2, pos2, mod3, g_mix, wbig, g_q, wq, g_kv, wkv, wvt, wvbt, jnp.asarray(ones_col),
      jnp.asarray(inv).reshape(LANES, 1), jnp.asarray(sgn).reshape(LANES, 1))


def _mla_kernel(q_ref, k_ref, vt_ref, o_ref, sa, sb, *, tq):
    i, nq = pl.program_id(2), pl.num_programs(2)
    lane = lax.broadcasted_iota(jnp.int32, (1, 2 * LANES), 1)
    first = (lane < HEAD_DIM) | ((lane >= LANES) & (lane < LANES + ROPE_DIM))
    second = ((lane >= HEAD_DIM) & (lane < LANES)) | ((lane >= LANES + ROPE_DIM) & (lane < LANES + 2 * ROPE_DIM))

    def head_queries(tile):
        q = q_ref[pl.ds(pl.multiple_of(tile * tq, tq), tq), :]
        zero = jnp.zeros_like(q)
        return (jnp.where(first, q, zero), jnp.where(second, q, zero))

    qs = head_queries(i)

    def scores(queries, blk, buf):
        k = k_ref[pl.ds(pl.multiple_of(blk * tq, tq), tq), :]
        for h, qh in enumerate(queries):
            buf[h] = _dot_nt(k, qh)

    def consume(blk, buf, state, mask):
        start = pl.multiple_of(blk * tq, tq)
        ps, ms, alphas = [], [], []
        for h, (m, _) in enumerate(state):
            st = buf[h]
            if mask is not None:
                st = jnp.where(mask, st, NEG_INF)
            m_new = jnp.maximum(m, jnp.max(st, axis=0, keepdims=True))
            ms.append(m_new)
            alphas.append(jnp.exp2(m - m_new))
            ps.append(jnp.exp2(st - m_new).astype(BF16))
        new = []
        for h, (p, m_new, alpha, (_, acc)) in enumerate(zip(ps, ms, alphas, state)):
            vt = vt_ref[0, h * VT_ROWS:(h + 1) * VT_ROWS, pl.ds(start, tq)]
            new.append((m_new, alpha * acc + _dot(vt, p)))
        return tuple(new)

    causal = (lax.broadcasted_iota(jnp.int32, (tq, tq), 0) // CHUNK
              <= lax.broadcasted_iota(jnp.int32, (tq, tq), 1) // CHUNK)
    following = jnp.minimum(i + 1, nq - 1)

    def tile(sx, sy):
        @pl.when(i == 0)
        def _():
            scores(qs, 0, sx)

        def two_blocks(jj, state):
            scores(qs, 2 * jj + 1, sy)
            state = consume(2 * jj, sx, state, None)
            scores(qs, 2 * jj + 2, sx)
            return consume(2 * jj + 1, sy, state, None)

        def even_tail(state):
            scores(head_queries(following), 0, sy)
            return consume(i, sx, state, causal)

        def odd_tail(state):
            scores(qs, i, sy)
            state = consume(i - 1, sx, state, None)
            scores(head_queries(following), 0, sx)
            return consume(i, sy, state, causal)

        init = tuple((jnp.full((1, tq), NEG_INF, F32), jnp.zeros((VT_ROWS, tq), F32)) for _ in range(2))
        state = lax.fori_loop(0, i // 4, lambda jj, st: two_blocks(2 * jj + 1, two_blocks(2 * jj, st)), init)
        state = lax.fori_loop(2 * (i // 4), i // 2, two_blocks, state)
        (_, a0), (_, a1) = lax.cond(i % 2 == 1, odd_tail, even_tail, state)
        out_t = jnp.concatenate([a[0:HEAD_DIM] * (1.0 / a[HEAD_DIM:HEAD_DIM + 1]) for a in (a0, a1)], axis=0)
        o_ref[...] = out_t.T.astype(o_ref.dtype)

    swapped = ((i + 1) // 2) % 2

    @pl.when(swapped == 0)
    def _():
        tile(sa, sb)

    @pl.when(swapped == 1)
    def _():
        tile(sb, sa)


def _mla(qa, ka, vt, bsz, seq, tq):
    t = qa.shape[0]
    pairs = qa.shape[1] // (2 * LANES)
    nq = seq // tq
    return pl.pallas_call(
        functools.partial(_mla_kernel, tq=tq),
        grid=(bsz, pairs, nq),
        in_specs=[pl.BlockSpec((seq, 2 * LANES), lambda b, p, i: (b, p)),
                  pl.BlockSpec((seq, 2 * LANES), lambda b, p, i: (b, p)),
                  pl.BlockSpec((1, 2 * VT_ROWS, seq), lambda b, p, i: (b, p, 0))],
        out_specs=pl.BlockSpec((tq, LANES), lambda b, p, i: (b * nq + i, p)),
        out_shape=jax.ShapeDtypeStruct((t, pairs * LANES), BF16),
        scratch_shapes=[pltpu.VMEM((2, tq, tq), F32)] * 2,
        compiler_params=_cparams(("parallel", "parallel", "arbitrary")),
        name="mla",
    )(qa, ka, vt)


def _band_kernel(q_ref, k_ref, vt_ref, ring_ref, o_ref, bias_ref, *, tq, nblk):
    lane = lax.broadcasted_iota(jnp.int32, (1, LANES), 1)
    nk = nblk * tq

    @pl.when((pl.program_id(1) == 0) & (pl.program_id(2) == 0))
    def _():
        kc = lax.broadcasted_iota(jnp.int32, (nk, tq), 0) // CHUNK - ((nblk - 1) * tq // CHUNK - LEFT_CHUNKS)
        qc = lax.broadcasted_iota(jnp.int32, (nk, tq), 1) // CHUNK
        band = (kc >= qc) & (kc <= qc + LEFT_CHUNKS)
        for h in range(2):
            ring = jnp.broadcast_to(ring_ref[0, h:h + 1, :], (nk, ring_ref.shape[2]))
            rotated = pltpu.roll(ring, 0, 1, stride=1, stride_axis=0)
            bias_ref[h] = jnp.where(band, rotated[:, 0:tq], NEG_INF)

    def scores(n, edge):
        tile = first + n
        q = q_ref[n * tq:(n + 1) * tq, :]
        zero = jnp.zeros_like(q)
        qs = (jnp.where(lane < HEAD_DIM, q, zero), jnp.where(lane >= HEAD_DIM, q, zero))
        blks = [tile - (nblk - 1) + j for j in range(nblk)]
        starts = [pl.multiple_of((jnp.maximum(blk, 0) if edge else blk) * tq, tq) for blk in blks]
        ks = [k_ref[pl.ds(s, tq), :] for s in starts]
        sts = [[_dot_nt(ks[j], qh) + bias_ref[h, j * tq:(j + 1) * tq, :] for j in range(nblk)]
               for h, qh in enumerate(qs)]
        if edge:
            sts = [[jnp.where(blks[j] >= 0, st, NEG_INF) for j, st in enumerate(row)] for row in sts]
        return starts, sts

    def finish(starts, sts):
        outs = []
        for h in range(2):
            m = functools.reduce(jnp.maximum, [jnp.max(st, axis=0, keepdims=True) for st in sts[h]])
            acc = functools.reduce(jnp.add, [
                _dot(vt_ref[0, h * VT_ROWS:(h + 1) * VT_ROWS, pl.ds(starts[j], tq)], jnp.exp2(st - m).astype(BF16))
                for j, st in enumerate(sts[h])])
            outs.append(acc[0:HEAD_DIM] * (1.0 / acc[HEAD_DIM:HEAD_DIM + 1]))
        return jnp.concatenate(outs, axis=0).T

    tiles_per_step = q_ref.shape[0] // tq
    assert tiles_per_step >= nblk - 1
    first = pl.program_id(2) * tiles_per_step

    def tiles(edge):
        pending = scores(0, edge)
        for n in range(tiles_per_step):
            upcoming = scores(n + 1, edge) if n + 1 < tiles_per_step else None
            o_ref[n * tq:(n + 1) * tq, :] = finish(*pending).astype(o_ref.dtype)
            pending = upcoming

    for edge in (True, False):
        @pl.when((pl.program_id(2) == 0) == edge)
        def _():
            tiles(edge)


def _band_ring(rel_table, tq, nblk):
    ring = (nblk + 1) * tq
    x = np.arange(ring)
    query_minus_key = np.where(x < tq, x, x - ring) + (nblk - 1) * tq
    idx = np.clip(query_minus_key, -MAX_REL, MAX_REL) + MAX_REL
    heads = rel_table.shape[0]
    return (rel_table.astype(F32)[:, idx] * LOG2_E).reshape(heads // 2, 2, ring)


def _band(qb, kb, vtb, ring, bsz, seq, tq, nblk, tiles_per_step):
    t, hw = qb.shape
    pairs = hw // LANES
    rows = tq * tiles_per_step
    nq = seq // rows
    return pl.pallas_call(
        functools.partial(_band_kernel, tq=tq, nblk=nblk),
        grid=(pairs, bsz, nq),
        in_specs=[pl.BlockSpec((rows, LANES), lambda p, b, i: (b * nq + i, p)),
                  pl.BlockSpec((seq, LANES), lambda p, b, i: (b, p)),
                  pl.BlockSpec((1, 2 * VT_ROWS, seq), lambda p, b, i: (b, p, 0)),
                  pl.BlockSpec((1,) + ring.shape[1:], lambda p, b, i: (p, 0, 0))],
        out_specs=pl.BlockSpec((rows, LANES), lambda p, b, i: (b * nq + i, p)),
        out_shape=jax.ShapeDtypeStruct((t, hw), BF16),
        scratch_shapes=[pltpu.VMEM((2, nblk * tq, tq), F32)],
        compiler_params=_cparams(("arbitrary", "arbitrary", "arbitrary")),
        name="band",
    )(qb, kb, vtb, ring)


def _merge_kernel(oa_ref, ob_ref, ga_ref, gb_ref, x_ref, mod_ref, woa_ref, wob_ref, wout_ref, gffn_ref,
                  wr_ref, br_ref, x1_ref, h2x_ref, rt_ref, cnt_ref):
    @pl.when(pl.program_id(0) == 0)
    def _():
        cnt_ref[...] = jnp.zeros_like(cnt_ref)

    merged = (ga_ref[...].astype(F32) * _dot(oa_ref[...], woa_ref[...])
              + gb_ref[...].astype(F32) * _dot(ob_ref[...], wob_ref[...]))
    mix = _dot(merged.astype(BF16), wout_ref[...])
    x1 = x_ref[...] + mod_ref[0, 2:3, :] * mix
    x1_ref[...] = x1
    h2 = _rms(x1, gffn_ref[...]) * (1.0 + mod_ref[0, 4:5, :]) + mod_ref[0, 3:4, :]

    h_hi = h2.astype(BF16)
    h_lo = (h2 - h_hi.astype(F32)).astype(BF16)
    r_hi = _dot_nt(wr_ref[...], h_hi)
    logits = (r_hi[0:LANES] + _dot_nt(wr_ref[0:LANES, :], h_lo) + r_hi[LANES:2 * LANES]) + br_ref[...]
    blk = EXPERTS_PER_GROUP
    sub = lax.broadcasted_iota(jnp.int32, (blk, logits.shape[1]), 0)
    far = jnp.int32(blk)
    is_g = sub < N_GROUPS
    gl = jnp.where(is_g, logits[0:blk], NEG_INF)
    g_max = jnp.max(gl, axis=0, keepdims=True)
    g_w = 1.0 / jnp.sum(jnp.where(is_g, jnp.exp(gl - g_max), 0.0), axis=0, keepdims=True)
    g_idx = jnp.min(jnp.where(gl == g_max, sub, far), axis=0, keepdims=True)
    el = logits[blk:2 * blk]
    for g in range(1, N_GROUPS):
        el = jnp.where(g_idx == g, logits[(g + 1) * blk:(g + 2) * blk], el)
    e1 = jnp.max(el, axis=0, keepdims=True)
    i1 = jnp.min(jnp.where(el == e1, sub, far), axis=0, keepdims=True)
    el2 = jnp.where(sub == i1, NEG_INF, el)
    e2 = jnp.max(el2, axis=0, keepdims=True)
    i2 = jnp.min(jnp.where(el2 == e2, sub, far), axis=0, keepdims=True)
    ratio = jnp.exp(e2 - e1)
    w1 = g_w / (1.0 + ratio)
    w2 = g_w * ratio / (1.0 + ratio)
    e_lo, e_hi = jnp.minimum(i1, i2), jnp.maximum(i1, i2)
    cls = (g_idx * PAIRS_PER_GROUP + jnp.right_shift(e_lo * (2 * EXPERTS_PER_GROUP - 1 - e_lo), 1)
           + (e_hi - e_lo - 1))
    first_is_lo = i1 < i2
    w_lo, w_hi = jnp.where(first_is_lo, w1, w2), jnp.where(first_is_lo, w2, w1)
    route_t = (jnp.where(sub == ROUTE_CLS, cls.astype(F32), 0.0) + jnp.where(sub == ROUTE_W_LO, w_lo, 0.0)
               + jnp.where(sub == ROUTE_W_HI, w_hi, 0.0))
    rt_ref[...] = route_t
    member = _class_onehot(cls.astype(F32)).astype(BF16)
    cnt_ref[...] += _dot(member, jnp.ones((member.shape[1], LANES), BF16))[:, 0:1]
    d = h2.shape[1]
    h2x_ref[:, 0:d] = h2
    h2x_ref[:, d:d + LANES] = jnp.concatenate(
        [route_t, jnp.zeros((LANES - ROUTE_ROWS, route_t.shape[1]), F32)], axis=0).T


def _merge(oa, ob, ga, gb, x2, mod3, woa, wob, wout, g_ffn, wr, br, seq, tm):
    t, d = x2.shape
    hw = oa.shape[1]
    per_b = seq // tm
    row = lambda i: (i, 0)
    full = lambda i: (0, 0)
    return pl.pallas_call(
        _merge_kernel,
        grid=(t // tm,),
        in_specs=[pl.BlockSpec((tm, hw), row), pl.BlockSpec((tm, hw), row),
                  pl.BlockSpec((tm, d), row), pl.BlockSpec((tm, d), row),
                  pl.BlockSpec((tm, d), row),
                  pl.BlockSpec((1,) + mod3.shape[1:], lambda i: (i // per_b, 0, 0)),
                  pl.BlockSpec(woa.shape, full), pl.BlockSpec(wob.shape, full),
                  pl.BlockSpec(wout.shape, full), pl.BlockSpec(g_ffn.shape, full),
                  pl.BlockSpec(wr.shape, full), pl.BlockSpec(br.shape, full)],
        out_specs=[pl.BlockSpec((tm, d), row), pl.BlockSpec((tm, d + LANES), row),
                   pl.BlockSpec((ROUTE_ROWS, tm), lambda i: (0, i)),
                   pl.BlockSpec((LANES, 1), lambda i: (0, 0))],
        out_shape=[jax.ShapeDtypeStruct((t, d), F32), jax.ShapeDtypeStruct((t, d + LANES), F32),
                   jax.ShapeDtypeStruct((ROUTE_ROWS, t), F32), jax.ShapeDtypeStruct((LANES, 1), F32)],
        compiler_params=_cparams(("arbitrary",)),
        name="merge",
    )(oa, ob, ga, gb, x2, mod3, woa, wob, wout, g_ffn, wr, br)


def _class_onehot(cls_row):
    return lax.broadcasted_iota(jnp.int32, (LANES, cls_row.shape[1]), 0).astype(F32) == cls_row


def _plan_kernel(rt_ref, cnt_ref, pos_ref, carry_ref, off_ref, *, tc):
    i = pl.program_id(0)
    onehot = _class_onehot(rt_ref[ROUTE_CLS:ROUTE_CLS + 1, :])
    per_class = jnp.sum(onehot.astype(F32), axis=1, keepdims=True)

    @pl.when(i == 0)
    def _():
        padded = jnp.floor((cnt_ref[...] + (MOE_ROWS - 1)) * (1.0 / MOE_ROWS)) * MOE_ROWS
        hi = jnp.floor(padded * (1.0 / 256.0))
        digits = [jnp.broadcast_to(v, (LANES, LANES)).astype(BF16) for v in (hi, padded - 256.0 * hi)]
        below = (lax.broadcasted_iota(jnp.int32, (LANES, LANES), 1)
                 < lax.broadcasted_iota(jnp.int32, (LANES, LANES), 0)).astype(BF16)
        off = 256.0 * _dot(below, digits[0]) + _dot(below, digits[1])
        off_ref[...] = off[:, 0:1]
        carry_ref[...] = jnp.zeros_like(carry_ref)

    earlier = (lax.broadcasted_iota(jnp.int32, (tc, tc), 0)
               < lax.broadcasted_iota(jnp.int32, (tc, tc), 1)).astype(BF16)
    rank = _dot(onehot.astype(BF16), earlier)
    row = rank + (carry_ref[...] + off_ref[...])
    pos_ref[...] = jnp.sum(jnp.where(onehot, row, 0.0), axis=0, keepdims=True).astype(jnp.int32)
    carry_ref[...] += per_class


def _plan(route_t, counts, tc):
    t = route_t.shape[1]
    return pl.pallas_call(
        functools.partial(_plan_kernel, tc=tc),
        grid=(t // tc,),
        in_specs=[pl.BlockSpec((ROUTE_ROWS, tc), lambda i: (0, i)),
                  pl.BlockSpec((LANES, 1), lambda i: (0, 0))],
        out_specs=pl.BlockSpec((1, tc), lambda i: (0, i)),
        out_shape=jax.ShapeDtypeStruct((1, t), jnp.int32),
        scratch_shapes=[pltpu.VMEM((LANES, 1), F32), pltpu.VMEM((LANES, 1), F32)],
        compiler_params=_cparams(("arbitrary",)),
        name="plan",
    )(route_t, counts)


def _tile_tables(counts, n_tiles):
    pairs = [(lo, hi) for lo in range(EXPERTS_PER_GROUP) for hi in range(lo + 1, EXPERTS_PER_GROUP)]
    exp_lo = np.array([g * EXPERTS_PER_GROUP + lo for g in range(N_GROUPS) for lo, _ in pairs], np.int32)
    exp_hi = np.array([g * EXPERTS_PER_GROUP + hi for g in range(N_GROUPS) for _, hi in pairs], np.int32)
    cnt = counts.reshape(-1)[:N_CLASSES].astype(jnp.int32)
    tiles_per_class = (cnt + (MOE_ROWS - 1)) // MOE_ROWS
    tile_end = jnp.cumsum(tiles_per_class)
    n_used = tile_end[-1]
    j = jnp.arange(n_tiles, dtype=jnp.int32)
    tile_cls = jnp.sum(tile_end[None, :] <= jnp.minimum(j, n_used - 1)[:, None], axis=1, dtype=jnp.int32)
    valid_end = (tile_end - tiles_per_class) * MOE_ROWS + cnt
    of_class = tile_cls[:, None] == jnp.arange(N_CLASSES, dtype=jnp.int32)[None, :]

    def per_tile(table):
        return jnp.sum(jnp.where(of_class, table[None, :], 0), axis=1, dtype=jnp.int32)

    units = MOE_ROWS // ZERO_ROWS
    unit_end = (jnp.arange(n_tiles * units, dtype=jnp.int32) + 1) * ZERO_ROWS
    zero_fill = (unit_end.reshape(n_tiles, units) > per_tile(valid_end)[:, None]).astype(jnp.int32).reshape(-1)
    return (per_tile(jnp.asarray(exp_lo)), per_tile(jnp.asarray(exp_hi)), n_used.reshape(1).astype(jnp.int32),
            zero_fill)


def _row_copy(src, src_row, dst, dst_row, n, sem):
    return pltpu.make_async_copy(src.at[pl.ds(src_row, n)], dst.at[pl.ds(dst_row, n)], sem)


def _dispatch_kernel(pos_ref, zf_ref, h_ref, *refs, td, n_tiles, n_weights):
    w_in, hs_ref, w_out = refs[:n_weights], refs[n_weights], refs[n_weights + 1:2 * n_weights + 1]
    zero_ref, sem = refs[2 * n_weights + 1:]
    i = pl.program_id(0)

    @pl.when(i == 0)
    def _():
        zero_ref[...] = jnp.zeros_like(zero_ref)

        def fill(j, n):
            @pl.when(zf_ref[j] > 0)
            def _():
                _row_copy(zero_ref, 0, hs_ref, pl.multiple_of(j * ZERO_ROWS, ZERO_ROWS), ZERO_ROWS, sem).start()
            return n + zf_ref[j]

        n_fill = lax.fori_loop(0, n_tiles * (MOE_ROWS // ZERO_ROWS), fill, jnp.int32(0))

        def drain(_, carry):
            _row_copy(zero_ref, 0, hs_ref, 0, ZERO_ROWS, sem).wait()
            return carry

        lax.fori_loop(0, n_fill, drain, 0)

    base = i * td

    for r in range(td):
        _row_copy(h_ref, r, hs_ref, pos_ref[base + r], 1, sem).start(priority=r % 2)
    for src, dst in zip(w_in, w_out):
        dst[...] = src[...].astype(dst.dtype)
    _row_copy(h_ref, 0, hs_ref, 0, td, sem).wait()


def _dispatch(h2x, pos, zero_fill, n_tiles, td, weights):
    t, w = h2x.shape
    steps = t // td
    flat = [a.reshape(-1, a.shape[-1]) for a in weights]
    assert all(a.shape[0] % (steps * 16) == 0 for a in flat)
    w_specs = [pl.BlockSpec((a.shape[0] // steps, a.shape[1]), lambda i, pos, zf: (i, 0)) for a in flat]
    outs = pl.pallas_call(
        functools.partial(_dispatch_kernel, td=td, n_tiles=n_tiles, n_weights=len(flat)),
        grid_spec=pltpu.PrefetchScalarGridSpec(
            num_scalar_prefetch=2,
            grid=(steps,),
            in_specs=[pl.BlockSpec((td, w), lambda i, pos, zf: (i, 0))] + w_specs,
            out_specs=[pl.BlockSpec(memory_space=pl.ANY)] + w_specs,
            scratch_shapes=[pltpu.VMEM((ZERO_ROWS, w), F32), pltpu.SemaphoreType.DMA(())]),
        out_shape=[jax.ShapeDtypeStruct((n_tiles * MOE_ROWS, w), F32)]
        + [jax.ShapeDtypeStruct(a.shape, BF16) for a in flat],
        compiler_params=_cparams(("arbitrary",)),
        name="dispatch",
    )(pos, zero_fill, h2x, *flat)
    return outs[0], [o.reshape(a.shape) for o, a in zip(outs[1:], weights)]


def _moe_kernel(elo_ref, ehi_ref, nu_ref, hs_ref, wg_lo, wu_lo, wd_lo, wg_hi, wu_hi, wd_hi, y_ref):
    j = pl.program_id(0)
    d = y_ref.shape[1]

    @pl.when(j < nu_ref[0])
    def _():
        h = hs_ref[:, 0:d].astype(BF16)
        ups = [(_dot(h, wg[0]), _dot(h, wu[0])) for wg, wu in ((wg_lo, wu_lo), (wg_hi, wu_hi))]
        hids = [((a * jax.nn.sigmoid(a)) * u * hs_ref[:, d + lane:d + lane + 1]).astype(BF16)
                for (a, u), lane in zip(ups, (ROUTE_W_LO, ROUTE_W_HI))]
        y_ref[...] = _dot(hids[0], wd_lo[0]) + _dot(hids[1], wd_hi[0])

    @pl.when(j >= nu_ref[0])
    def _():
        y_ref[...] = jnp.zeros_like(y_ref)


def _moe(hs, exp_lo, exp_hi, n_used, wg, wu, wd):
    n_tiles = hs.shape[0] // MOE_ROWS
    _, d, ff = wg.shape
    up = lambda sel: pl.BlockSpec((1, d, ff), lambda j, elo, ehi, nu: ((elo, ehi)[sel][j], 0, 0))
    down = lambda sel: pl.BlockSpec((1, ff, d), lambda j, elo, ehi, nu: ((elo, ehi)[sel][j], 0, 0))
    return pl.pallas_call(
        _moe_kernel,
        grid_spec=pltpu.PrefetchScalarGridSpec(
            num_scalar_prefetch=3,
            grid=(n_tiles,),
            in_specs=[pl.BlockSpec((MOE_ROWS, hs.shape[1]),
                                   lambda j, elo, ehi, nu: (jnp.minimum(j, nu[0] - 1), 0)),
                      up(0), up(0), down(0), up(1), up(1), down(1)],
            out_specs=pl.BlockSpec((MOE_ROWS, d), lambda j, elo, ehi, nu: (j, 0))),
        out_shape=jax.ShapeDtypeStruct((n_tiles * MOE_ROWS, d), F32),
        compiler_params=_cparams(("arbitrary",)),
        name="moe",
    )(exp_lo, exp_hi, n_used, hs, wg, wu, wd, wg, wu, wd)


def _final_kernel(pos_ref, x1_ref, mod_ref, gfin_ref, y_ref, o_ref, rows_ref, sem, *, tf):
    i, n = pl.program_id(0), pl.num_programs(0)

    def fetch(step, slot):
        for r in range(tf):
            _row_copy(y_ref, pos_ref[step * tf + r], rows_ref.at[slot], r, 1, sem.at[slot]).start(priority=r % 2)

    @pl.when(i == 0)
    def _():
        fetch(0, 0)

    @pl.when(i + 1 < n)
    def _():
        fetch(i + 1, (i + 1) % 2)

    slot = i % 2
    _row_copy(y_ref, 0, rows_ref.at[slot], 0, tf, sem.at[slot]).wait()
    x2 = x1_ref[...] + mod_ref[0, 5:6, :] * rows_ref[slot]
    o_ref[...] = _rms(x2, gfin_ref[...])


def _final(y, pos, x1, mod3, g_final, seq, tf):
    t, d = x1.shape
    per_b = seq // tf
    return pl.pallas_call(
        functools.partial(_final_kernel, tf=tf),
        grid_spec=pltpu.PrefetchScalarGridSpec(
            num_scalar_prefetch=1,
            grid=(t // tf,),
            in_specs=[pl.BlockSpec((tf, d), lambda i, pos: (i, 0)),
                      pl.BlockSpec((1,) + mod3.shape[1:], lambda i, pos: (i // per_b, 0, 0)),
                      pl.BlockSpec((1, d), lambda i, pos: (0, 0)),
                      pl.BlockSpec(memory_space=pl.ANY)],
            out_specs=pl.BlockSpec((tf, d), lambda i, pos: (i, 0)),
            scratch_shapes=[pltpu.VMEM((2, tf, d), F32), pltpu.SemaphoreType.DMA((2,))]),
        out_shape=jax.ShapeDtypeStruct((t, d), F32),
        compiler_params=_cparams(("arbitrary",)),
        name="final",
    )(pos, x1, mod3, g_final, y)


def _layout_weights(w_in, w_uq, w_uk, w_uv, heads, q_lora, kv_lora, d):
    hw = heads * HEAD_DIM
    cuts = np.cumsum([q_lora, kv_lora, ROPE_DIM, hw, hw, hw, d, d])
    w_qlat, w_kvlat, w_kr, w_qb, w_kb, w_vb, w_ga, w_gb = jnp.split(w_in, [int(v) for v in cuts[:-1]], axis=1)
    swap = np.concatenate([np.arange(ROPE_DIM // 2, ROPE_DIM), np.arange(ROPE_DIM // 2)])
    w_kr_sw = w_kr[:, swap]
    kr_blk = jnp.concatenate([w_kr, w_kr, w_kr_sw, w_kr_sw], axis=1)
    band_scale = HEAD_DIM ** -0.5 * LOG2_E
    wbig = jnp.concatenate([w_qlat, w_kvlat, kr_blk, w_qb * band_scale, w_kb, w_ga, w_gb], axis=1)

    def transposed_values(w, k):
        return jnp.pad(w.reshape(k, heads, HEAD_DIM).transpose(1, 2, 0),
                       ((0, 0), (0, VT_ROWS - HEAD_DIM), (0, 0))).reshape(heads * VT_ROWS, k)

    qk = HEAD_DIM + ROPE_DIM
    wq3 = w_uq.reshape(q_lora, heads, qk) * (qk ** -0.5 * LOG2_E)
    nope = wq3[:, :, :HEAD_DIM].reshape(q_lora, hw)
    rope = wq3[:, :, HEAD_DIM:]
    pad = jnp.zeros((q_lora, heads // 2, LANES - 2 * ROPE_DIM), w_uq.dtype)

    def lay(r):
        return jnp.concatenate([r.reshape(q_lora, heads // 2, 2 * ROPE_DIM), pad], axis=2).reshape(q_lora, hw)

    wq = jnp.concatenate([nope, lay(rope), lay(rope[:, :, swap])], axis=1)
    return (wbig.astype(BF16), wq.astype(BF16), w_uk.astype(BF16), transposed_values(w_uv, kv_lora).astype(BF16),
            transposed_values(w_vb, d).astype(BF16))


def kernel(x, c, positions, w_ada, b_ada, g_mix, w_in, g_q, w_uq, g_kv, w_uk, w_uv, rel_bias, w_oa, w_ob,
           w_out, g_ffn, w_rg, b_rg, w_re, b_re, w_gate, w_up, w_down, g_final):
    bsz, seq, d = x.shape
    depth = w_ada.shape[0]
    t = bsz * seq
    heads = rel_bias.shape[1]
    q_lora, kv_lora = g_q.shape[1], g_kv.shape[1]
    tq, tq_mla = TQ_BAND, TQ_MLA
    nblk = -(-LEFT_CHUNKS * CHUNK // tq) + 1
    assert seq % tq == 0 and seq % tq_mla == 0 and tq % CHUNK == 0 and heads % 2 == 0
    assert ROUTE_ROWS == EXPERTS_PER_GROUP and N_GROUPS <= EXPERTS_PER_GROUP

    x2 = x.reshape(t, d)
    pos2 = positions.reshape(bsz, 1, seq)
    out = x2
    for l in range(depth):
        mod3 = _ada(c, w_ada[l], b_ada[l]).reshape(bsz, -1, d)
        wbig, wq, wkv, wvt, wvbt = _layout_weights(w_in[l], w_uq[l], w_uk[l], w_uv[l], heads, q_lora, kv_lora, d)
        qa, ka, vt, qb, kb, vtb, ga, gb = _proj(
            x2, pos2, mod3, g_mix[l].reshape(1, d), wbig, g_q[l].reshape(1, -1), wq, g_kv[l].reshape(1, -1), wkv,
            wvt, wvbt, seq, tm=TM_PROJ)
        oa = _mla(qa, ka, vt, bsz, seq, tq_mla)
        ob = _band(qb, kb, vtb, _band_ring(rel_bias[l], tq, nblk), bsz, seq, tq, nblk,
                   tiles_per_step=BAND_TILES_PER_STEP)

        pad_g = EXPERTS_PER_GROUP - N_GROUPS
        n_route = (N_GROUPS + 1) * EXPERTS_PER_GROUP
        w_r = jnp.concatenate([w_rg[l], jnp.zeros((d, pad_g), F32), w_re[l],
                               jnp.zeros((d, LANES - n_route), F32)], axis=1).T
        b_r = jnp.concatenate([b_rg[l], jnp.zeros((pad_g,), F32), b_re[l],
                               jnp.zeros((LANES - n_route,), F32)]).reshape(LANES, 1)
        w_r_hi = w_r.astype(BF16)
        w_r_lo = (w_r - w_r_hi.astype(F32)).astype(BF16)
        x1, h2x, route_t, counts = _merge(oa, ob, ga, gb, x2, mod3, w_oa[l].astype(BF16), w_ob[l].astype(BF16),
                                          w_out[l].astype(BF16), g_ffn[l].reshape(1, d),
                                          jnp.concatenate([w_r_hi, w_r_lo], axis=0), b_r, seq, tm=TM_MERGE)
        n_tiles = t // MOE_ROWS + N_CLASSES
        pos = _plan(route_t, counts, tc=TC_PLAN).reshape(t)
        exp_lo, exp_hi, n_used, zero_fill = _tile_tables(counts, n_tiles)
        hs, expert_weights = _dispatch(h2x, pos, zero_fill, n_tiles, TD_DISPATCH, (w_gate[l], w_up[l], w_down[l]))
        y = _moe(hs, exp_lo, exp_hi, n_used, *expert_weights)
        assert l == depth - 1, "multi-layer stacks need an un-normalised residual output"
        out = _final(y, pos, x1, mod3, g_final.reshape(1, d), seq, tf=TF_FINAL)
    return out.reshape(bsz, seq, d)
```
